```python
import jax
import jax.numpy as jnp
from jax import lax
import numpy as np

D_MODEL = 2048
BATCH = 8
SEQ = 4096
DEPTH = 2

PLE_DIM = 256
ROPE_THETA = 500000.0
EPS = 1e-6
Q_BLOCK = 128

MLA_HEADS = 6
MLA_Q_LORA = 512
MLA_KV_LORA = 512
MLA_NOPE = 128
MLA_ROPE = 64
MLA_V = 128
MLA_QK = MLA_NOPE + MLA_ROPE
MLA_WIDTH = MLA_HEADS * MLA_V

CONV_CH = 512
CONV_K = 3

DSA_HEADS = 6
DSA_KV_HEADS = 2
DSA_GROUP = DSA_HEADS // DSA_KV_HEADS
DSA_HEAD_DIM = 128
DSA_WIDTH = DSA_HEADS * DSA_HEAD_DIM
DSA_ROT = DSA_HEAD_DIM // 4
IDX_HEADS = 16
IDX_DIM = 64
IDX_ROT = IDX_DIM // 4
TOPK_MAX = 256

MIX_WIDTH = MLA_WIDTH + CONV_CH + DSA_WIDTH

SPLIT_SIZES = (
    MLA_Q_LORA, MLA_KV_LORA, MLA_ROPE, MLA_WIDTH,
    CONV_CH, CONV_CH, CONV_CH, CONV_CH,
    DSA_HEADS * DSA_HEAD_DIM, DSA_KV_HEADS * DSA_HEAD_DIM,
    DSA_KV_HEADS * DSA_HEAD_DIM, DSA_WIDTH,
    IDX_HEADS * IDX_DIM, IDX_HEADS, IDX_DIM,
)
N_IN = sum(SPLIT_SIZES)

kernel_name = "hybrid_mla_conv_dsa_block"


def _split(z, sizes):
    idx = []
    acc = 0
    for s in sizes[:-1]:
        acc += s
        idx.append(acc)
    return jnp.split(z, idx, axis=-1)


def rms_norm(x, g):
    xf = x.astype(jnp.float32)
    y = xf * lax.rsqrt(jnp.mean(xf * xf, axis=-1, keepdims=True) + EPS)
    return (y * g.astype(jnp.float32)).astype(x.dtype)


def rope(x, pos):
    half = x.shape[-1] // 2
    inv = ROPE_THETA ** (-jnp.arange(half, dtype=jnp.float32) / half)
    ang = pos.astype(jnp.float32)[:, :, None, None] * inv
    cos, sin = jnp.cos(ang), jnp.sin(ang)
    xf = x.astype(jnp.float32)
    x1, x2 = xf[..., :half], xf[..., half:]
    return jnp.concatenate([x1 * cos - x2 * sin, x2 * cos + x1 * sin], axis=-1).astype(x.dtype)


def partial_rope(x, pos, rot):
    return jnp.concatenate([rope(x[..., :rot], pos), x[..., rot:]], axis=-1)


def _to_blocks(a):
    b, s = a.shape[:2]
    return jnp.moveaxis(a.reshape((b, s // Q_BLOCK, Q_BLOCK) + a.shape[2:]), 1, 0)


def _from_blocks(a):
    nb, b, q = a.shape[:3]
    return jnp.moveaxis(a, 0, 1).reshape((b, nb * q) + a.shape[3:])


def mla_attention(q, k, v):
    s_len = q.shape[1]
    scale = q.shape[-1] ** -0.5
    kpos = jnp.arange(s_len)

    def block(args):
        qb, start = args
        s = jnp.einsum("bqhd,bkhd->bhqk", qb, k).astype(jnp.float32) * scale
        qpos = start + jnp.arange(Q_BLOCK)
        s = jnp.where(kpos[None, :] <= qpos[:, None], s, -jnp.inf)
        pr = jax.nn.softmax(s, axis=-1).astype(v.dtype)
        return jnp.einsum("bhqk,bkhd->bqhd", pr, v)

    starts = jnp.arange(s_len // Q_BLOCK) * Q_BLOCK
    return _from_blocks(lax.map(block, (_to_blocks(q), starts)))


def dsa_attention(q, k, v, iq, iw, ik, k_top):
    b, s_len = q.shape[:2]
    scale = q.shape[-1] ** -0.5
    kpos = jnp.arange(s_len)
    bidx = jnp.arange(b)[:, None, None]
    ikf = ik.astype(jnp.float32)

    def block(args):
        qb, iqb, iwb, start = args
        qpos = start + jnp.arange(Q_BLOCK)
        causal = kpos[None, :] <= qpos[:, None]
        logits = jnp.einsum("bqhd,bkd->bqhk", iqb.astype(jnp.float32), ikf)
        score = jnp.einsum("bqh,bqhk->bqk", iwb.astype(jnp.float32), jax.nn.relu(logits))
        score = jnp.where(causal[None], score, -jnp.inf)
        _, idx = lax.top_k(score, k_top)
        valid = idx <= qpos[None, :, None]
        ks = k[bidx, idx]
        vs = v[bidx, idx]
        s = jnp.einsum("bqgrd,bqkgd->bqgrk", qb, ks).astype(jnp.float32) * scale
        s = jnp.where(valid[:, :, None, None, :], s, -jnp.inf)
        pr = jax.nn.softmax(s, axis=-1).astype(v.dtype)
        return jnp.einsum("bqgrk,bqkgd->bqgrd", pr, vs)

    starts = jnp.arange(s_len // Q_BLOCK) * Q_BLOCK
    return _from_blocks(lax.map(block, (_to_blocks(q), _to_blocks(iq), _to_blocks(iw), starts)))


def hybrid_layer(h, p_i, positions, norm_in, w_in, mla_gq, mla_w_uq, mla_gkv, mla_w_ukv,
                 mla_qn, mla_kn, conv_w, dsa_qn, dsa_kn, w_out, ple_norm, ple_w_gate, ple_w_proj):
    b, s_len, _ = h.shape
    a = rms_norm(h, norm_in)
    z = a @ w_in
    (cq, ckv, kpe, mla_z, cx, cb, cc, conv_z,
     dq, dk, dv, dsa_z, iq, iw, ik) = _split(z, SPLIT_SIZES)

    q = (rms_norm(cq, mla_gq) @ mla_w_uq).reshape(b, s_len, MLA_HEADS, MLA_QK)
    kv = (rms_norm(ckv, mla_gkv) @ mla_w_ukv).reshape(b, s_len, MLA_HEADS, MLA_NOPE + MLA_V)
    k_nope, v = kv[..., :MLA_NOPE], kv[..., MLA_NOPE:]
    k_pe = jnp.broadcast_to(kpe[:, :, None, :], (b, s_len, MLA_HEADS, MLA_ROPE))
    k = jnp.concatenate([k_nope, k_pe], axis=-1)
    q = rms_norm(q, mla_qn)
    k = rms_norm(k, mla_kn)
    q = jnp.concatenate([q[..., :MLA_NOPE], rope(q[..., MLA_NOPE:], positions)], axis=-1)
    k = jnp.concatenate([k[..., :MLA_NOPE], rope(k[..., MLA_NOPE:], positions)], axis=-1)
    y_a = mla_attention(q, k, v).reshape(b, s_len, MLA_WIDTH) * jax.nn.silu(mla_z)

    u = cc * cx
    up = jnp.pad(u, ((0, 0), (CONV_K - 1, 0), (0, 0)))
    conv = (conv_w[0] * up[:, 0:s_len] + conv_w[1] * up[:, 1:s_len + 1]
            + conv_w[2] * up[:, 2:s_len + 2])
    y_b = cb * conv * jax.nn.silu(conv_z)

    dq = partial_rope(rms_norm(dq.reshape(b, s_len, DSA_HEADS, DSA_HEAD_DIM), dsa_qn), positions, DSA_ROT)
    dq = dq.reshape(b, s_len, DSA_KV_HEADS, DSA_GROUP, DSA_HEAD_DIM)
    dk = partial_rope(rms_norm(dk.reshape(b, s_len, DSA_KV_HEADS, DSA_HEAD_DIM), dsa_kn), positions, DSA_ROT)
    dv = dv.reshape(b, s_len, DSA_KV_HEADS, DSA_HEAD_DIM)
    iq = partial_rope(iq.reshape(b, s_len, IDX_HEADS, IDX_DIM), positions, IDX_ROT) * (IDX_DIM ** -0.5)
    ik = partial_rope(ik[:, :, None, :], positions, IDX_ROT)[:, :, 0, :]
    iw = iw * (IDX_HEADS ** -0.5)
    k_top = min(TOPK_MAX, s_len // 4)
    y_c = dsa_attention(dq, dk, dv, iq, iw, ik, k_top).reshape(b, s_len, DSA_WIDTH) * jax.nn.silu(dsa_z)

    mix = jnp.concatenate([y_a, y_b, y_c], axis=-1)
    h = h + mix @ w_out

    gate = jax.nn.sigmoid(rms_norm(h, ple_norm) @ ple_w_gate)
    return h + gate * (p_i @ ple_w_proj)


def setup_inputs(seed: int = 0) -> dict:
    key = jax.random.key(seed)
    ks = jax.random.split(key, 20)

    def dense(k, shape, fan_in):
        return jax.random.normal(k, shape, jnp.float32) * (fan_in ** -0.5)

    def gain(k, shape):
        return 1.0 + 0.05 * jax.random.normal(k, shape, jnp.float32)

    x = jax.random.normal(ks[0], (BATCH, SEQ, D_MODEL), jnp.float32)
    p = jax.random.normal(ks[1], (DEPTH, BATCH, SEQ, PLE_DIM), jnp.float32)
    offs = jax.random.randint(ks[2], (BATCH, 1), 0, 1024, jnp.int32)
    positions = offs + jnp.arange(SEQ, dtype=jnp.int32)[None, :]
    return {
        "x": x,
        "p": p,
        "positions": positions,
        "norm_in": gain(ks[3], (DEPTH, D_MODEL)),
        "w_in": dense(ks[4], (DEPTH, D_MODEL, N_IN), D_MODEL),
        "mla_gq": gain(ks[5], (DEPTH, MLA_Q_LORA)),
        "mla_w_uq": dense(ks[6], (DEPTH, MLA_Q_LORA, MLA_HEADS * MLA_QK), MLA_Q_LORA),
        "mla_gkv": gain(ks[7], (DEPTH, MLA_KV_LORA)),
        "mla_w_ukv": dense(ks[8], (DEPTH, MLA_KV_LORA, MLA_HEADS * (MLA_NOPE + MLA_V)), MLA_KV_LORA),
        "mla_qn": gain(ks[9], (DEPTH, MLA_QK)),
        "mla_kn": gain(ks[10], (DEPTH, MLA_QK)),
        "conv_w": dense(ks[11], (DEPTH, CONV_K, CONV_CH), CONV_K),
        "dsa_qn": gain(ks[12], (DEPTH, DSA_HEAD_DIM)),
        "dsa_kn": gain(ks[13], (DEPTH, DSA_HEAD_DIM)),
        "w_out": dense(ks[14], (DEPTH, MIX_WIDTH, D_MODEL), MIX_WIDTH),
        "ple_norm": gain(ks[15], (DEPTH, D_MODEL)),
        "ple_w_gate": dense(ks[16], (DEPTH, D_MODEL, D_MODEL), D_MODEL),
        "ple_w_proj": dense(ks[17], (DEPTH, PLE_DIM, D_MODEL), PLE_DIM),
    }


def reference(x, p, positions, norm_in, w_in, mla_gq, mla_w_uq, mla_gkv, mla_w_ukv, mla_qn, mla_kn,
              conv_w, dsa_qn, dsa_kn, w_out, ple_norm, ple_w_gate, ple_w_proj):
    h = x
    for i in range(DEPTH):
        h = hybrid_layer(h, p[i], positions, norm_in[i], w_in[i], mla_gq[i], mla_w_uq[i], mla_gkv[i],
                         mla_w_ukv[i], mla_qn[i], mla_kn[i], conv_w[i], dsa_qn[i], dsa_kn[i], w_out[i],
                         ple_norm[i], ple_w_gate[i], ple_w_proj[i])
    return h
```

```python
import functools

import jax
import jax.numpy as jnp
from jax import lax
from jax.experimental import pallas as pl
from jax.experimental.pallas import tpu as pltpu

D_MODEL = 2048
PLE_DIM = 256
ROPE_THETA = 500000.0
EPS = 1e-6

MLA_HEADS = 6
MLA_Q_LORA = 512
MLA_KV_LORA = 512
MLA_NOPE = 128
MLA_ROPE = 64
MLA_V = 128
MLA_QK = MLA_NOPE + MLA_ROPE
MLA_WIDTH = MLA_HEADS * MLA_V

CONV_CH = 512
CONV_K = 3

DSA_HEADS = 6
DSA_KV_HEADS = 2
DSA_GROUP = DSA_HEADS // DSA_KV_HEADS
DSA_HEAD_DIM = 128
DSA_WIDTH = DSA_HEADS * DSA_HEAD_DIM
DSA_ROT = DSA_HEAD_DIM // 4
IDX_HEADS = 16
IDX_DIM = 64
IDX_ROT = IDX_DIM // 4
TOPK_MAX = 256

LANES = 128

_O_CQ, _O_CKV, _O_KPE, _O_MLAZ = 0, 512, 1024, 1088
_O_CX, _O_CB, _O_CC, _O_CONVZ = 1856, 2368, 2880, 3392
_O_DQ, _O_DK, _O_DV, _O_DSAZ = 3904, 4672, 4928, 5184
_O_IQ, _O_IW, _O_IK, _N_IN = 5952, 6976, 6992, 7056

Z_CQ, Z_CKV = 0, 512
Z_CX, Z_CB, Z_CC, Z_CONVZ = 1024, 1536, 2048, 2560
Z_MLAZ, Z_DQ, Z_DSAZ = 3072, 3840, 4608
Z_DK, Z_DV = 5376, 5632
Z_KPE_IK, Z_IW, Z_IQ = 5888, 6016, 6144
Z_WIDTH = 7168

INT_MIN = -(2 ** 31)
NEG_BIG = -1e30

VMEM_LIMIT = 56 * 1024 * 1024


def _cparams(sem):
    return pltpu.CompilerParams(dimension_semantics=sem, vmem_limit_bytes=VMEM_LIMIT)


def _dot(a, b):
    return jnp.dot(a, b, preferred_element_type=jnp.float32)


def _dot_nt(a, b):
    return lax.dot_general(a, b, (((1,), (1,)), ((), ())), preferred_element_type=jnp.float32)


def _roll(x, shift):
    return pltpu.roll(x, shift, 1)


def _inproj_kernel(x_ref, g_ref, w_ref, o_ref, a_ref):
    @pl.when(pl.program_id(1) == 0)
    def _():
        x = x_ref[...]
        r = lax.rsqrt(jnp.mean(x * x, axis=-1, keepdims=True) + EPS)
        a_ref[...] = (x * r * g_ref[...]).astype(jnp.bfloat16)

    o_ref[...] = _dot(a_ref[...], w_ref[...]).astype(o_ref.dtype)


def _inproj(h, g, w, tm, tn):
    t, d = h.shape
    n = w.shape[1]
    return pl.pallas_call(
        _inproj_kernel,
        out_shape=jax.ShapeDtypeStruct((t, n), jnp.bfloat16),
        grid=(t // tm, n // tn),
        in_specs=[
            pl.BlockSpec((tm, d), lambda i, j: (i, 0)),
            pl.BlockSpec((1, d), lambda i, j: (0, 0)),
            pl.BlockSpec((d, tn), lambda i, j: (0, j)),
        ],
        out_specs=pl.BlockSpec((tm, tn), lambda i, j: (i, j)),
        scratch_shapes=[pltpu.VMEM((tm, d), jnp.bfloat16)],
        compiler_params=_cparams(("parallel", "arbitrary")),
        name="inproj",
    )(h, g, w)


TB_CSQ = 0
TB_C46, TB_S1A, TB_S2A, TB_S1B, TB_S2B = 1, 2, 3, 4, 5
TB_CD, TB_S1D, TB_S2D = 6, 7, 8
TB_CI, TB_S1I, TB_S2I = 9, 10, 11
N_TABLES = 12


def _prep_kernel(cq_ref, ckv_ref, t46_ref, dq_ref, dk_ref, iq_ref, tab_ref,
                 glq_ref, glkv_ref, wuq_ref, wukv_ref, gq_ref, gkn_ref, g46_ref, gdq_ref, gdk_ref,
                 qm_ref, km_ref, vm_ref, qd_ref, kd_ref, iqp_ref, ikd_ref):
    f32, bf16 = jnp.float32, jnp.bfloat16
    tm = cq_ref.shape[0]
    lane = lax.broadcasted_iota(jnp.int32, (tm, LANES), 1)
    low = lane < 64

    def rms(x, n):
        return lax.rsqrt(jnp.sum(x * x, axis=-1, keepdims=True) * (1.0 / n) + EPS)

    cq = cq_ref[...].astype(f32)
    aq = (cq * rms(cq, MLA_Q_LORA) * glq_ref[...]).astype(bf16)
    qx = _dot(aq, wuq_ref[...])
    csq = tab_ref[TB_CSQ]
    for h in range(MLA_HEADS):
        nope = qx[:, 256 * h:256 * h + 128]
        pair = qx[:, 256 * h + 128:256 * h + 256]
        ss = jnp.sum(nope * nope, axis=-1, keepdims=True) + 0.5 * jnp.sum(pair * pair, axis=-1, keepdims=True)
        r = lax.rsqrt(ss * (1.0 / MLA_QK) + EPS)
        qm_ref[h, :, 0:128] = (nope * r * gq_ref[:, 256 * h:256 * h + 128]).astype(bf16)
        qm_ref[h, :, 128:256] = (pair * r * gq_ref[:, 256 * h + 128:256 * h + 256] * csq).astype(bf16)

    t46 = t46_ref[...].astype(f32)
    ss_pe = jnp.sum(jnp.where(low, t46 * t46, 0.0), axis=-1, keepdims=True)
    y46 = t46 * g46_ref[...]
    r46 = (y46 * tab_ref[TB_C46]
           + _roll(y46, 96) * tab_ref[TB_S1A] + _roll(y46, 32) * tab_ref[TB_S2A]
           + _roll(y46, 120) * tab_ref[TB_S1B] + _roll(y46, 8) * tab_ref[TB_S2B])
    sw = _roll(r46, 64)
    kpe2 = jnp.where(low, r46, sw)
    ikd_ref[...] = jnp.where(low, sw, r46).astype(bf16)

    ckv = ckv_ref[...].astype(f32)
    akv = (ckv * rms(ckv, MLA_KV_LORA) * glkv_ref[...]).astype(bf16)
    kvx = _dot(akv, wukv_ref[...])
    for h in range(MLA_HEADS):
        kn = kvx[:, 256 * h:256 * h + 128]
        ss = jnp.sum(kn * kn, axis=-1, keepdims=True) + ss_pe
        r = lax.rsqrt(ss * (1.0 / MLA_QK) + EPS)
        km_ref[h, :, 0:128] = (kn * r * gkn_ref[...]).astype(bf16)
        km_ref[h, :, 128:256] = (kpe2 * r).astype(bf16)
        vm_ref[h] = kvx[:, 256 * h + 128:256 * h + 256].astype(bf16)

    cd, s1d, s2d = tab_ref[TB_CD], tab_ref[TB_S1D], tab_ref[TB_S2D]

    def dsa_head(x, g):
        y = x * rms(x, DSA_HEAD_DIM) * g
        return y * cd + _roll(y, 112) * s1d + _roll(y, 16) * s2d

    for h in range(DSA_HEADS):
        x = dq_ref[:, 128 * h:128 * h + 128].astype(f32)
        qd_ref[h] = dsa_head(x, gdq_ref[...]).astype(bf16)
    for g in range(DSA_KV_HEADS):
        x = dk_ref[:, 128 * g:128 * g + 128].astype(f32)
        kd_ref[g] = dsa_head(x, gdk_ref[...]).astype(bf16)

    ci, s1i, s2i = tab_ref[TB_CI], tab_ref[TB_S1I], tab_ref[TB_S2I]
    for j in range(IDX_HEADS // 2):
        x = iq_ref[:, 128 * j:128 * j + 128].astype(f32)
        y = x * ci + _roll(x, 120) * s1i + _roll(x, 8) * s2i
        iqp_ref[:, 256 * j:256 * j + 128] = jnp.where(low, y, 0.0).astype(bf16)
        iqp_ref[:, 256 * j + 128:256 * j + 256] = jnp.where(low, 0.0, y).astype(bf16)


def _prep(z, tabs, glq, glkv, wuq, wukv, gq, gkn, g46, gdq, gdk, tm):
    t = z.shape[0]
    bf16 = jnp.bfloat16
    row = lambda w, c: pl.BlockSpec((tm, w), lambda i, c=c: (i, c))
    full = lambda a: pl.BlockSpec(a.shape, lambda i: (0,) * a.ndim)
    heads = lambda n, w: pl.BlockSpec((n, tm, w), lambda i: (0, i, 0))
    return pl.pallas_call(
        _prep_kernel,
        out_shape=(
            jax.ShapeDtypeStruct((MLA_HEADS, t, 256), bf16),
            jax.ShapeDtypeStruct((MLA_HEADS, t, 256), bf16),
            jax.ShapeDtypeStruct((MLA_HEADS, t, 128), bf16),
            jax.ShapeDtypeStruct((DSA_HEADS, t, 128), bf16),
            jax.ShapeDtypeStruct((DSA_KV_HEADS, t, 128), bf16),
            jax.ShapeDtypeStruct((t, IDX_HEADS * 128), bf16),
            jax.ShapeDtypeStruct((t, 128), bf16),
        ),
        grid=(t // tm,),
        in_specs=[
            row(512, Z_CQ // 512), row(512, Z_CKV // 512), row(128, Z_KPE_IK // 128),
            row(768, Z_DQ // 768), row(256, Z_DK // 256), row(1024, Z_IQ // 1024),
            pl.BlockSpec((N_TABLES, tm, LANES), lambda i: (0, i, 0)),
            full(glq), full(glkv), full(wuq), full(wukv), full(gq), full(gkn), full(g46), full(gdq), full(gdk),
        ],
        out_specs=(
            heads(MLA_HEADS, 256), heads(MLA_HEADS, 256), heads(MLA_HEADS, 128),
            heads(DSA_HEADS, 128), heads(DSA_KV_HEADS, 128),
            pl.BlockSpec((tm, IDX_HEADS * 128), lambda i: (i, 0)),
            pl.BlockSpec((tm, 128), lambda i: (i, 0)),
        ),
        compiler_params=_cparams(("parallel",)),
        name="prep",
    )(z, z, z, z, z, z, tabs, glq, glkv, wuq, wukv, gq, gkn, g46, gdq, gdk)


def _silu(x):
    return x / (1.0 + jnp.exp(-x))


def _conv_kernel(cx_ref, cb_ref, cc_ref, cz_ref, hx_ref, hc_ref, w_ref, o_ref, *, tiles_per_seq):
    f32 = jnp.float32
    tm, ch = cx_ref.shape
    u = cc_ref[...].astype(f32) * cx_ref[...].astype(f32)
    keep = (pl.program_id(0) % tiles_per_seq != 0).astype(f32)
    hu = hc_ref[...].astype(f32) * hx_ref[...].astype(f32) * keep
    row = lax.broadcasted_iota(jnp.int32, (tm, ch), 0)
    u1 = jnp.where(row == 0, hu[7:8, :], pltpu.roll(u, 1, 0))
    u2 = jnp.where(row == 0, hu[6:7, :], jnp.where(row == 1, hu[7:8, :], pltpu.roll(u, 2, 0)))
    conv = w_ref[0:1, :] * u2 + w_ref[1:2, :] * u1 + w_ref[2:3, :] * u
    o_ref[...] = (cb_ref[...].astype(f32) * conv * _silu(cz_ref[...].astype(f32))).astype(o_ref.dtype)


def _conv(z, w, seq, tm):
    t = z.shape[0]
    col = lambda c: pl.BlockSpec((tm, CONV_CH), lambda i, c=c: (i, c))
    halo = lambda c: pl.BlockSpec((8, CONV_CH), lambda i, c=c: (jnp.maximum(i * (tm // 8) - 1, 0), c))
    return pl.pallas_call(
        functools.partial(_conv_kernel, tiles_per_seq=seq // tm),
        out_shape=jax.ShapeDtypeStruct((t, CONV_CH), jnp.bfloat16),
        grid=(t // tm,),
        in_specs=[col(Z_CX // CONV_CH), col(Z_CB // CONV_CH), col(Z_CC // CONV_CH), col(Z_CONVZ // CONV_CH),
                  halo(Z_CX // CONV_CH), halo(Z_CC // CONV_CH),
                  pl.BlockSpec((8, CONV_CH), lambda i: (0, 0))],
        out_specs=pl.BlockSpec((tm, CONV_CH), lambda i: (i, 0)),
        compiler_params=_cparams(("parallel",)),
        name="conv",
    )(z, z, z, z, z, z, w)


def _mla_kernel(q_ref, k_ref, v_ref, zg_ref, o_ref, m_ref, l_ref, acc_ref):
    f32 = jnp.float32
    tq = q_ref.shape[1]
    qi = pl.program_id(2)
    q = q_ref[0]
    m_ref[...] = jnp.full(m_ref.shape, NEG_BIG, f32)
    l_ref[...] = jnp.zeros(l_ref.shape, f32)
    acc_ref[...] = jnp.zeros(acc_ref.shape, f32)

    def step(ki, masked):
        k = k_ref[0, pl.ds(pl.multiple_of(ki * tq, tq), tq), :]
        v = v_ref[0, pl.ds(pl.multiple_of(ki * tq, tq), tq), :]
        s = _dot_nt(q, k)
        if masked:
            r = lax.broadcasted_iota(jnp.int32, s.shape, 0)
            c = lax.broadcasted_iota(jnp.int32, s.shape, 1)
            s = jnp.where(c <= r, s, 2.0 * NEG_BIG)
        m_prev = m_ref[...]
        m_new = jnp.maximum(m_prev, jnp.max(s, axis=-1, keepdims=True))
        p = jnp.exp(s - m_new)
        alpha = jnp.exp(m_prev - m_new)
        l_ref[...] = alpha * l_ref[...] + jnp.sum(p, axis=-1, keepdims=True)
        acc_ref[...] = alpha * acc_ref[...] + _dot(p.astype(jnp.bfloat16), v)
        m_ref[...] = m_new

    def body(ki, c):
        step(ki, False)
        return c

    lax.fori_loop(0, qi, body, 0)
    step(qi, True)
    o = acc_ref[...] / l_ref[...]
    o_ref[...] = (o * _silu(zg_ref[...].astype(f32))).astype(o_ref.dtype)


def _mla(qm, km, vm, z, batch, seq, tq):
    t = z.shape[0]
    nq = seq // tq
    return pl.pallas_call(
        _mla_kernel,
        out_shape=jax.ShapeDtypeStruct((t, MLA_WIDTH), jnp.bfloat16),
        grid=(batch, MLA_HEADS, nq),
        in_specs=[
            pl.BlockSpec((1, tq, 256), lambda b, h, i: (h, b * nq + i, 0)),
            pl.BlockSpec((1, seq, 256), lambda b, h, i: (h, b, 0)),
            pl.BlockSpec((1, seq, 128), lambda b, h, i: (h, b, 0)),
            pl.BlockSpec((tq, 128), lambda b, h, i: (b * nq + i, Z_MLAZ // 128 + h)),
        ],
        out_specs=pl.BlockSpec((tq, 128), lambda b, h, i: (b * nq + i, h)),
        scratch_shapes=[pltpu.VMEM((tq, 1), jnp.float32), pltpu.VMEM((tq, 1), jnp.float32),
                        pltpu.VMEM((tq, 128), jnp.float32)],
        compiler_params=_cparams(("parallel", "parallel", "arbitrary")),
        name="mla",
    )(qm, km, vm, z)


KC = 256


def _dsa_kernel(iq_ref, iw_ref, ik_ref, q_ref, k_ref, v0_ref, v1_ref, zg_ref, o_ref,
                key_ref, wb_ref, m_ref, l_ref, acc_ref, *, k_top):
    f32, i32, bf16 = jnp.float32, jnp.int32, jnp.bfloat16
    tq = iq_ref.shape[0]
    i = pl.program_id(1)
    n_chunks = ((i + 1) * tq + KC - 1) // KC
    q_pos = i * tq + lax.broadcasted_iota(i32, (tq, KC), 0)
    lane_kc = lax.broadcasted_iota(i32, (tq, KC), 1)

    w = iw_ref[...].astype(f32) * (IDX_HEADS ** -0.5)
    for h in range(IDX_HEADS):
        wb_ref[h] = jnp.broadcast_to(w[:, h:h + 1], (tq, KC))

    def score_chunk(c, carry):
        start = pl.multiple_of(c * KC, KC)
        ikc = ik_ref[pl.ds(start, KC), :]
        acc = jnp.zeros((tq, KC), f32)
        for h in range(IDX_HEADS):
            lg = _dot_nt(iq_ref[:, 128 * h:128 * h + 128], ikc)
            acc = acc + wb_ref[h] * jnp.maximum(lg, 0.0)
        bits = lax.bitcast_convert_type(acc, i32)
        key = jnp.where(bits < 0, bits ^ jnp.int32(0x7FFFFFFF), bits)
        key_ref[:, pl.ds(start, KC)] = jnp.where(start + lane_kc <= q_pos, key, jnp.int32(INT_MIN))
        return carry

    lax.fori_loop(0, n_chunks, score_chunk, 0)

    def count_ge(cand):
        def body(c, cnt):
            start = pl.multiple_of(c * KC, KC)
            for j in range(KC // LANES):
                kt = key_ref[:, pl.ds(start + j * LANES, LANES)]
                cnt = cnt + jnp.where(kt >= cand, 1, 0)
            return cnt
        cnt = lax.fori_loop(0, n_chunks, body, jnp.zeros((tq, LANES), i32))
        return jnp.broadcast_to(jnp.sum(cnt, axis=-1, keepdims=True), (tq, LANES))

    def bit_step(it, r):
        cand = r + lax.shift_left(jnp.int32(1), 31 - it)
        return jnp.where(count_ge(cand) >= k_top, cand, r)

    thr = lax.fori_loop(0, 32, bit_step, jnp.full((tq, LANES), INT_MIN, i32))

    n_ge = count_ge(thr)
    has_excess = jnp.max(n_ge) > k_top

    @pl.when(has_excess)
    def _():
        def count_gt_body(c, cnt):
            start = pl.multiple_of(c * KC, KC)
            for j in range(KC // LANES):
                kt = key_ref[:, pl.ds(start + j * LANES, LANES)]
                cnt = cnt + jnp.where(kt > thr, 1, 0)
            return cnt
        n_gt = lax.fori_loop(0, n_chunks, count_gt_body, jnp.zeros((tq, LANES), i32))
        need = k_top - jnp.broadcast_to(jnp.sum(n_gt, axis=-1, keepdims=True), (tq, LANES))
        lane = lax.broadcasted_iota(i32, (tq, LANES), 1)

        def count_tie_below(bound):
            def body(c, cnt):
                start = pl.multiple_of(c * KC, KC)
                for j in range(KC // LANES):
                    kt = key_ref[:, pl.ds(start + j * LANES, LANES)]
                    idx = start + j * LANES + lane
                    cnt = cnt + jnp.where(kt == thr, jnp.where(idx < bound, 1, 0), 0)
                return cnt
            cnt = lax.fori_loop(0, n_chunks, body, jnp.zeros((tq, LANES), i32))
            return jnp.broadcast_to(jnp.sum(cnt, axis=-1, keepdims=True), (tq, LANES))

        def idx_step(it, bound):
            cand = bound + lax.shift_left(jnp.int32(1), 30 - it)
            return jnp.where(count_tie_below(cand) <= need, cand, bound)

        bound = lax.fori_loop(0, 31, idx_step, jnp.zeros((tq, LANES), i32))

        def demote(c, carry):
            start = pl.multiple_of(c * KC, KC)
            for j in range(KC // LANES):
                sl = pl.ds(start + j * LANES, LANES)
                kt = key_ref[:, sl]
                idx = start + j * LANES + lane
                drop = jnp.where(kt == thr, jnp.where(idx >= bound, 1, 0), 0)
                key_ref[:, sl] = jnp.where(drop == 1, jnp.int32(INT_MIN), kt)
            return carry

        lax.fori_loop(0, n_chunks, demote, 0)

    thr_sel = jnp.maximum(thr, jnp.int32(INT_MIN + 1))
    thr_kc = jnp.concatenate([thr_sel] * (KC // LANES), axis=1)

    m_ref[...] = jnp.full(m_ref.shape, NEG_BIG, f32)
    l_ref[...] = jnp.zeros(l_ref.shape, f32)
    acc_ref[...] = jnp.zeros(acc_ref.shape, f32)
    v_refs = (v0_ref, v1_ref)

    def attn_chunk(c, carry):
        start = pl.multiple_of(c * KC, KC)
        bias = jnp.where(key_ref[:, pl.ds(start, KC)] >= thr_kc, 0.0, 2.0 * NEG_BIG)
        bias3 = jnp.concatenate([bias] * DSA_GROUP, axis=0)
        for g in range(DSA_KV_HEADS):
            qg = q_ref[DSA_GROUP * g:DSA_GROUP * (g + 1)].reshape(DSA_GROUP * tq, DSA_HEAD_DIM)
            kc = k_ref[g, pl.ds(start, KC), :]
            vc = v_refs[g][pl.ds(start, KC), :]
            s = _dot_nt(qg, kc) + bias3
            m_prev = m_ref[g]
            m_new = jnp.maximum(m_prev, jnp.max(s, axis=-1, keepdims=True))
            p = jnp.exp(s - m_new)
            alpha = jnp.exp(m_prev - m_new)
            l_ref[g] = alpha * l_ref[g] + jnp.sum(p, axis=-1, keepdims=True)
            acc_ref[g] = alpha * acc_ref[g] + _dot(p.astype(bf16), vc)
            m_ref[g] = m_new
        return carry

    lax.fori_loop(0, n_chunks, attn_chunk, 0)

    for g in range(DSA_KV_HEADS):
        o = acc_ref[g] / l_ref[g]
        for r in range(DSA_GROUP):
            hd = DSA_GROUP * g + r
            zg = zg_ref[:, 128 * hd:128 * hd + 128].astype(f32)
            o_ref[:, 128 * hd:128 * hd + 128] = (o[r * tq:(r + 1) * tq] * _silu(zg)).astype(o_ref.dtype)


def _dsa(iqp, ikd, qd, kd, z, batch, seq, tq, k_top):
    t = z.shape[0]
    nq = seq // tq
    f32 = jnp.float32
    return pl.pallas_call(
        functools.partial(_dsa_kernel, k_top=k_top),
        out_shape=jax.ShapeDtypeStruct((t, DSA_WIDTH), jnp.bfloat16),
        grid=(batch, nq),
        in_specs=[
            pl.BlockSpec((tq, IDX_HEADS * 128), lambda b, i: (b * nq + i, 0)),
            pl.BlockSpec((tq, 128), lambda b, i: (b * nq + i, Z_IW // 128)),
            pl.BlockSpec((seq, 128), lambda b, i: (b, 0)),
            pl.BlockSpec((DSA_HEADS, tq, 128), lambda b, i: (0, b * nq + i, 0)),
            pl.BlockSpec((DSA_KV_HEADS, seq, 128), lambda b, i: (0, b, 0)),
            pl.BlockSpec((seq, 128), lambda b, i: (b, Z_DV // 128)),
            pl.BlockSpec((seq, 128), lambda b, i: (b, Z_DV // 128 + 1)),
            pl.BlockSpec((tq, DSA_WIDTH), lambda b, i: (b * nq + i, Z_DSAZ // DSA_WIDTH)),
        ],
        out_specs=pl.BlockSpec((tq, DSA_WIDTH), lambda b, i: (b * nq + i, 0)),
        scratch_shapes=[
            pltpu.VMEM((tq, seq), jnp.int32),
            pltpu.VMEM((IDX_HEADS, tq, KC), f32),
            pltpu.VMEM((DSA_KV_HEADS, DSA_GROUP * tq, 1), f32),
            pltpu.VMEM((DSA_KV_HEADS, DSA_GROUP * tq, 1), f32),
            pltpu.VMEM((DSA_KV_HEADS, DSA_GROUP * tq, DSA_HEAD_DIM), f32),
        ],
        compiler_params=_cparams(("parallel", "arbitrary")),
        name="dsa",
    )(iqp, z, ikd, qd, kd, z, z, z)


def _outproj_kernel(h_ref, ya_ref, yb_ref, yc_ref, wa_ref, wb_ref, wc_ref, o_ref):
    acc = _dot(ya_ref[...], wa_ref[...])
    acc = acc + _dot(yb_ref[...], wb_ref[...])
    acc = acc + _dot(yc_ref[...], wc_ref[...])
    o_ref[...] = h_ref[...] + acc


def _outproj(h, ya, yb, yc, wa, wb, wc, tm, tn):
    t, d = h.shape
    rows = lambda a: pl.BlockSpec((tm, a.shape[1]), lambda i, j: (i, 0))
    cols = lambda a: pl.BlockSpec((a.shape[0], tn), lambda i, j: (0, j))
    return pl.pallas_call(
        _outproj_kernel,
        out_shape=jax.ShapeDtypeStruct((t, d), jnp.float32),
        grid=(t // tm, d // tn),
        in_specs=[pl.BlockSpec((tm, tn), lambda i, j: (i, j)), rows(ya), rows(yb), rows(yc),
                  cols(wa), cols(wb), cols(wc)],
        out_specs=pl.BlockSpec((tm, tn), lambda i, j: (i, j)),
        compiler_params=_cparams(("parallel", "arbitrary")),
        name="outproj",
    )(h, ya, yb, yc, wa, wb, wc)


def _ple_kernel(x_ref, xc_ref, p_ref, g_ref, wg_ref, wp_ref, o_ref, a_ref, pb_ref):
    @pl.when(pl.program_id(1) == 0)
    def _():
        x = x_ref[...]
        r = lax.rsqrt(jnp.mean(x * x, axis=-1, keepdims=True) + EPS)
        a_ref[...] = (x * r * g_ref[...]).astype(jnp.bfloat16)
        pb_ref[...] = p_ref[...].astype(jnp.bfloat16)

    gate = _dot(a_ref[...], wg_ref[...])
    gate = 1.0 / (1.0 + jnp.exp(-gate))
    o_ref[...] = xc_ref[...] + gate * _dot(pb_ref[...], wp_ref[...])


def _ple(h, p, g, wg, wp, tm, tn):
    t, d = h.shape
    return pl.pallas_call(
        _ple_kernel,
        out_shape=jax.ShapeDtypeStruct((t, d), jnp.float32),
        grid=(t // tm, d // tn),
        in_specs=[
            pl.BlockSpec((tm, d), lambda i, j: (i, 0)),
            pl.BlockSpec((tm, tn), lambda i, j: (i, j)),
            pl.BlockSpec((tm, PLE_DIM), lambda i, j: (i, 0)),
            pl.BlockSpec((1, d), lambda i, j: (0, 0)),
            pl.BlockSpec((d, tn), lambda i, j: (0, j)),
            pl.BlockSpec((PLE_DIM, tn), lambda i, j: (0, j)),
        ],
        out_specs=pl.BlockSpec((tm, tn), lambda i, j: (i, j)),
        scratch_shapes=[pltpu.VMEM((tm, d), jnp.bfloat16), pltpu.VMEM((tm, PLE_DIM), jnp.bfloat16)],
        compiler_params=_cparams(("parallel", "arbitrary")),
        name="ple",
    )(h, h, p, g, wg, wp)


def _regroup_w_in(w):
    d = w.shape[0]
    seg = lambda a, b: w[:, a:b]
    parts = [
        seg(_O_CQ, _O_CKV), seg(_O_CKV, _O_KPE),
        seg(_O_CX, _O_CB), seg(_O_CB, _O_CC), seg(_O_CC, _O_CONVZ), seg(_O_CONVZ, _O_DQ),
        seg(_O_MLAZ, _O_CX), seg(_O_DQ, _O_DK), seg(_O_DSAZ, _O_IQ),
        seg(_O_DK, _O_DV), seg(_O_DV, _O_DSAZ),
        seg(_O_KPE, _O_MLAZ), seg(_O_IK, _N_IN), seg(_O_IW, _O_IK),
        jnp.zeros((d, Z_IQ - Z_IW - IDX_HEADS), w.dtype),
        seg(_O_IQ, _O_IW),
    ]
    out = jnp.concatenate(parts, axis=1).astype(jnp.bfloat16)
    assert out.shape[1] == Z_WIDTH
    return out


def _pair_layout(a):
    lead = a.shape[:-1]
    a = a.reshape(lead + (MLA_HEADS, MLA_QK))
    nope, x1, x2 = a[..., :MLA_NOPE], a[..., MLA_NOPE:MLA_NOPE + 32], a[..., MLA_NOPE + 32:]
    return jnp.concatenate([nope, x1, x2, x2, x1], axis=-1).reshape(lead + (MLA_HEADS * 256,))


def _rope_tables(positions):
    f32 = jnp.float32
    pos = positions.reshape(-1).astype(f32)[:, None]
    t = pos.shape[0]

    def cs(half):
        inv = ROPE_THETA ** (-jnp.arange(half, dtype=f32) / half)
        ang = pos * inv
        return jnp.cos(ang), jnp.sin(ang)

    c32, s32 = cs(MLA_ROPE // 2)
    c16, s16 = cs(DSA_ROT // 2)
    c8, s8 = cs(IDX_ROT // 2)
    zeros = lambda n: jnp.zeros((t, n), f32)
    ones = lambda n: jnp.ones((t, n), f32)
    cat = lambda *xs: jnp.concatenate(xs, axis=1)
    qs = IDX_DIM ** -0.5
    half_i = lambda a, b, c: cat(a, b, c, a, b, c) * qs
    tabs = [
        cat(c32, c32, -s32, s32),
        cat(c32, c32, c8, c8, ones(48)),
        cat(-s32, zeros(96)),
        cat(zeros(32), s32, zeros(64)),
        cat(zeros(64), -s8, zeros(56)),
        cat(zeros(72), s8, zeros(48)),
        cat(c16, c16, ones(96)),
        cat(-s16, zeros(112)),
        cat(zeros(16), s16, zeros(96)),
        half_i(c8, c8, ones(48)),
        half_i(-s8, zeros(8), zeros(48)),
        half_i(zeros(8), s8, zeros(48)),
    ]
    return jnp.stack(tabs, axis=0)


def _layer(h, p_i, tabs, batch, seq, k_top, norm_in, w_in, mla_gq, mla_w_uq, mla_gkv, mla_w_ukv, mla_qn, mla_kn,
           conv_w, dsa_qn, dsa_kn, w_out, ple_norm, ple_w_gate, ple_w_proj):
    f32, bf16 = jnp.float32, jnp.bfloat16
    t = h.shape[0]
    tm_big = min(1024, t)

    z = _inproj(h, norm_in[None, :], _regroup_w_in(w_in), tm_big, 1024)

    gq = _pair_layout(jnp.tile(mla_qn, MLA_HEADS))[None, :] * (MLA_QK ** -0.5)
    g46 = jnp.concatenate([mla_kn[MLA_NOPE:], jnp.ones((64,), f32)])[None, :]
    qm, km, vm, qd, kd, iqp, ikd = _prep(
        z, tabs, mla_gq[None, :], mla_gkv[None, :], _pair_layout(mla_w_uq).astype(bf16), mla_w_ukv.astype(bf16),
        gq, mla_kn[None, :MLA_NOPE], g46, dsa_qn[None, :] * (DSA_HEAD_DIM ** -0.5), dsa_kn[None, :], 256)

    y_b = _conv(z, jnp.pad(conv_w, ((0, 8 - CONV_K), (0, 0))), seq, 512)
    y_a = _mla(qm, km, vm, z, batch, seq, 512)
    y_c = _dsa(iqp, ikd, qd, kd, z, batch, seq, 128, k_top)

    wo = w_out.astype(bf16)
    h = _outproj(h, y_a, y_b, y_c, wo[:MLA_WIDTH], wo[MLA_WIDTH:MLA_WIDTH + CONV_CH], wo[MLA_WIDTH + CONV_CH:],
                 tm_big, 1024)
    return _ple(h, p_i, ple_norm[None, :], ple_w_gate.astype(bf16), ple_w_proj.astype(bf16), 512, 1024)


def kernel(x, p, positions, norm_in, w_in, mla_gq, mla_w_uq, mla_gkv, mla_w_ukv, mla_qn, mla_kn, conv_w, dsa_qn,
           dsa_kn, w_out, ple_norm, ple_w_gate, ple_w_proj):
    batch, seq, d = x.shape
    depth = p.shape[0]
    t = batch * seq
    k_top = min(TOPK_MAX, seq // 4)
    tabs = _rope_tables(positions)
    h = x.reshape(t, d)
    for i in range(depth):
        h = _layer(h, p[i].reshape(t, PLE_DIM), tabs, batch, seq, k_top, norm_in[i], w_in[i], mla_gq[i],
                   mla_w_uq[i], mla_gkv[i], mla_w_ukv[i], mla_qn[i], mla_kn[i], conv_w[i], dsa_qn[i], dsa_kn[i],
                   w_out[i], ple_norm[i], ple_w_gate[i], ple_w_proj[i])
    return h.reshape(batch, seq, d)
```

```python
import functools

import jax
import jax.numpy as jnp
from jax import lax
from jax.experimental import pallas as pl
from jax.experimental.pallas import tpu as pltpu

D_MODEL = 2048
PLE_DIM = 256
ROPE_THETA = 500000.0
EPS = 1e-6

MLA_HEADS = 6
MLA_Q_LORA = 512
MLA_KV_LORA = 512
MLA_NOPE = 128
MLA_ROPE = 64
MLA_V = 128
MLA_QK = MLA_NOPE + MLA_ROPE
MLA_WIDTH = MLA_HEADS * MLA_V

CONV_CH = 512
CONV_K = 3

DSA_HEADS = 6
DSA_KV_HEADS = 2
DSA_GROUP = DSA_HEADS // DSA_KV_HEADS
DSA_HEAD_DIM = 128
DSA_WIDTH = DSA_HEADS * DSA_HEAD_DIM
DSA_ROT = DSA_HEAD_DIM // 4
IDX_HEADS = 16
IDX_DIM = 64
IDX_ROT = IDX_DIM // 4
TOPK_MAX = 256

LANES = 128

_O_CQ, _O_CKV, _O_KPE, _O_MLAZ = 0, 512, 1024, 1088
_O_CX, _O_CB, _O_CC, _O_CONVZ = 1856, 2368, 2880, 3392
_O_DQ, _O_DK, _O_DV, _O_DSAZ = 3904, 4672, 4928, 5184
_O_IQ, _O_IW, _O_IK, _N_IN = 5952, 6976, 6992, 7056

Z_CQ, Z_CKV = 0, 512
Z_CX, Z_CB, Z_CC, Z_CONVZ = 1024, 1536, 2048, 2560
Z_MLAZ, Z_DQ, Z_DSAZ = 3072, 3840, 4608
Z_DK, Z_DV = 5376, 5632
Z_KPE_IK, Z_IW, Z_IQ = 5888, 6016, 6144
Z_WIDTH = 7168

LOG2E = 1.4426950408889634
INT_MIN = -(2 ** 31)
NEG_BIG = -1e30

VMEM_LIMIT = 56 * 1024 * 1024


def _cparams(sem):
    return pltpu.CompilerParams(dimension_semantics=sem, vmem_limit_bytes=VMEM_LIMIT)


def _dot(a, b):
    return jnp.dot(a, b, preferred_element_type=jnp.float32)


def _dot_nt(a, b):
    return lax.dot_general(a, b, (((1,), (1,)), ((), ())), preferred_element_type=jnp.float32)


def _roll(x, shift):
    return pltpu.roll(x, shift, 1)


def _inproj_kernel(x_ref, g_ref, w_ref, o_ref, a_ref):
    @pl.when(pl.program_id(1) == 0)
    def _():
        x = x_ref[...]
        r = lax.rsqrt(jnp.mean(x * x, axis=-1, keepdims=True) + EPS)
        a_ref[...] = (x * r * g_ref[...]).astype(jnp.bfloat16)

    o_ref[...] = _dot(a_ref[...], w_ref[...]).astype(o_ref.dtype)


def _inproj(h, g, w, tm, tn):
    t, d = h.shape
    n = w.shape[1]
    return pl.pallas_call(
        _inproj_kernel,
        out_shape=jax.ShapeDtypeStruct((t, n), jnp.bfloat16),
        grid=(t // tm, n // tn),
        in_specs=[
            pl.BlockSpec((tm, d), lambda i, j: (i, 0)),
            pl.BlockSpec((1, d), lambda i, j: (0, 0)),
            pl.BlockSpec((d, tn), lambda i, j: (0, j)),
        ],
        out_specs=pl.BlockSpec((tm, tn), lambda i, j: (i, j)),
        scratch_shapes=[pltpu.VMEM((tm, d), jnp.bfloat16)],
        compiler_params=_cparams(("parallel", "arbitrary")),
        name="inproj",
    )(h, g, w)


TB_CSQ = 0
TB_C46, TB_S1A, TB_S2A, TB_S1B, TB_S2B = 1, 2, 3, 4, 5
TB_CD, TB_S1D, TB_S2D = 6, 7, 8
TB_CI, TB_S1I, TB_S2I = 9, 10, 11
N_TABLES = 12


def _prep_kernel(cq_ref, ckv_ref, t46_ref, dq_ref, dk_ref, iq_ref, tab_ref,
                 glq_ref, glkv_ref, wuq_ref, wukv_ref, gq_ref, gkn_ref, g46_ref, gdq_ref, gdk_ref,
                 qm_ref, km_ref, vm_ref, qd_ref, kd_ref, iqp_ref, ikd_ref):
    f32, bf16 = jnp.float32, jnp.bfloat16
    tm = cq_ref.shape[0]
    lane = lax.broadcasted_iota(jnp.int32, (tm, LANES), 1)
    low = lane < 64

    def rms(x, n):
        return lax.rsqrt(jnp.sum(x * x, axis=-1, keepdims=True) * (1.0 / n) + EPS)

    cq = cq_ref[...].astype(f32)
    aq = (cq * rms(cq, MLA_Q_LORA) * glq_ref[...]).astype(bf16)
    qx = _dot(aq, wuq_ref[...])
    csq = tab_ref[TB_CSQ]
    for h in range(MLA_HEADS):
        nope = qx[:, 256 * h:256 * h + 128]
        pair = qx[:, 256 * h + 128:256 * h + 256]
        ss = jnp.sum(nope * nope, axis=-1, keepdims=True) + 0.5 * jnp.sum(pair * pair, axis=-1, keepdims=True)
        r = lax.rsqrt(ss * (1.0 / MLA_QK) + EPS)
        qm_ref[h, :, 0:128] = (nope * r * gq_ref[:, 256 * h:256 * h + 128]).astype(bf16)
        qm_ref[h, :, 128:256] = (pair * r * gq_ref[:, 256 * h + 128:256 * h + 256] * csq).astype(bf16)

    t46 = t46_ref[...].astype(f32)
    ss_pe = jnp.sum(jnp.where(low, t46 * t46, 0.0), axis=-1, keepdims=True)
    y46 = t46 * g46_ref[...]
    r46 = (y46 * tab_ref[TB_C46]
           + _roll(y46, 96) * tab_ref[TB_S1A] + _roll(y46, 32) * tab_ref[TB_S2A]
           + _roll(y46, 120) * tab_ref[TB_S1B] + _roll(y46, 8) * tab_ref[TB_S2B])
    sw = _roll(r46, 64)
    kpe2 = jnp.where(low, r46, sw)
    ikd_ref[...] = jnp.where(low, sw, r46).astype(bf16)

    ckv = ckv_ref[...].astype(f32)
    akv = (ckv * rms(ckv, MLA_KV_LORA) * glkv_ref[...]).astype(bf16)
    kvx = _dot(akv, wukv_ref[...])
    for h in range(MLA_HEADS):
        kn = kvx[:, 256 * h:256 * h + 128]
        ss = jnp.sum(kn * kn, axis=-1, keepdims=True) + ss_pe
        r = lax.rsqrt(ss * (1.0 / MLA_QK) + EPS)
        km_ref[h, :, 0:128] = (kn * r * gkn_ref[...]).astype(bf16)
        km_ref[h, :, 128:256] = (kpe2 * r).astype(bf16)
        vm_ref[h] = kvx[:, 256 * h + 128:256 * h + 256].astype(bf16)

    cd, s1d, s2d = tab_ref[TB_CD], tab_ref[TB_S1D], tab_ref[TB_S2D]

    def dsa_head(x, g):
        y = x * rms(x, DSA_HEAD_DIM) * g
        return y * cd + _roll(y, 112) * s1d + _roll(y, 16) * s2d

    for h in range(DSA_HEADS):
        x = dq_ref[:, 128 * h:128 * h + 128].astype(f32)
        qd_ref[h] = dsa_head(x, gdq_ref[...]).astype(bf16)
    for g in range(DSA_KV_HEADS):
        x = dk_ref[:, 128 * g:128 * g + 128].astype(f32)
        kd_ref[g] = dsa_head(x, gdk_ref[...]).astype(bf16)

    ci, s1i, s2i = tab_ref[TB_CI], tab_ref[TB_S1I], tab_ref[TB_S2I]
    for j in range(IDX_HEADS // 2):
        x = iq_ref[:, 128 * j:128 * j + 128].astype(f32)
        y = x * ci + _roll(x, 120) * s1i + _roll(x, 8) * s2i
        iqp_ref[:, 256 * j:256 * j + 128] = jnp.where(low, y, 0.0).astype(bf16)
        iqp_ref[:, 256 * j + 128:256 * j + 256] = jnp.where(low, 0.0, y).astype(bf16)


def _prep(z, tabs, glq, glkv, wuq, wukv, gq, gkn, g46, gdq, gdk, tm):
    t = z.shape[0]
    bf16 = jnp.bfloat16
    row = lambda w, c: pl.BlockSpec((tm, w), lambda i, c=c: (i, c))
    full = lambda a: pl.BlockSpec(a.shape, lambda i: (0,) * a.ndim)
    heads = lambda n, w: pl.BlockSpec((n, tm, w), lambda i: (0, i, 0))
    return pl.pallas_call(
        _prep_kernel,
        out_shape=(
            jax.ShapeDtypeStruct((MLA_HEADS, t, 256), bf16),
            jax.ShapeDtypeStruct((MLA_HEADS, t, 256), bf16),
            jax.ShapeDtypeStruct((MLA_HEADS, t, 128), bf16),
            jax.ShapeDtypeStruct((DSA_HEADS, t, 128), bf16),
            jax.ShapeDtypeStruct((DSA_KV_HEADS, t, 128), bf16),
            jax.ShapeDtypeStruct((t, IDX_HEADS * 128), bf16),
            jax.ShapeDtypeStruct((t, 128), bf16),
        ),
        grid=(t // tm,),
        in_specs=[
            row(512, Z_CQ // 512), row(512, Z_CKV // 512), row(128, Z_KPE_IK // 128),
            row(768, Z_DQ // 768), row(256, Z_DK // 256), row(1024, Z_IQ // 1024),
            pl.BlockSpec((N_TABLES, tm, LANES), lambda i: (0, i, 0)),
            full(glq), full(glkv), full(wuq), full(wukv), full(gq), full(gkn), full(g46), full(gdq), full(gdk),
        ],
        out_specs=(
            heads(MLA_HEADS, 256), heads(MLA_HEADS, 256), heads(MLA_HEADS, 128),
            heads(DSA_HEADS, 128), heads(DSA_KV_HEADS, 128),
            pl.BlockSpec((tm, IDX_HEADS * 128), lambda i: (i, 0)),
            pl.BlockSpec((tm, 128), lambda i: (i, 0)),
        ),
        compiler_params=_cparams(("parallel",)),
        name="prep",
    )(z, z, z, z, z, z, tabs, glq, glkv, wuq, wukv, gq, gkn, g46, gdq, gdk)


def _silu(x):
    return x / (1.0 + jnp.exp(-x))


def _conv_kernel(cx_ref, cb_ref, cc_ref, cz_ref, hx_ref, hc_ref, w_ref, o_ref, *, tiles_per_seq):
    f32 = jnp.float32
    tm, ch = cx_ref.shape
    u = cc_ref[...].astype(f32) * cx_ref[...].astype(f32)
    keep = (pl.program_id(0) % tiles_per_seq != 0).astype(f32)
    hu = hc_ref[...].astype(f32) * hx_ref[...].astype(f32) * keep
    row = lax.broadcasted_iota(jnp.int32, (tm, ch), 0)
    u1 = jnp.where(row == 0, hu[7:8, :], pltpu.roll(u, 1, 0))
    u2 = jnp.where(row == 0, hu[6:7, :], jnp.where(row == 1, hu[7:8, :], pltpu.roll(u, 2, 0)))
    conv = w_ref[0:1, :] * u2 + w_ref[1:2, :] * u1 + w_ref[2:3, :] * u
    o_ref[...] = (cb_ref[...].astype(f32) * conv * _silu(cz_ref[...].astype(f32))).astype(o_ref.dtype)


def _conv(z, w, seq, tm):
    t = z.shape[0]
    col = lambda c: pl.BlockSpec((tm, CONV_CH), lambda i, c=c: (i, c))
    halo = lambda c: pl.BlockSpec((8, CONV_CH), lambda i, c=c: (jnp.maximum(i * (tm // 8) - 1, 0), c))
    return pl.pallas_call(
        functools.partial(_conv_kernel, tiles_per_seq=seq // tm),
        out_shape=jax.ShapeDtypeStruct((t, CONV_CH), jnp.bfloat16),
        grid=(t // tm,),
        in_specs=[col(Z_CX // CONV_CH), col(Z_CB // CONV_CH), col(Z_CC // CONV_CH), col(Z_CONVZ // CONV_CH),
                  halo(Z_CX // CONV_CH), halo(Z_CC // CONV_CH),
                  pl.BlockSpec((8, CONV_CH), lambda i: (0, 0))],
        out_specs=pl.BlockSpec((tm, CONV_CH), lambda i: (i, 0)),
        compiler_params=_cparams(("parallel",)),
        name="conv",
    )(z, z, z, z, z, z, w)


def _tile_lanes(x, n):
    return x if n == 1 else jnp.concatenate([x] * n, axis=1)


def _softmax_step(s, v2, m_ref, acc_ref):
    m_prev = m_ref[...]
    m_new = jnp.maximum(m_prev, jnp.max(s, axis=-1, keepdims=True))
    p = jnp.exp2(s - _tile_lanes(m_new, s.shape[1] // LANES))
    alpha = jnp.exp2(m_prev - m_new)
    acc_ref[...] = _tile_lanes(alpha, 2) * acc_ref[...] + _dot(p.astype(jnp.bfloat16), v2)
    m_ref[...] = m_new


def _with_ones(v):
    return jnp.concatenate([v, jnp.ones_like(v)], axis=1)


MLA_HEADS_PER_STEP = 2


def _mla_kernel(q_ref, k_ref, v_ref, zg_ref, o_ref, m_ref, acc_ref):
    f32 = jnp.float32
    tq = q_ref.shape[1]
    qi = pl.program_id(2)
    m_ref[...] = jnp.full(m_ref.shape, NEG_BIG, f32)
    acc_ref[...] = jnp.zeros(acc_ref.shape, f32)

    def step(ki, masked):
        rows = pl.ds(pl.multiple_of(ki * tq, tq), tq)
        for h in range(MLA_HEADS_PER_STEP):
            s = _dot_nt(q_ref[h], k_ref[h, rows, :])
            if masked:
                r = lax.broadcasted_iota(jnp.int32, s.shape, 0)
                c = lax.broadcasted_iota(jnp.int32, s.shape, 1)
                s = jnp.where(c <= r, s, 2.0 * NEG_BIG)
            _softmax_step(s, _with_ones(v_ref[h, rows, :]), m_ref.at[h], acc_ref.at[h])

    def body(ki, c):
        step(ki, False)
        return c

    lax.fori_loop(0, qi, body, 0)
    step(qi, True)
    for h in range(MLA_HEADS_PER_STEP):
        o = acc_ref[h, :, 0:128] / acc_ref[h, :, 128:256]
        zg = zg_ref[:, 128 * h:128 * h + 128].astype(f32)
        o_ref[:, 128 * h:128 * h + 128] = (o * _silu(zg)).astype(o_ref.dtype)


def _mla(qm, km, vm, z, batch, seq, tq):
    t = z.shape[0]
    nq = seq // tq
    hs = MLA_HEADS_PER_STEP
    return pl.pallas_call(
        _mla_kernel,
        out_shape=jax.ShapeDtypeStruct((t, MLA_WIDTH), jnp.bfloat16),
        grid=(batch, MLA_HEADS // hs, nq),
        in_specs=[
            pl.BlockSpec((hs, tq, 256), lambda b, h, i: (h, b * nq + i, 0)),
            pl.BlockSpec((hs, seq, 256), lambda b, h, i: (h, b, 0)),
            pl.BlockSpec((hs, seq, 128), lambda b, h, i: (h, b, 0)),
            pl.BlockSpec((tq, 128 * hs), lambda b, h, i: (b * nq + i, Z_MLAZ // (128 * hs) + h)),
        ],
        out_specs=pl.BlockSpec((tq, 128 * hs), lambda b, h, i: (b * nq + i, h)),
        scratch_shapes=[pltpu.VMEM((hs, tq, LANES), jnp.float32), pltpu.VMEM((hs, tq, 256), jnp.float32)],
        compiler_params=_cparams(("parallel", "parallel", "arbitrary")),
        name="mla",
    )(qm, km, vm, z)


KC = 256


def _dsa_kernel(iq_ref, iw_ref, ik_ref, q_ref, k_ref, v0_ref, v1_ref, zg_ref, o_ref,
                key_ref, keyt_ref, wb_ref, m_ref, acc_ref, *, k_top):
    f32, i32, bf16 = jnp.float32, jnp.int32, jnp.bfloat16
    tq = iq_ref.shape[0]
    i = pl.program_id(1)
    n_chunks = ((i + 1) * tq + KC - 1) // KC
    q_pos = i * tq + lax.broadcasted_iota(i32, (tq, KC), 0)
    lane_kc = lax.broadcasted_iota(i32, (tq, KC), 1)

    w = iw_ref[...].astype(f32) * (IDX_HEADS ** -0.5)
    for h in range(IDX_HEADS):
        wb_ref[h] = jnp.broadcast_to(w[:, h:h + 1], (tq, KC))

    def score_chunk(c, carry):
        start = pl.multiple_of(c * KC, KC)
        ikc = ik_ref[pl.ds(start, KC), :]
        acc = jnp.zeros((tq, KC), f32)
        for h in range(IDX_HEADS):
            lg = _dot_nt(iq_ref[:, 128 * h:128 * h + 128], ikc)
            acc = acc + wb_ref[h] * jnp.maximum(lg, 0.0)
        bits = lax.bitcast_convert_type(acc, i32)
        key = jnp.where(bits < 0, bits ^ jnp.int32(0x7FFFFFFF), bits)
        key = jnp.where(start + lane_kc <= q_pos, key, jnp.int32(INT_MIN))
        key_ref[:, pl.ds(start, KC)] = key
        keyt_ref[pl.ds(start, KC), :] = key.T
        return carry

    lax.fori_loop(0, n_chunks, score_chunk, 0)

    def count_where(hits):
        def body(c, cnt):
            start = pl.multiple_of(c * KC, KC)
            hit = hits(keyt_ref[pl.ds(start, KC), :], start)
            return cnt + jnp.sum(hit.reshape(KC // 8, 8, tq), axis=0)
        cnt = lax.fori_loop(0, n_chunks, body, jnp.zeros((8, tq), i32))
        return jnp.sum(cnt, axis=0, keepdims=True)

    def bit_step(it, r):
        cand = r + lax.shift_left(jnp.int32(1), 31 - it)
        return jnp.where(count_where(lambda kt, _: jnp.where(kt >= cand, 1, 0)) >= k_top, cand, r)

    thr = lax.fori_loop(0, 32, bit_step, jnp.full((1, tq), INT_MIN, i32))

    n_ge = count_where(lambda kt, _: jnp.where(kt >= thr, 1, 0))
    has_excess = jnp.max(n_ge) > k_top

    @pl.when(has_excess)
    def _():
        need = k_top - count_where(lambda kt, _: jnp.where(kt > thr, 1, 0))
        sub = lax.broadcasted_iota(i32, (KC, tq), 0)

        def idx_step(it, bound):
            cand = bound + lax.shift_left(jnp.int32(1), 30 - it)
            n_below = count_where(lambda kt, start: jnp.where(kt == thr, jnp.where(start + sub < cand, 1, 0), 0))
            return jnp.where(n_below <= need, cand, bound)

        bound = lax.fori_loop(0, 31, idx_step, jnp.zeros((1, tq), i32))

        def demote(c, carry):
            start = pl.multiple_of(c * KC, KC)
            kt = keyt_ref[pl.ds(start, KC), :]
            drop = jnp.where(kt == thr, jnp.where(start + sub >= bound, 1, 0), 0)
            kt = jnp.where(drop == 1, jnp.int32(INT_MIN), kt)
            keyt_ref[pl.ds(start, KC), :] = kt
            key_ref[:, pl.ds(start, KC)] = kt.T
            return carry

        lax.fori_loop(0, n_chunks, demote, 0)

    thr_sel = jnp.maximum(thr, jnp.int32(INT_MIN + 1))
    thr_kc = _tile_lanes(jnp.broadcast_to(thr_sel, (LANES, tq)).T, KC // LANES)

    m_ref[...] = jnp.full(m_ref.shape, NEG_BIG, f32)
    acc_ref[...] = jnp.zeros(acc_ref.shape, f32)
    v_refs = (v0_ref, v1_ref)

    def attn_chunk(c, carry):
        start = pl.multiple_of(c * KC, KC)
        bias = jnp.where(key_ref[:, pl.ds(start, KC)] >= thr_kc, 0.0, 2.0 * NEG_BIG)
        bias3 = jnp.concatenate([bias] * DSA_GROUP, axis=0)
        for g in range(DSA_KV_HEADS):
            qg = q_ref[DSA_GROUP * g:DSA_GROUP * (g + 1)].reshape(DSA_GROUP * tq, DSA_HEAD_DIM)
            s = _dot_nt(qg, k_ref[g, pl.ds(start, KC), :]) + bias3
            _softmax_step(s, _with_ones(v_refs[g][pl.ds(start, KC), :]), m_ref.at[g], acc_ref.at[g])
        return carry

    lax.fori_loop(0, n_chunks, attn_chunk, 0)

    for g in range(DSA_KV_HEADS):
        o = acc_ref[g, :, 0:128] / acc_ref[g, :, 128:256]
        for r in range(DSA_GROUP):
            hd = DSA_GROUP * g + r
            zg = zg_ref[:, 128 * hd:128 * hd + 128].astype(f32)
            o_ref[:, 128 * hd:128 * hd + 128] = (o[r * tq:(r + 1) * tq] * _silu(zg)).astype(o_ref.dtype)


def _dsa(iqp, ikd, qd, kd, z, batch, seq, tq, k_top):
    t = z.shape[0]
    nq = seq // tq
    f32 = jnp.float32
    return pl.pallas_call(
        functools.partial(_dsa_kernel, k_top=k_top),
        out_shape=jax.ShapeDtypeStruct((t, DSA_WIDTH), jnp.bfloat16),
        grid=(batch, nq),
        in_specs=[
            pl.BlockSpec((tq, IDX_HEADS * 128), lambda b, i: (b * nq + i, 0)),
            pl.BlockSpec((tq, 128), lambda b, i: (b * nq + i, Z_IW // 128)),
            pl.BlockSpec((seq, 128), lambda b, i: (b, 0)),
            pl.BlockSpec((DSA_HEADS, tq, 128), lambda b, i: (0, b * nq + i, 0)),
            pl.BlockSpec((DSA_KV_HEADS, seq, 128), lambda b, i: (0, b, 0)),
            pl.BlockSpec((seq, 128), lambda b, i: (b, Z_DV // 128)),
            pl.BlockSpec((seq, 128), lambda b, i: (b, Z_DV // 128 + 1)),
            pl.BlockSpec((tq, DSA_WIDTH), lambda b, i: (b * nq + i, Z_DSAZ // DSA_WIDTH)),
        ],
        out_specs=pl.BlockSpec((tq, DSA_WIDTH), lambda b, i: (b * nq + i, 0)),
        scratch_shapes=[
            pltpu.VMEM((tq, seq), jnp.int32),
            pltpu.VMEM((seq, tq), jnp.int32),
            pltpu.VMEM((IDX_HEADS, tq, KC), f32),
            pltpu.VMEM((DSA_KV_HEADS, DSA_GROUP * tq, LANES), f32),
            pltpu.VMEM((DSA_KV_HEADS, DSA_GROUP * tq, 2 * DSA_HEAD_DIM), f32),
        ],
        compiler_params=_cparams(("parallel", "arbitrary")),
        name="dsa",
    )(iqp, z, ikd, qd, kd, z, z, z)


def _outproj_kernel(h_ref, ya_ref, yb_ref, yc_ref, wa_ref, wb_ref, wc_ref, o_ref):
    acc = _dot(ya_ref[...], wa_ref[...])
    acc = acc + _dot(yb_ref[...], wb_ref[...])
    acc = acc + _dot(yc_ref[...], wc_ref[...])
    o_ref[...] = h_ref[...] + acc


def _outproj(h, ya, yb, yc, wa, wb, wc, tm, tn):
    t, d = h.shape
    rows = lambda a: pl.BlockSpec((tm, a.shape[1]), lambda i, j: (i, 0))
    cols = lambda a: pl.BlockSpec((a.shape[0], tn), lambda i, j: (0, j))
    return pl.pallas_call(
        _outproj_kernel,
        out_shape=jax.ShapeDtypeStruct((t, d), jnp.float32),
        grid=(t // tm, d // tn),
        in_specs=[pl.BlockSpec((tm, tn), lambda i, j: (i, j)), rows(ya), rows(yb), rows(yc),
                  cols(wa), cols(wb), cols(wc)],
        out_specs=pl.BlockSpec((tm, tn), lambda i, j: (i, j)),
        compiler_params=_cparams(("parallel", "arbitrary")),
        name="outproj",
    )(h, ya, yb, yc, wa, wb, wc)


def _ple_kernel(x_ref, xc_ref, p_ref, g_ref, wg_ref, wp_ref, o_ref, a_ref, pb_ref):
    @pl.when(pl.program_id(1) == 0)
    def _():
        x = x_ref[...]
        r = lax.rsqrt(jnp.mean(x * x, axis=-1, keepdims=True) + EPS)
        a_ref[...] = (x * r * g_ref[...]).astype(jnp.bfloat16)
        pb_ref[...] = p_ref[...].astype(jnp.bfloat16)

    gate = _dot(a_ref[...], wg_ref[...])
    gate = 1.0 / (1.0 + jnp.exp(-gate))
    o_ref[...] = xc_ref[...] + gate * _dot(pb_ref[...], wp_ref[...])


def _ple(h, p, g, wg, wp, tm, tn):
    t, d = h.shape
    return pl.pallas_call(
        _ple_kernel,
        out_shape=jax.ShapeDtypeStruct((t, d), jnp.float32),
        grid=(t // tm, d // tn),
        in_specs=[
            pl.BlockSpec((tm, d), lambda i, j: (i, 0)),
            pl.BlockSpec((tm, tn), lambda i, j: (i, j)),
            pl.BlockSpec((tm, PLE_DIM), lambda i, j: (i, 0)),
            pl.BlockSpec((1, d), lambda i, j: (0, 0)),
            pl.BlockSpec((d, tn), lambda i, j: (0, j)),
            pl.BlockSpec((PLE_DIM, tn), lambda i, j: (0, j)),
        ],
        out_specs=pl.BlockSpec((tm, tn), lambda i, j: (i, j)),
        scratch_shapes=[pltpu.VMEM((tm, d), jnp.bfloat16), pltpu.VMEM((tm, PLE_DIM), jnp.bfloat16)],
        compiler_params=_cparams(("parallel", "arbitrary")),
        name="ple",
    )(h, h, p, g, wg, wp)


def _regroup_w_in(w):
    d = w.shape[0]
    seg = lambda a, b: w[:, a:b]
    parts = [
        seg(_O_CQ, _O_CKV), seg(_O_CKV, _O_KPE),
        seg(_O_CX, _O_CB), seg(_O_CB, _O_CC), seg(_O_CC, _O_CONVZ), seg(_O_CONVZ, _O_DQ),
        seg(_O_MLAZ, _O_CX), seg(_O_DQ, _O_DK), seg(_O_DSAZ, _O_IQ),
        seg(_O_DK, _O_DV), seg(_O_DV, _O_DSAZ),
        seg(_O_KPE, _O_MLAZ), seg(_O_IK, _N_IN), seg(_O_IW, _O_IK),
        jnp.zeros((d, Z_IQ - Z_IW - IDX_HEADS), w.dtype),
        seg(_O_IQ, _O_IW),
    ]
    out = jnp.concatenate(parts, axis=1).astype(jnp.bfloat16)
    assert out.shape[1] == Z_WIDTH
    return out


def _pair_layout(a):
    lead = a.shape[:-1]
    a = a.reshape(lead + (MLA_HEADS, MLA_QK))
    nope, x1, x2 = a[..., :MLA_NOPE], a[..., MLA_NOPE:MLA_NOPE + 32], a[..., MLA_NOPE + 32:]
    return jnp.concatenate([nope, x1, x2, x2, x1], axis=-1).reshape(lead + (MLA_HEADS * 256,))


def _rope_tables(positions):
    f32 = jnp.float32
    pos = positions.reshape(-1).astype(f32)[:, None]
    t = pos.shape[0]

    def cs(half):
        inv = ROPE_THETA ** (-jnp.arange(half, dtype=f32) / half)
        ang = pos * inv
        return jnp.cos(ang), jnp.sin(ang)

    c32, s32 = cs(MLA_ROPE // 2)
    c16, s16 = cs(DSA_ROT // 2)
    c8, s8 = cs(IDX_ROT // 2)
    zeros = lambda n: jnp.zeros((t, n), f32)
    ones = lambda n: jnp.ones((t, n), f32)
    cat = lambda *xs: jnp.concatenate(xs, axis=1)
    qs = IDX_DIM ** -0.5
    half_i = lambda a, b, c: cat(a, b, c, a, b, c) * qs
    tabs = [
        cat(c32, c32, -s32, s32),
        cat(c32, c32, c8, c8, ones(48)),
        cat(-s32, zeros(96)),
        cat(zeros(32), s32, zeros(64)),
        cat(zeros(64), -s8, zeros(56)),
        cat(zeros(72), s8, zeros(48)),
        cat(c16, c16, ones(96)),
        cat(-s16, zeros(112)),
        cat(zeros(16), s16, zeros(96)),
        half_i(c8, c8, ones(48)),
        half_i(-s8, zeros(8), zeros(48)),
        half_i(zeros(8), s8, zeros(48)),
    ]
    return jnp.stack(tabs, axis=0)


def _layer(h, p_i, tabs, batch, seq, k_top, norm_in, w_in, mla_gq, mla_w_uq, mla_gkv, mla_w_ukv, mla_qn, mla_kn,
           conv_w, dsa_qn, dsa_kn, w_out, ple_norm, ple_w_gate, ple_w_proj):
    f32, bf16 = jnp.float32, jnp.bfloat16
    t = h.shape[0]
    tm_big = min(1024, t)

    z = _inproj(h, norm_in[None, :], _regroup_w_in(w_in), tm_big, 1024)

    gq = _pair_layout(jnp.tile(mla_qn, MLA_HEADS))[None, :] * (MLA_QK ** -0.5 * LOG2E)
    g46 = jnp.concatenate([mla_kn[MLA_NOPE:], jnp.ones((64,), f32)])[None, :]
    qm, km, vm, qd, kd, iqp, ikd = _prep(
        z, tabs, mla_gq[None, :], mla_gkv[None, :], _pair_layout(mla_w_uq).astype(bf16), mla_w_ukv.astype(bf16),
        gq, mla_kn[None, :MLA_NOPE], g46, dsa_qn[None, :] * (DSA_HEAD_DIM ** -0.5 * LOG2E), dsa_kn[None, :], 256)

    y_b = _conv(z, jnp.pad(conv_w, ((0, 8 - CONV_K), (0, 0))), seq, 512)
    y_a = _mla(qm, km, vm, z, batch, seq, 512)
    y_c = _dsa(iqp, ikd, qd, kd, z, batch, seq, 256, k_top)

    wo = w_out.astype(bf16)
    h = _outproj(h, y_a, y_b, y_c, wo[:MLA_WIDTH], wo[MLA_WIDTH:MLA_WIDTH + CONV_CH], wo[MLA_WIDTH + CONV_CH:],
                 tm_big, 1024)
    return _ple(h, p_i, ple_norm[None, :], ple_w_gate.astype(bf16), ple_w_proj.astype(bf16), 512, 1024)


def kernel(x, p, positions, norm_in, w_in, mla_gq, mla_w_uq, mla_gkv, mla_w_ukv, mla_qn, mla_kn, conv_w, dsa_qn,
           dsa_kn, w_out, ple_norm, ple_w_gate, ple_w_proj):
    batch, seq, d = x.shape
    depth = p.shape[0]
    t = batch * seq
    k_top = min(TOPK_MAX, seq // 4)
    tabs = _rope_tables(positions)
    h = x.reshape(t, d)
    for i in range(depth):
        h = _layer(h, p[i].reshape(t, PLE_DIM), tabs, batch, seq, k_top, norm_in[i], w_in[i], mla_gq[i],
                   mla_w_uq[i], mla_gkv[i], mla_w_ukv[i], mla_qn[i], mla_kn[i], conv_w[i], dsa_qn[i], dsa_kn[i],
                   w_out[i], ple_norm[i], ple_w_gate[i], ple_w_proj[i])
    return h.reshape(batch, seq, d)
```

```python
import functools

import numpy as np
import jax
import jax.numpy as jnp
from jax import lax
from jax.experimental import pallas as pl
from jax.experimental.pallas import tpu as pltpu

D_MODEL = 2048
PLE_DIM = 256
ROPE_THETA = 500000.0
EPS = 1e-6

MLA_HEADS = 6
MLA_Q_LORA = 512
MLA_KV_LORA = 512
MLA_NOPE = 128
MLA_ROPE = 64
MLA_V = 128
MLA_QK = MLA_NOPE + MLA_ROPE
MLA_WIDTH = MLA_HEADS * MLA_V

CONV_CH = 512
CONV_K = 3

DSA_HEADS = 6
DSA_KV_HEADS = 2
DSA_GROUP = DSA_HEADS // DSA_KV_HEADS
DSA_HEAD_DIM = 128
DSA_WIDTH = DSA_HEADS * DSA_HEAD_DIM
DSA_ROT = DSA_HEAD_DIM // 4
IDX_HEADS = 16
IDX_DIM = 64
IDX_ROT = IDX_DIM // 4
TOPK_MAX = 256

LANES = 128

_O_CQ, _O_CKV, _O_KPE, _O_MLAZ = 0, 512, 1024, 1088
_O_CX, _O_CB, _O_CC, _O_CONVZ = 1856, 2368, 2880, 3392
_O_DQ, _O_DK, _O_DV, _O_DSAZ = 3904, 4672, 4928, 5184
_O_IQ, _O_IW, _O_IK, _N_IN = 5952, 6976, 6992, 7056

Z_CQ, Z_CKV = 0, 512
Z_CX, Z_CB, Z_CC, Z_CONVZ = 1024, 1536, 2048, 2560
Z_MLAZ, Z_DQ, Z_DSAZ = 3072, 3840, 4608
Z_DK, Z_DV = 5376, 5632
Z_KPE_IK, Z_IW, Z_IQ = 5888, 6016, 6144
Z_WIDTH = 7168

LOG2E = 1.4426950408889634
INT_MIN = -(2 ** 31)
NEG_BIG = -1e30

VMEM_LIMIT = 56 * 1024 * 1024


def _cparams(sem):
    return pltpu.CompilerParams(dimension_semantics=sem, vmem_limit_bytes=VMEM_LIMIT)


def _dot(a, b):
    return jnp.dot(a, b, preferred_element_type=jnp.float32)


def _dot_nt(a, b):
    return lax.dot_general(a, b, (((1,), (1,)), ((), ())), preferred_element_type=jnp.float32)


def _roll(x, shift):
    return pltpu.roll(x, shift, 1)


def _inproj_kernel(x_ref, g_ref, w_ref, o_ref, a_ref):
    @pl.when(pl.program_id(1) == 0)
    def _():
        x = x_ref[...]
        r = lax.rsqrt(jnp.mean(x * x, axis=-1, keepdims=True) + EPS)
        a_ref[...] = (x * r * g_ref[...]).astype(jnp.bfloat16)

    o_ref[...] = _dot(a_ref[...], w_ref[...]).astype(o_ref.dtype)


def _inproj(h, g, w, tm, tn):
    t, d = h.shape
    n = w.shape[1]
    return pl.pallas_call(
        _inproj_kernel,
        out_shape=jax.ShapeDtypeStruct((t, n), jnp.bfloat16),
        grid=(t // tm, n // tn),
        in_specs=[
            pl.BlockSpec((tm, d), lambda i, j: (i, 0)),
            pl.BlockSpec((1, d), lambda i, j: (0, 0)),
            pl.BlockSpec((d, tn), lambda i, j: (0, j)),
        ],
        out_specs=pl.BlockSpec((tm, tn), lambda i, j: (i, j)),
        scratch_shapes=[pltpu.VMEM((tm, d), jnp.bfloat16)],
        compiler_params=_cparams(("parallel", "arbitrary")),
        name="inproj",
    )(h, g, w)


TB_CSQ = 0
TB_C46, TB_S1A, TB_S2A, TB_S1B, TB_S2B = 1, 2, 3, 4, 5
TB_CD, TB_S1D, TB_S2D = 6, 7, 8
TB_CI, TB_S1I, TB_S2I = 9, 10, 11
N_TABLES = 12


def _prep_kernel(cq_ref, ckv_ref, t46_ref, dq_ref, dk_ref, iq_ref, tab_ref,
                 glq_ref, glkv_ref, wuq_ref, wukv_ref, gq_ref, gkn_ref, g46_ref, gdq_ref, gdk_ref,
                 qm_ref, km_ref, vm_ref, qd_ref, kd_ref, iqp_ref, ikd_ref):
    f32, bf16 = jnp.float32, jnp.bfloat16
    tm = cq_ref.shape[0]
    lane = lax.broadcasted_iota(jnp.int32, (tm, LANES), 1)
    low = lane < 64
    tab = lambda k: tab_ref[:, LANES * k:LANES * (k + 1)]

    def rms(x, n):
        return lax.rsqrt(jnp.sum(x * x, axis=-1, keepdims=True) * (1.0 / n) + EPS)

    cq = cq_ref[...].astype(f32)
    aq = (cq * rms(cq, MLA_Q_LORA) * glq_ref[...]).astype(bf16)
    qx = _dot(aq, wuq_ref[...])
    csq = tab(TB_CSQ)
    for h in range(MLA_HEADS):
        nope = qx[:, 256 * h:256 * h + 128]
        pair = qx[:, 256 * h + 128:256 * h + 256]
        ss = jnp.sum(nope * nope, axis=-1, keepdims=True) + 0.5 * jnp.sum(pair * pair, axis=-1, keepdims=True)
        r = lax.rsqrt(ss * (1.0 / MLA_QK) + EPS)
        qm_ref[h, :, 0:128] = (nope * r * gq_ref[:, 256 * h:256 * h + 128]).astype(bf16)
        qm_ref[h, :, 128:256] = (pair * r * gq_ref[:, 256 * h + 128:256 * h + 256] * csq).astype(bf16)

    t46 = t46_ref[...].astype(f32)
    ss_pe = jnp.sum(jnp.where(low, t46 * t46, 0.0), axis=-1, keepdims=True)
    y46 = t46 * g46_ref[...]
    r46 = (y46 * tab(TB_C46)
           + _roll(y46, 96) * tab(TB_S1A) + _roll(y46, 32) * tab(TB_S2A)
           + _roll(y46, 120) * tab(TB_S1B) + _roll(y46, 8) * tab(TB_S2B))
    sw = _roll(r46, 64)
    kpe2 = jnp.where(low, r46, sw)
    ikd_ref[...] = jnp.where(low, sw, r46).astype(bf16)

    ckv = ckv_ref[...].astype(f32)
    akv = (ckv * rms(ckv, MLA_KV_LORA) * glkv_ref[...]).astype(bf16)
    kvx = _dot(akv, wukv_ref[...])
    for h in range(MLA_HEADS):
        kn = kvx[:, 256 * h:256 * h + 128]
        ss = jnp.sum(kn * kn, axis=-1, keepdims=True) + ss_pe
        r = lax.rsqrt(ss * (1.0 / MLA_QK) + EPS)
        km_ref[h, :, 0:128] = (kn * r * gkn_ref[...]).astype(bf16)
        km_ref[h, :, 128:256] = (kpe2 * r).astype(bf16)
        vm_ref[h] = kvx[:, 256 * h + 128:256 * h + 256].astype(bf16)

    cd, s1d, s2d = tab(TB_CD), tab(TB_S1D), tab(TB_S2D)

    def dsa_head(x, g):
        y = x * rms(x, DSA_HEAD_DIM) * g
        return y * cd + _roll(y, 112) * s1d + _roll(y, 16) * s2d

    for h in range(DSA_HEADS):
        x = dq_ref[:, 128 * h:128 * h + 128].astype(f32)
        qd_ref[h] = dsa_head(x, gdq_ref[...]).astype(bf16)
    for g in range(DSA_KV_HEADS):
        x = dk_ref[:, 128 * g:128 * g + 128].astype(f32)
        kd_ref[g] = dsa_head(x, gdk_ref[...]).astype(bf16)

    ci, s1i, s2i = tab(TB_CI), tab(TB_S1I), tab(TB_S2I)
    for j in range(IDX_HEADS // 2):
        x = iq_ref[:, 128 * j:128 * j + 128].astype(f32)
        y = x * ci + _roll(x, 120) * s1i + _roll(x, 8) * s2i
        iqp_ref[:, 256 * j:256 * j + 128] = jnp.where(low, y, 0.0).astype(bf16)
        iqp_ref[:, 256 * j + 128:256 * j + 256] = jnp.where(low, 0.0, y).astype(bf16)


def _prep(z, tabs, glq, glkv, wuq, wukv, gq, gkn, g46, gdq, gdk, tm):
    t = z.shape[0]
    bf16 = jnp.bfloat16
    row = lambda w, c: pl.BlockSpec((tm, w), lambda i, c=c: (i, c))
    full = lambda a: pl.BlockSpec(a.shape, lambda i: (0,) * a.ndim)
    heads = lambda n, w: pl.BlockSpec((n, tm, w), lambda i: (0, i, 0))
    return pl.pallas_call(
        _prep_kernel,
        out_shape=(
            jax.ShapeDtypeStruct((MLA_HEADS, t, 256), bf16),
            jax.ShapeDtypeStruct((MLA_HEADS, t, 256), bf16),
            jax.ShapeDtypeStruct((MLA_HEADS, t, 128), bf16),
            jax.ShapeDtypeStruct((DSA_HEADS, t, 128), bf16),
            jax.ShapeDtypeStruct((DSA_KV_HEADS, t, 128), bf16),
            jax.ShapeDtypeStruct((t, IDX_HEADS * 128), bf16),
            jax.ShapeDtypeStruct((t, 128), bf16),
        ),
        grid=(t // tm,),
        in_specs=[
            row(512, Z_CQ // 512), row(512, Z_CKV // 512), row(128, Z_KPE_IK // 128),
            row(768, Z_DQ // 768), row(256, Z_DK // 256), row(1024, Z_IQ // 1024),
            pl.BlockSpec((tm, N_TABLES * LANES), lambda i: (i, 0)),
            full(glq), full(glkv), full(wuq), full(wukv), full(gq), full(gkn), full(g46), full(gdq), full(gdk),
        ],
        out_specs=(
            heads(MLA_HEADS, 256), heads(MLA_HEADS, 256), heads(MLA_HEADS, 128),
            heads(DSA_HEADS, 128), heads(DSA_KV_HEADS, 128),
            pl.BlockSpec((tm, IDX_HEADS * 128), lambda i: (i, 0)),
            pl.BlockSpec((tm, 128), lambda i: (i, 0)),
        ),
        compiler_params=_cparams(("parallel",)),
        name="prep",
    )(z, z, z, z, z, z, tabs, glq, glkv, wuq, wukv, gq, gkn, g46, gdq, gdk)


def _silu(x):
    return x / (1.0 + jnp.exp(-x))


def _conv_kernel(cx_ref, cb_ref, cc_ref, cz_ref, hx_ref, hc_ref, w_ref, o_ref, *, tiles_per_seq):
    f32 = jnp.float32
    tm, ch = cx_ref.shape
    u = cc_ref[...].astype(f32) * cx_ref[...].astype(f32)
    keep = (pl.program_id(0) % tiles_per_seq != 0).astype(f32)
    hu = hc_ref[...].astype(f32) * hx_ref[...].astype(f32) * keep
    row = lax.broadcasted_iota(jnp.int32, (tm, ch), 0)
    u1 = jnp.where(row == 0, hu[7:8, :], pltpu.roll(u, 1, 0))
    u2 = jnp.where(row == 0, hu[6:7, :], jnp.where(row == 1, hu[7:8, :], pltpu.roll(u, 2, 0)))
    conv = w_ref[0:1, :] * u2 + w_ref[1:2, :] * u1 + w_ref[2:3, :] * u
    o_ref[...] = (cb_ref[...].astype(f32) * conv * _silu(cz_ref[...].astype(f32))).astype(o_ref.dtype)


def _conv(z, w, seq, tm):
    t = z.shape[0]
    col = lambda c: pl.BlockSpec((tm, CONV_CH), lambda i, c=c: (i, c))
    halo = lambda c: pl.BlockSpec((8, CONV_CH), lambda i, c=c: (jnp.maximum(i * (tm // 8) - 1, 0), c))
    return pl.pallas_call(
        functools.partial(_conv_kernel, tiles_per_seq=seq // tm),
        out_shape=jax.ShapeDtypeStruct((t, CONV_CH), jnp.bfloat16),
        grid=(t // tm,),
        in_specs=[col(Z_CX // CONV_CH), col(Z_CB // CONV_CH), col(Z_CC // CONV_CH), col(Z_CONVZ // CONV_CH),
                  halo(Z_CX // CONV_CH), halo(Z_CC // CONV_CH),
                  pl.BlockSpec((8, CONV_CH), lambda i: (0, 0))],
        out_specs=pl.BlockSpec((tm, CONV_CH), lambda i: (i, 0)),
        compiler_params=_cparams(("parallel",)),
        name="conv",
    )(z, z, z, z, z, z, w)


def _tile_lanes(x, n):
    return x if n == 1 else jnp.concatenate([x] * n, axis=1)


def _softmax_step(s, v2, m_ref, acc_ref):
    m_prev = m_ref[...]
    m_new = jnp.maximum(m_prev, jnp.max(s, axis=-1, keepdims=True))
    p = jnp.exp2(s - _tile_lanes(m_new, s.shape[1] // LANES))
    alpha = jnp.exp2(m_prev - m_new)
    acc_ref[...] = _tile_lanes(alpha, 2) * acc_ref[...] + _dot(p.astype(jnp.bfloat16), v2)
    m_ref[...] = m_new


def _with_ones(v):
    return jnp.concatenate([v, jnp.ones_like(v)], axis=1)


MLA_HEADS_PER_STEP = 6


def _mla_kernel(q_ref, k_ref, v_ref, zg_ref, o_ref, m_ref, acc_ref):
    f32 = jnp.float32
    tq = q_ref.shape[1]
    qi = pl.program_id(2)
    m_ref[...] = jnp.full(m_ref.shape, NEG_BIG, f32)
    acc_ref[...] = jnp.zeros(acc_ref.shape, f32)

    def step(ki, masked):
        rows = pl.ds(pl.multiple_of(ki * tq, tq), tq)
        for h in range(MLA_HEADS_PER_STEP):
            s = _dot_nt(q_ref[h], k_ref[h, rows, :])
            if masked:
                r = lax.broadcasted_iota(jnp.int32, s.shape, 0)
                c = lax.broadcasted_iota(jnp.int32, s.shape, 1)
                s = jnp.where(c <= r, s, 2.0 * NEG_BIG)
            _softmax_step(s, _with_ones(v_ref[h, rows, :]), m_ref.at[h], acc_ref.at[h])

    def body(ki, c):
        step(ki, False)
        return c

    lax.fori_loop(0, qi, body, 0)
    step(qi, True)
    for h in range(MLA_HEADS_PER_STEP):
        o = acc_ref[h, :, 0:128] / acc_ref[h, :, 128:256]
        zg = zg_ref[:, 128 * h:128 * h + 128].astype(f32)
        o_ref[:, 128 * h:128 * h + 128] = (o * _silu(zg)).astype(o_ref.dtype)


def _mla(qm, km, vm, z, batch, seq, tq):
    t = z.shape[0]
    nq = seq // tq
    hs = MLA_HEADS_PER_STEP
    return pl.pallas_call(
        _mla_kernel,
        out_shape=jax.ShapeDtypeStruct((t, MLA_WIDTH), jnp.bfloat16),
        grid=(batch, MLA_HEADS // hs, nq),
        in_specs=[
            pl.BlockSpec((hs, tq, 256), lambda b, h, i: (h, b * nq + i, 0)),
            pl.BlockSpec((hs, seq, 256), lambda b, h, i: (h, b, 0)),
            pl.BlockSpec((hs, seq, 128), lambda b, h, i: (h, b, 0)),
            pl.BlockSpec((tq, 128 * hs), lambda b, h, i: (b * nq + i, Z_MLAZ // (128 * hs) + h)),
        ],
        out_specs=pl.BlockSpec((tq, 128 * hs), lambda b, h, i: (b * nq + i, h)),
        scratch_shapes=[pltpu.VMEM((hs, tq, LANES), jnp.float32), pltpu.VMEM((hs, tq, 256), jnp.float32)],
        compiler_params=_cparams(("parallel", "parallel", "arbitrary")),
        name="mla",
    )(qm, km, vm, z)


KC = 256
ATT_KC = 512


def _dsa_kernel(iq_ref, iw_ref, ik_ref, q_ref, k_ref, v0_ref, v1_ref, zg_ref, o_ref,
                key_ref, keyt_ref, wb_ref, m_ref, acc_ref, *, k_top):
    f32, i32, bf16 = jnp.float32, jnp.int32, jnp.bfloat16
    tq = iq_ref.shape[0]
    i = pl.program_id(1)
    n_chunks = ((i + 1) * tq + KC - 1) // KC
    q_pos = i * tq + lax.broadcasted_iota(i32, (tq, KC), 0)
    lane_kc = lax.broadcasted_iota(i32, (tq, KC), 1)

    w = iw_ref[...].astype(f32) * (IDX_HEADS ** -0.5)
    for h in range(IDX_HEADS):
        wb_ref[h] = jnp.broadcast_to(w[:, h:h + 1], (tq, KC))

    def score_chunk(c, carry):
        start = pl.multiple_of(c * KC, KC)
        ikc = ik_ref[pl.ds(start, KC), :]
        acc = jnp.zeros((tq, KC), f32)
        for h in range(IDX_HEADS):
            lg = _dot_nt(iq_ref[:, 128 * h:128 * h + 128], ikc)
            acc = acc + wb_ref[h] * jnp.maximum(lg, 0.0)
        bits = lax.bitcast_convert_type(acc, i32)
        key = jnp.where(bits < 0, bits ^ jnp.int32(0x7FFFFFFF), bits)
        key = jnp.where(start + lane_kc <= q_pos, key, jnp.int32(INT_MIN))
        key_ref[:, pl.ds(start, KC)] = key
        keyt_ref[pl.ds(start, KC), :] = key.T
        return carry

    lax.fori_loop(0, n_chunks, score_chunk, 0)

    def count_where(hits):
        def body(c, cnt):
            start = pl.multiple_of(c * KC, KC)
            hit = hits(keyt_ref[pl.ds(start, KC), :], start)
            return cnt + jnp.sum(hit.reshape(KC // 8, 8, tq), axis=0)
        cnt = lax.fori_loop(0, n_chunks, body, jnp.zeros((8, tq), i32))
        return jnp.sum(cnt, axis=0, keepdims=True)

    def bit_step(it, r):
        cand = r + lax.shift_left(jnp.int32(1), 31 - it)
        return jnp.where(count_where(lambda kt, _: jnp.where(kt >= cand, 1, 0)) >= k_top, cand, r)

    thr = lax.fori_loop(0, 32, bit_step, jnp.full((1, tq), INT_MIN, i32))

    n_ge = count_where(lambda kt, _: jnp.where(kt >= thr, 1, 0))
    has_excess = jnp.max(n_ge) > k_top

    @pl.when(has_excess)
    def _():
        need = k_top - count_where(lambda kt, _: jnp.where(kt > thr, 1, 0))
        sub = lax.broadcasted_iota(i32, (KC, tq), 0)

        def idx_step(it, bound):
            cand = bound + lax.shift_left(jnp.int32(1), 30 - it)
            n_below = count_where(lambda kt, start: jnp.where(kt == thr, jnp.where(start + sub < cand, 1, 0), 0))
            return jnp.where(n_below <= need, cand, bound)

        bound = lax.fori_loop(0, 31, idx_step, jnp.zeros((1, tq), i32))

        def demote(c, carry):
            start = pl.multiple_of(c * KC, KC)
            kt = keyt_ref[pl.ds(start, KC), :]
            drop = jnp.where(kt == thr, jnp.where(start + sub >= bound, 1, 0), 0)
            kt = jnp.where(drop == 1, jnp.int32(INT_MIN), kt)
            keyt_ref[pl.ds(start, KC), :] = kt
            key_ref[:, pl.ds(start, KC)] = kt.T
            return carry

        lax.fori_loop(0, n_chunks, demote, 0)

    thr_sel = jnp.maximum(thr, jnp.int32(INT_MIN + 1))
    thr_kc = _tile_lanes(jnp.broadcast_to(thr_sel, (LANES, tq)).T, KC // LANES)

    m_ref[...] = jnp.full(m_ref.shape, NEG_BIG, f32)
    acc_ref[...] = jnp.zeros(acc_ref.shape, f32)
    v_refs = (v0_ref, v1_ref)

    n_att = (n_chunks * KC + ATT_KC - 1) // ATT_KC

    def fill(c, carry):
        key_ref[:, pl.ds(pl.multiple_of(c * KC, KC), KC)] = jnp.full((tq, KC), INT_MIN, i32)
        return carry

    lax.fori_loop(n_chunks, n_att * (ATT_KC // KC), fill, 0)
    thr_att = _tile_lanes(thr_kc, ATT_KC // KC)

    def attn_chunk(c, carry):
        start = pl.multiple_of(c * ATT_KC, ATT_KC)
        bias = jnp.where(key_ref[:, pl.ds(start, ATT_KC)] >= thr_att, 0.0, 2.0 * NEG_BIG)
        for hd in range(DSA_HEADS):
            g = hd // DSA_GROUP
            s = _dot_nt(q_ref[hd], k_ref[g, pl.ds(start, ATT_KC), :]) + bias
            _softmax_step(s, _with_ones(v_refs[g][pl.ds(start, ATT_KC), :]), m_ref.at[hd], acc_ref.at[hd])
        return carry

    lax.fori_loop(0, n_att, attn_chunk, 0)

    for hd in range(DSA_HEADS):
        o = acc_ref[hd, :, 0:128] / acc_ref[hd, :, 128:256]
        zg = zg_ref[:, 128 * hd:128 * hd + 128].astype(f32)
        o_ref[:, 128 * hd:128 * hd + 128] = (o * _silu(zg)).astype(o_ref.dtype)


def _dsa(iqp, ikd, qd, kd, z, batch, seq, tq, k_top):
    t = z.shape[0]
    nq = seq // tq
    f32 = jnp.float32
    return pl.pallas_call(
        functools.partial(_dsa_kernel, k_top=k_top),
        out_shape=jax.ShapeDtypeStruct((t, DSA_WIDTH), jnp.bfloat16),
        grid=(batch, nq),
        in_specs=[
            pl.BlockSpec((tq, IDX_HEADS * 128), lambda b, i: (b * nq + i, 0)),
            pl.BlockSpec((tq, 128), lambda b, i: (b * nq + i, Z_IW // 128)),
            pl.BlockSpec((seq, 128), lambda b, i: (b, 0)),
            pl.BlockSpec((DSA_HEADS, tq, 128), lambda b, i: (0, b * nq + i, 0)),
            pl.BlockSpec((DSA_KV_HEADS, seq, 128), lambda b, i: (0, b, 0)),
            pl.BlockSpec((seq, 128), lambda b, i: (b, Z_DV // 128)),
            pl.BlockSpec((seq, 128), lambda b, i: (b, Z_DV // 128 + 1)),
            pl.BlockSpec((tq, DSA_WIDTH), lambda b, i: (b * nq + i, Z_DSAZ // DSA_WIDTH)),
        ],
        out_specs=pl.BlockSpec((tq, DSA_WIDTH), lambda b, i: (b * nq + i, 0)),
        scratch_shapes=[
            pltpu.VMEM((tq, seq), jnp.int32),
            pltpu.VMEM((seq, tq), jnp.int32),
            pltpu.VMEM((IDX_HEADS, tq, KC), f32),
            pltpu.VMEM((DSA_HEADS, tq, LANES), f32),
            pltpu.VMEM((DSA_HEADS, tq, 2 * DSA_HEAD_DIM), f32),
        ],
        compiler_params=_cparams(("parallel", "arbitrary")),
        name="dsa",
    )(iqp, z, ikd, qd, kd, z, z, z)


def _post_kernel(h_ref, ya_ref, yb_ref, yc_ref, p_ref, g_ref, wa_ref, wb_ref, wc_ref, wg_ref, wp_ref, o_ref):
    h1 = h_ref[...] + _dot(ya_ref[...], wa_ref[...])
    h1 = h1 + _dot(yb_ref[...], wb_ref[...])
    h1 = h1 + _dot(yc_ref[...], wc_ref[...])
    r = lax.rsqrt(jnp.mean(h1 * h1, axis=-1, keepdims=True) + EPS)
    a = (h1 * r * g_ref[...]).astype(jnp.bfloat16)
    gate = _dot(a, wg_ref[...])
    gate = 1.0 / (1.0 + jnp.exp(-gate))
    o_ref[...] = h1 + gate * _dot(p_ref[...].astype(jnp.bfloat16), wp_ref[...])


def _post(h, ya, yb, yc, p, g, wa, wb, wc, wg, wp, tm):
    t, d = h.shape
    rows = lambda a: pl.BlockSpec((tm, a.shape[1]), lambda i: (i, 0))
    whole = lambda a: pl.BlockSpec(a.shape, lambda i: (0, 0), pipeline_mode=pl.Buffered(1))
    return pl.pallas_call(
        _post_kernel,
        out_shape=jax.ShapeDtypeStruct((t, d), jnp.float32),
        grid=(t // tm,),
        in_specs=[rows(h), rows(ya), rows(yb), rows(yc), rows(p), whole(g),
                  whole(wa), whole(wb), whole(wc), whole(wg), whole(wp)],
        out_specs=rows(h),
        compiler_params=_cparams(("parallel",)),
        name="post",
    )(h, ya, yb, yc, p, g, wa, wb, wc, wg, wp)


def _regroup_w_in(w):
    d = w.shape[0]
    w = w.astype(jnp.bfloat16)
    seg = lambda a, b: w[:, a:b]
    parts = [
        seg(_O_CQ, _O_CKV), seg(_O_CKV, _O_KPE),
        seg(_O_CX, _O_CB), seg(_O_CB, _O_CC), seg(_O_CC, _O_CONVZ), seg(_O_CONVZ, _O_DQ),
        seg(_O_MLAZ, _O_CX), seg(_O_DQ, _O_DK), seg(_O_DSAZ, _O_IQ),
        seg(_O_DK, _O_DV), seg(_O_DV, _O_DSAZ),
        seg(_O_KPE, _O_MLAZ), seg(_O_IK, _N_IN), seg(_O_IW, _O_IK),
        jnp.zeros((d, Z_IQ - Z_IW - IDX_HEADS), w.dtype),
        seg(_O_IQ, _O_IW),
    ]
    out = jnp.concatenate(parts, axis=1).astype(jnp.bfloat16)
    assert out.shape[1] == Z_WIDTH
    return out


def _pair_layout(a):
    lead = a.shape[:-1]
    a = a.reshape(lead + (MLA_HEADS, MLA_QK))
    nope, x1, x2 = a[..., :MLA_NOPE], a[..., MLA_NOPE:MLA_NOPE + 32], a[..., MLA_NOPE + 32:]
    return jnp.concatenate([nope, x1, x2, x2, x1], axis=-1).reshape(lead + (MLA_HEADS * 256,))


_B_C32, _B_S32, _B_C16, _B_S16, _B_C8, _B_S8, _B_ONE, _B_ROWS = 0, 32, 64, 80, 96, 104, 112, 128


def _table_selector():
    sel = np.zeros((_B_ROWS, N_TABLES * LANES), np.float32)

    def put(table, lane, base, n, coef=1.0):
        for j in range(n):
            sel[base + (j if base != _B_ONE else 0), table * LANES + lane + j] = coef

    c32, s32, c16, s16, c8, s8, one = _B_C32, _B_S32, _B_C16, _B_S16, _B_C8, _B_S8, _B_ONE
    put(TB_CSQ, 0, c32, 32); put(TB_CSQ, 32, c32, 32); put(TB_CSQ, 64, s32, 32, -1.0); put(TB_CSQ, 96, s32, 32)
    put(TB_C46, 0, c32, 32); put(TB_C46, 32, c32, 32); put(TB_C46, 64, c8, 8); put(TB_C46, 72, c8, 8)
    put(TB_C46, 80, one, 48)
    put(TB_S1A, 0, s32, 32, -1.0)
    put(TB_S2A, 32, s32, 32)
    put(TB_S1B, 64, s8, 8, -1.0)
    put(TB_S2B, 72, s8, 8)
    put(TB_CD, 0, c16, 16); put(TB_CD, 16, c16, 16); put(TB_CD, 32, one, 96)
    put(TB_S1D, 0, s16, 16, -1.0)
    put(TB_S2D, 16, s16, 16)
    qs = IDX_DIM ** -0.5
    for off in (0, 64):
        put(TB_CI, off, c8, 8, qs); put(TB_CI, off + 8, c8, 8, qs); put(TB_CI, off + 16, one, 48, qs)
        put(TB_S1I, off, s8, 8, -qs)
        put(TB_S2I, off + 8, s8, 8, qs)
    return sel


def _rope_tables(positions):
    f32 = jnp.float32
    pos = positions.reshape(1, -1).astype(f32)
    t = pos.shape[1]
    inv = jnp.concatenate([ROPE_THETA ** (-jnp.arange(half, dtype=f32) / half)
                           for half in (MLA_ROPE // 2, DSA_ROT // 2, IDX_ROT // 2)])
    ang = inv[:, None] * pos
    c, s = jnp.cos(ang), jnp.sin(ang)
    basis = jnp.concatenate([c[0:32], s[0:32], c[32:48], s[32:48], c[48:56], s[48:56],
                             jnp.ones((1, t), f32), jnp.zeros((_B_ROWS - _B_ONE - 1, t), f32)], axis=0)
    return lax.dot_general(basis, jnp.asarray(_table_selector()), (((0,), (0,)), ((), ())),
                           precision=lax.Precision.HIGHEST)


def _layer(h, p_i, tabs, batch, seq, k_top, norm_in, w_in, mla_gq, mla_w_uq, mla_gkv, mla_w_ukv, mla_qn, mla_kn,
           conv_w, dsa_qn, dsa_kn, w_out, ple_norm, ple_w_gate, ple_w_proj):
    f32, bf16 = jnp.float32, jnp.bfloat16
    t = h.shape[0]
    tm_big = min(1024, t)

    z = _inproj(h, norm_in[None, :], _regroup_w_in(w_in), tm_big, 1024)

    gq = _pair_layout(jnp.tile(mla_qn, MLA_HEADS))[None, :] * (MLA_QK ** -0.5 * LOG2E)
    g46 = jnp.concatenate([mla_kn[MLA_NOPE:], jnp.ones((64,), f32)])[None, :]
    qm, km, vm, qd, kd, iqp, ikd = _prep(
        z, tabs, mla_gq[None, :], mla_gkv[None, :], _pair_layout(mla_w_uq).astype(bf16), mla_w_ukv.astype(bf16),
        gq, mla_kn[None, :MLA_NOPE], g46, dsa_qn[None, :] * (DSA_HEAD_DIM ** -0.5 * LOG2E), dsa_kn[None, :], 256)

    y_b = _conv(z, jnp.pad(conv_w, ((0, 8 - CONV_K), (0, 0))), seq, 512)
    y_a = _mla(qm, km, vm, z, batch, seq, 512)
    y_c = _dsa(iqp, ikd, qd, kd, z, batch, seq, 256, k_top)

    wo = w_out.astype(bf16)
    return _post(h, y_a, y_b, y_c, p_i, ple_norm[None, :],
                 wo[:MLA_WIDTH], wo[MLA_WIDTH:MLA_WIDTH + CONV_CH], wo[MLA_WIDTH + CONV_CH:],
                 ple_w_gate.astype(bf16), ple_w_proj.astype(bf16), min(512, t))


def kernel(x, p, positions, norm_in, w_in, mla_gq, mla_w_uq, mla_gkv, mla_w_ukv, mla_qn, mla_kn, conv_w, dsa_qn,
           dsa_kn, w_out, ple_norm, ple_w_gate, ple_w_proj):
    batch, seq, d = x.shape
    depth = p.shape[0]
    t = batch * seq
    k_top = min(TOPK_MAX, seq // 4)
    tabs = _rope_tables(positions)
    h = x.reshape(t, d)
    for i in range(depth):
        h = _layer(h, p[i].reshape(t, PLE_DIM), tabs, batch, seq, k_top, norm_in[i], w_in[i], mla_gq[i],
                   mla_w_uq[i], mla_gkv[i], mla_w_ukv[i], mla_qn[i], mla_kn[i], conv_w[i], dsa_qn[i], dsa_kn[i],
                   w_out[i], ple_norm[i], ple_w_gate[i], ple_w_proj[i])
    return h.reshape(batch, seq, d)
```

```python
import functools

import numpy as np
import jax
import jax.numpy as jnp
from jax import lax
from jax.experimental import pallas as pl
from jax.experimental.pallas import tpu as pltpu

D_MODEL = 2048
PLE_DIM = 256
ROPE_THETA = 500000.0
EPS = 1e-6

MLA_HEADS = 6
MLA_Q_LORA = 512
MLA_KV_LORA = 512
MLA_NOPE = 128
MLA_ROPE = 64
MLA_V = 128
MLA_QK = MLA_NOPE + MLA_ROPE
MLA_WIDTH = MLA_HEADS * MLA_V

CONV_CH = 512
CONV_K = 3

DSA_HEADS = 6
DSA_KV_HEADS = 2
DSA_GROUP = DSA_HEADS // DSA_KV_HEADS
DSA_HEAD_DIM = 128
DSA_WIDTH = DSA_HEADS * DSA_HEAD_DIM
DSA_ROT = DSA_HEAD_DIM // 4
IDX_HEADS = 16
IDX_DIM = 64
IDX_ROT = IDX_DIM // 4
TOPK_MAX = 256

LANES = 128

_O_CQ, _O_CKV, _O_KPE, _O_MLAZ = 0, 512, 1024, 1088
_O_CX, _O_CB, _O_CC, _O_CONVZ = 1856, 2368, 2880, 3392
_O_DQ, _O_DK, _O_DV, _O_DSAZ = 3904, 4672, 4928, 5184
_O_IQ, _O_IW, _O_IK, _N_IN = 5952, 6976, 6992, 7056

Z_CQ, Z_CKV = 0, 512
Z_CX, Z_CB, Z_CC, Z_CONVZ = 1024, 1536, 2048, 2560
Z_MLAZ, Z_DQ, Z_DSAZ = 3072, 3840, 4608
Z_DK, Z_DV = 5376, 5632
Z_KPE_IK, Z_IW, Z_IQ = 5888, 6016, 6144
Z_WIDTH = 7168

LOG2E = 1.4426950408889634
INT_MIN = -(2 ** 31)
NEG_BIG = -1e30

VMEM_LIMIT = 56 * 1024 * 1024


def _cparams(sem):
    return pltpu.CompilerParams(dimension_semantics=sem, vmem_limit_bytes=VMEM_LIMIT)


def _dot(a, b):
    return jnp.dot(a, b, preferred_element_type=jnp.float32)


def _dot_nt(a, b):
    return lax.dot_general(a, b, (((1,), (1,)), ((), ())), preferred_element_type=jnp.float32)


def _roll(x, shift):
    return pltpu.roll(x, shift, 1)


def _inproj_kernel(x_ref, g_ref, w_ref, o_ref, a_ref):
    @pl.when(pl.program_id(1) == 0)
    def _():
        x = x_ref[...]
        r = lax.rsqrt(jnp.mean(x * x, axis=-1, keepdims=True) + EPS)
        a_ref[...] = (x * r * g_ref[...]).astype(jnp.bfloat16)

    o_ref[...] = _dot(a_ref[...], w_ref[...]).astype(o_ref.dtype)


def _inproj(h, g, w, tm, tn):
    t, d = h.shape
    n = w.shape[1]
    return pl.pallas_call(
        _inproj_kernel,
        out_shape=jax.ShapeDtypeStruct((t, n), jnp.bfloat16),
        grid=(t // tm, n // tn),
        in_specs=[
            pl.BlockSpec((tm, d), lambda i, j: (i, 0)),
            pl.BlockSpec((1, d), lambda i, j: (0, 0)),
            pl.BlockSpec((d, tn), lambda i, j: (0, j)),
        ],
        out_specs=pl.BlockSpec((tm, tn), lambda i, j: (i, j)),
        scratch_shapes=[pltpu.VMEM((tm, d), jnp.bfloat16)],
        compiler_params=_cparams(("parallel", "arbitrary")),
        name="inproj",
    )(h, g, w)


TB_CSQ = 0
TB_C46, TB_S1A, TB_S2A, TB_S1B, TB_S2B = 1, 2, 3, 4, 5
TB_CD, TB_S1D, TB_S2D = 6, 7, 8
TB_CI, TB_S1I, TB_S2I = 9, 10, 11
N_TABLES = 12


def _prep_kernel(cq_ref, ckv_ref, t46_ref, dq_ref, dk_ref, iq_ref, basis_ref, sel_ref,
                 glq_ref, glkv_ref, wuq_ref, wukv_ref, gq_ref, gkn_ref, g46_ref, gdq_ref, gdk_ref,
                 qm_ref, km_ref, vm_ref, qd_ref, kd_ref, iqp_ref, ikd_ref):
    f32, bf16 = jnp.float32, jnp.bfloat16
    tm = cq_ref.shape[0]
    lane = lax.broadcasted_iota(jnp.int32, (tm, LANES), 1)
    low = lane < 64

    b0 = basis_ref[...]
    b_hi = b0.astype(bf16)
    b1 = b0 - b_hi.astype(f32)
    b_mid = b1.astype(bf16)
    b_lo = (b1 - b_mid.astype(f32)).astype(bf16)
    tabs = _dot(b_hi, sel_ref[...]) + _dot(b_mid, sel_ref[...]) + _dot(b_lo, sel_ref[...])
    tab = lambda k: tabs[:, LANES * k:LANES * (k + 1)]

    def rms(x, n):
        return lax.rsqrt(jnp.sum(x * x, axis=-1, keepdims=True) * (1.0 / n) + EPS)

    cq = cq_ref[...].astype(f32)
    aq = (cq * rms(cq, MLA_Q_LORA) * glq_ref[...]).astype(bf16)
    qx = _dot(aq, wuq_ref[...])
    csq = tab(TB_CSQ)
    for h in range(MLA_HEADS):
        nope = qx[:, 256 * h:256 * h + 128]
        pair = qx[:, 256 * h + 128:256 * h + 256]
        ss = jnp.sum(nope * nope, axis=-1, keepdims=True) + 0.5 * jnp.sum(pair * pair, axis=-1, keepdims=True)
        r = lax.rsqrt(ss * (1.0 / MLA_QK) + EPS)
        qm_ref[h, :, 0:128] = (nope * r * gq_ref[:, 256 * h:256 * h + 128]).astype(bf16)
        qm_ref[h, :, 128:256] = (pair * r * gq_ref[:, 256 * h + 128:256 * h + 256] * csq).astype(bf16)

    t46 = t46_ref[...].astype(f32)
    ss_pe = jnp.sum(jnp.where(low, t46 * t46, 0.0), axis=-1, keepdims=True)
    y46 = t46 * g46_ref[...]
    r46 = (y46 * tab(TB_C46)
           + _roll(y46, 96) * tab(TB_S1A) + _roll(y46, 32) * tab(TB_S2A)
           + _roll(y46, 120) * tab(TB_S1B) + _roll(y46, 8) * tab(TB_S2B))
    sw = _roll(r46, 64)
    kpe2 = jnp.where(low, r46, sw)
    ikd_ref[...] = jnp.where(low, sw, r46).astype(bf16)

    ckv = ckv_ref[...].astype(f32)
    akv = (ckv * rms(ckv, MLA_KV_LORA) * glkv_ref[...]).astype(bf16)
    kvx = _dot(akv, wukv_ref[...])
    for h in range(MLA_HEADS):
        kn = kvx[:, 256 * h:256 * h + 128]
        ss = jnp.sum(kn * kn, axis=-1, keepdims=True) + ss_pe
        r = lax.rsqrt(ss * (1.0 / MLA_QK) + EPS)
        km_ref[h, :, 0:128] = (kn * r * gkn_ref[...]).astype(bf16)
        km_ref[h, :, 128:256] = (kpe2 * r).astype(bf16)
        vm_ref[h] = kvx[:, 256 * h + 128:256 * h + 256].astype(bf16)

    cd, s1d, s2d = tab(TB_CD), tab(TB_S1D), tab(TB_S2D)

    def dsa_head(x, g):
        y = x * rms(x, DSA_HEAD_DIM) * g
        return y * cd + _roll(y, 112) * s1d + _roll(y, 16) * s2d

    for h in range(DSA_HEADS):
        x = dq_ref[:, 128 * h:128 * h + 128].astype(f32)
        qd_ref[h] = dsa_head(x, gdq_ref[...]).astype(bf16)
    for g in range(DSA_KV_HEADS):
        x = dk_ref[:, 128 * g:128 * g + 128].astype(f32)
        kd_ref[g] = dsa_head(x, gdk_ref[...]).astype(bf16)

    ci, s1i, s2i = tab(TB_CI), tab(TB_S1I), tab(TB_S2I)
    for j in range(IDX_HEADS // 2):
        x = iq_ref[:, 128 * j:128 * j + 128].astype(f32)
        y = x * ci + _roll(x, 120) * s1i + _roll(x, 8) * s2i
        iqp_ref[:, 256 * j:256 * j + 128] = jnp.where(low, y, 0.0).astype(bf16)
        iqp_ref[:, 256 * j + 128:256 * j + 256] = jnp.where(low, 0.0, y).astype(bf16)


def _prep(z, basis, sel, glq, glkv, wuq, wukv, gq, gkn, g46, gdq, gdk, tm):
    t = z.shape[0]
    bf16 = jnp.bfloat16
    row = lambda w, c: pl.BlockSpec((tm, w), lambda i, c=c: (i, c))
    full = lambda a: pl.BlockSpec(a.shape, lambda i: (0,) * a.ndim)
    heads = lambda n, w: pl.BlockSpec((n, tm, w), lambda i: (0, i, 0))
    return pl.pallas_call(
        _prep_kernel,
        out_shape=(
            jax.ShapeDtypeStruct((MLA_HEADS, t, 256), bf16),
            jax.ShapeDtypeStruct((MLA_HEADS, t, 256), bf16),
            jax.ShapeDtypeStruct((MLA_HEADS, t, 128), bf16),
            jax.ShapeDtypeStruct((DSA_HEADS, t, 128), bf16),
            jax.ShapeDtypeStruct((DSA_KV_HEADS, t, 128), bf16),
            jax.ShapeDtypeStruct((t, IDX_HEADS * 128), bf16),
            jax.ShapeDtypeStruct((t, 128), bf16),
        ),
        grid=(t // tm,),
        in_specs=[
            row(512, Z_CQ // 512), row(512, Z_CKV // 512), row(128, Z_KPE_IK // 128),
            row(768, Z_DQ // 768), row(256, Z_DK // 256), row(1024, Z_IQ // 1024),
            pl.BlockSpec((tm, LANES), lambda i: (i, 0)), full(sel),
            full(glq), full(glkv), full(wuq), full(wukv), full(gq), full(gkn), full(g46), full(gdq), full(gdk),
        ],
        out_specs=(
            heads(MLA_HEADS, 256), heads(MLA_HEADS, 256), heads(MLA_HEADS, 128),
            heads(DSA_HEADS, 128), heads(DSA_KV_HEADS, 128),
            pl.BlockSpec((tm, IDX_HEADS * 128), lambda i: (i, 0)),
            pl.BlockSpec((tm, 128), lambda i: (i, 0)),
        ),
        compiler_params=_cparams(("parallel",)),
        name="prep",
    )(z, z, z, z, z, z, basis, sel, glq, glkv, wuq, wukv, gq, gkn, g46, gdq, gdk)


def _silu(x):
    return x / (1.0 + jnp.exp(-x))


def _conv_kernel(cx_ref, cb_ref, cc_ref, cz_ref, hx_ref, hc_ref, w_ref, o_ref, *, tiles_per_seq):
    f32 = jnp.float32
    tm, ch = cx_ref.shape
    u = cc_ref[...].astype(f32) * cx_ref[...].astype(f32)
    keep = (pl.program_id(0) % tiles_per_seq != 0).astype(f32)
    hu = hc_ref[...].astype(f32) * hx_ref[...].astype(f32) * keep
    row = lax.broadcasted_iota(jnp.int32, (tm, ch), 0)
    u1 = jnp.where(row == 0, hu[7:8, :], pltpu.roll(u, 1, 0))
    u2 = jnp.where(row == 0, hu[6:7, :], jnp.where(row == 1, hu[7:8, :], pltpu.roll(u, 2, 0)))
    conv = w_ref[0:1, :] * u2 + w_ref[1:2, :] * u1 + w_ref[2:3, :] * u
    o_ref[...] = (cb_ref[...].astype(f32) * conv * _silu(cz_ref[...].astype(f32))).astype(o_ref.dtype)


def _conv(z, w, seq, tm):
    t = z.shape[0]
    col = lambda c: pl.BlockSpec((tm, CONV_CH), lambda i, c=c: (i, c))
    halo = lambda c: pl.BlockSpec((8, CONV_CH), lambda i, c=c: (jnp.maximum(i * (tm // 8) - 1, 0), c))
    return pl.pallas_call(
        functools.partial(_conv_kernel, tiles_per_seq=seq // tm),
        out_shape=jax.ShapeDtypeStruct((t, CONV_CH), jnp.bfloat16),
        grid=(t // tm,),
        in_specs=[col(Z_CX // CONV_CH), col(Z_CB // CONV_CH), col(Z_CC // CONV_CH), col(Z_CONVZ // CONV_CH),
                  halo(Z_CX // CONV_CH), halo(Z_CC // CONV_CH),
                  pl.BlockSpec((8, CONV_CH), lambda i: (0, 0))],
        out_specs=pl.BlockSpec((tm, CONV_CH), lambda i: (i, 0)),
        compiler_params=_cparams(("parallel",)),
        name="conv",
    )(z, z, z, z, z, z, w)


def _tile_lanes(x, n):
    return x if n == 1 else jnp.concatenate([x] * n, axis=1)


def _softmax_step(s, v2, m_ref, acc_ref):
    m_prev = m_ref[...]
    m_new = jnp.maximum(m_prev, jnp.max(s, axis=-1, keepdims=True))
    p = jnp.exp2(s - _tile_lanes(m_new, s.shape[1] // LANES))
    alpha = jnp.exp2(m_prev - m_new)
    acc_ref[...] = _tile_lanes(alpha, 2) * acc_ref[...] + _dot(p.astype(jnp.bfloat16), v2)
    m_ref[...] = m_new


def _with_ones(v):
    return jnp.concatenate([v, jnp.ones_like(v)], axis=1)


MLA_HEADS_PER_STEP = 6


def _mla_kernel(q_ref, k_ref, v_ref, zg_ref, o_ref, m_ref, acc_ref):
    f32 = jnp.float32
    tq = q_ref.shape[1]
    qi = pl.program_id(2)
    m_ref[...] = jnp.full(m_ref.shape, NEG_BIG, f32)
    acc_ref[...] = jnp.zeros(acc_ref.shape, f32)

    def step(ki, masked):
        rows = pl.ds(pl.multiple_of(ki * tq, tq), tq)
        for h in range(MLA_HEADS_PER_STEP):
            s = _dot_nt(q_ref[h], k_ref[h, rows, :])
            if masked:
                r = lax.broadcasted_iota(jnp.int32, s.shape, 0)
                c = lax.broadcasted_iota(jnp.int32, s.shape, 1)
                s = jnp.where(c <= r, s, 2.0 * NEG_BIG)
            _softmax_step(s, _with_ones(v_ref[h, rows, :]), m_ref.at[h], acc_ref.at[h])

    def body(ki, c):
        step(ki, False)
        return c

    lax.fori_loop(0, qi, body, 0)
    step(qi, True)
    for h in range(MLA_HEADS_PER_STEP):
        o = acc_ref[h, :, 0:128] / acc_ref[h, :, 128:256]
        zg = zg_ref[:, 128 * h:128 * h + 128].astype(f32)
        o_ref[:, 128 * h:128 * h + 128] = (o * _silu(zg)).astype(o_ref.dtype)


def _mla(qm, km, vm, z, batch, seq, tq):
    t = z.shape[0]
    nq = seq // tq
    hs = MLA_HEADS_PER_STEP
    return pl.pallas_call(
        _mla_kernel,
        out_shape=jax.ShapeDtypeStruct((t, MLA_WIDTH), jnp.bfloat16),
        grid=(batch, MLA_HEADS // hs, nq),
        in_specs=[
            pl.BlockSpec((hs, tq, 256), lambda b, h, i: (h, b * nq + i, 0)),
            pl.BlockSpec((hs, seq, 256), lambda b, h, i: (h, b, 0)),
            pl.BlockSpec((hs, seq, 128), lambda b, h, i: (h, b, 0)),
            pl.BlockSpec((tq, 128 * hs), lambda b, h, i: (b * nq + i, Z_MLAZ // (128 * hs) + h)),
        ],
        out_specs=pl.BlockSpec((tq, 128 * hs), lambda b, h, i: (b * nq + i, h)),
        scratch_shapes=[pltpu.VMEM((hs, tq, LANES), jnp.float32), pltpu.VMEM((hs, tq, 256), jnp.float32)],
        compiler_params=_cparams(("parallel", "parallel", "arbitrary")),
        name="mla",
    )(qm, km, vm, z)


KC = 256
ATT_KC = 512


def _dsa_kernel(iq_ref, iw_ref, ik_ref, q_ref, k_ref, v0_ref, v1_ref, zg_ref, o_ref,
                key_ref, keyt_ref, hi_ref, lo_ref, wb_ref, m_ref, acc_ref, *, k_top):
    f32, i32, bf16 = jnp.float32, jnp.int32, jnp.bfloat16
    tq = iq_ref.shape[0]
    i = pl.program_id(1)
    n_chunks = ((i + 1) * tq + KC - 1) // KC
    q_pos = i * tq + lax.broadcasted_iota(i32, (tq, KC), 0)
    lane_kc = lax.broadcasted_iota(i32, (tq, KC), 1)

    w = iw_ref[...].astype(f32) * (IDX_HEADS ** -0.5)
    for h in range(IDX_HEADS):
        wb_ref[h] = jnp.broadcast_to(w[:, h:h + 1], (tq, KC))

    def score_chunk(c, carry):
        start = pl.multiple_of(c * KC, KC)
        ikc = ik_ref[pl.ds(start, KC), :]
        acc = jnp.zeros((tq, KC), f32)
        for h in range(IDX_HEADS):
            lg = _dot_nt(iq_ref[:, 128 * h:128 * h + 128], ikc)
            acc = acc + wb_ref[h] * jnp.maximum(lg, 0.0)
        bits = lax.bitcast_convert_type(acc, i32)
        key = jnp.where(bits < 0, bits ^ jnp.int32(0x7FFFFFFF), bits)
        key = jnp.where(start + lane_kc <= q_pos, key, jnp.int32(INT_MIN))
        key_ref[:, pl.ds(start, KC)] = key
        key_t = key.T
        keyt_ref[pl.ds(start, KC), :] = key_t
        hi_ref[pl.ds(start, KC), :] = (key_t >> 16).astype(jnp.int16)
        lo_ref[pl.ds(start, KC), :] = ((key_t & 0xFFFF) - 32768).astype(jnp.int16)
        return carry

    lax.fori_loop(0, n_chunks, score_chunk, 0)

    def count_where(hits):
        def body(c, cnt):
            start = pl.multiple_of(c * KC, KC)
            hit = hits(keyt_ref[pl.ds(start, KC), :], start)
            return cnt + jnp.sum(hit.reshape(KC // 8, 8, tq), axis=0)
        cnt = lax.fori_loop(0, n_chunks, body, jnp.zeros((8, tq), i32))
        return jnp.sum(cnt, axis=0, keepdims=True)

    i16 = jnp.int16
    one16, zero16 = jnp.ones((), i16), jnp.zeros((), i16)

    def count16(ref, cand):
        c16 = cand.astype(i16)
        def body(c, cnt):
            start = pl.multiple_of(c * KC, KC)
            hit = jnp.where(ref[pl.ds(start, KC), :] >= c16, one16, zero16)
            for j in range(KC // 16):
                cnt = cnt + hit[16 * j:16 * (j + 1)]
            return cnt
        cnt = lax.fori_loop(0, n_chunks, body, jnp.zeros((16, tq), i16))
        return jnp.sum(cnt.astype(i32), axis=0, keepdims=True)

    def search16(ref, want):
        def step(it, r):
            cand = r + lax.shift_left(jnp.int32(1), 15 - it)
            return jnp.where(count16(ref, cand) >= want, cand, r)
        return lax.fori_loop(0, 16, step, jnp.full((1, tq), -32768, i32))

    t_hi = search16(hi_ref, k_top)
    n_above = count16(hi_ref, t_hi + 1)
    t_hi16 = t_hi.astype(i16)

    def restrict(c, carry):
        rows = pl.ds(pl.multiple_of(c * KC, KC), KC)
        lo_ref[rows, :] = jnp.where(hi_ref[rows, :] == t_hi16, lo_ref[rows, :], jnp.full((), -32768, i16))
        return carry

    lax.fori_loop(0, n_chunks, restrict, 0)
    t_lo = search16(lo_ref, k_top - n_above)
    thr = t_hi * 65536 + (t_lo + 32768)

    n_ge = count_where(lambda kt, _: jnp.where(kt >= thr, 1, 0))
    has_excess = jnp.max(n_ge) > k_top

    @pl.when(has_excess)
    def _():
        need = k_top - count_where(lambda kt, _: jnp.where(kt > thr, 1, 0))
        sub = lax.broadcasted_iota(i32, (KC, tq), 0)

        def idx_step(it, bound):
            cand = bound + lax.shift_left(jnp.int32(1), 30 - it)
            n_below = count_where(lambda kt, start: jnp.where(kt == thr, jnp.where(start + sub < cand, 1, 0), 0))
            return jnp.where(n_below <= need, cand, bound)

        bound = lax.fori_loop(0, 31, idx_step, jnp.zeros((1, tq), i32))

        def demote(c, carry):
            start = pl.multiple_of(c * KC, KC)
            kt = keyt_ref[pl.ds(start, KC), :]
            drop = jnp.where(kt == thr, jnp.where(start + sub >= bound, 1, 0), 0)
            kt = jnp.where(drop == 1, jnp.int32(INT_MIN), kt)
            keyt_ref[pl.ds(start, KC), :] = kt
            key_ref[:, pl.ds(start, KC)] = kt.T
            return carry

        lax.fori_loop(0, n_chunks, demote, 0)

    thr_sel = jnp.maximum(thr, jnp.int32(INT_MIN + 1))
    thr_kc = _tile_lanes(jnp.broadcast_to(thr_sel, (LANES, tq)).T, KC // LANES)

    m_ref[...] = jnp.full(m_ref.shape, NEG_BIG, f32)
    acc_ref[...] = jnp.zeros(acc_ref.shape, f32)
    v_refs = (v0_ref, v1_ref)

    n_att = (n_chunks * KC + ATT_KC - 1) // ATT_KC

    def fill(c, carry):
        key_ref[:, pl.ds(pl.multiple_of(c * KC, KC), KC)] = jnp.full((tq, KC), INT_MIN, i32)
        return carry

    lax.fori_loop(n_chunks, n_att * (ATT_KC // KC), fill, 0)
    thr_att = _tile_lanes(thr_kc, ATT_KC // KC)

    def attn_chunk(c, carry):
        start = pl.multiple_of(c * ATT_KC, ATT_KC)
        bias = jnp.where(key_ref[:, pl.ds(start, ATT_KC)] >= thr_att, 0.0, 2.0 * NEG_BIG)
        for hd in range(DSA_HEADS):
            g = hd // DSA_GROUP
            s = _dot_nt(q_ref[hd], k_ref[g, pl.ds(start, ATT_KC), :]) + bias
            _softmax_step(s, _with_ones(v_refs[g][pl.ds(start, ATT_KC), :]), m_ref.at[hd], acc_ref.at[hd])
        return carry

    lax.fori_loop(0, n_att, attn_chunk, 0)

    for hd in range(DSA_HEADS):
        o = acc_ref[hd, :, 0:128] / acc_ref[hd, :, 128:256]
        zg = zg_ref[:, 128 * hd:128 * hd + 128].astype(f32)
        o_ref[:, 128 * hd:128 * hd + 128] = (o * _silu(zg)).astype(o_ref.dtype)


def _dsa(iqp, ikd, qd, kd, z, batch, seq, tq, k_top):
    t = z.shape[0]
    nq = seq // tq
    f32 = jnp.float32
    return pl.pallas_call(
        functools.partial(_dsa_kernel, k_top=k_top),
        out_shape=jax.ShapeDtypeStruct((t, DSA_WIDTH), jnp.bfloat16),
        grid=(batch, nq),
        in_specs=[
            pl.BlockSpec((tq, IDX_HEADS * 128), lambda b, i: (b * nq + i, 0)),
            pl.BlockSpec((tq, 128), lambda b, i: (b * nq + i, Z_IW // 128)),
            pl.BlockSpec((seq, 128), lambda b, i: (b, 0)),
            pl.BlockSpec((DSA_HEADS, tq, 128), lambda b, i: (0, b * nq + i, 0)),
            pl.BlockSpec((DSA_KV_HEADS, seq, 128), lambda b, i: (0, b, 0)),
            pl.BlockSpec((seq, 128), lambda b, i: (b, Z_DV // 128)),
            pl.BlockSpec((seq, 128), lambda b, i: (b, Z_DV // 128 + 1)),
            pl.BlockSpec((tq, DSA_WIDTH), lambda b, i: (b * nq + i, Z_DSAZ // DSA_WIDTH)),
        ],
        out_specs=pl.BlockSpec((tq, DSA_WIDTH), lambda b, i: (b * nq + i, 0)),
        scratch_shapes=[
            pltpu.VMEM((tq, seq), jnp.int32),
            pltpu.VMEM((seq, tq), jnp.int32),
            pltpu.VMEM((seq, tq), jnp.int16),
            pltpu.VMEM((seq, tq), jnp.int16),
            pltpu.VMEM((IDX_HEADS, tq, KC), f32),
            pltpu.VMEM((DSA_HEADS, tq, LANES), f32),
            pltpu.VMEM((DSA_HEADS, tq, 2 * DSA_HEAD_DIM), f32),
        ],
        compiler_params=_cparams(("parallel", "arbitrary")),
        name="dsa",
    )(iqp, z, ikd, qd, kd, z, z, z)


def _post_kernel(h_ref, ya_ref, yb_ref, yc_ref, p_ref, g_ref, wa_ref, wb_ref, wc_ref, wg_ref, wp_ref, o_ref):
    h1 = h_ref[...] + _dot(ya_ref[...], wa_ref[...])
    h1 = h1 + _dot(yb_ref[...], wb_ref[...])
    h1 = h1 + _dot(yc_ref[...], wc_ref[...])
    r = lax.rsqrt(jnp.mean(h1 * h1, axis=-1, keepdims=True) + EPS)
    a = (h1 * r * g_ref[...]).astype(jnp.bfloat16)
    gate = _dot(a, wg_ref[...])
    gate = 1.0 / (1.0 + jnp.exp(-gate))
    o_ref[...] = h1 + gate * _dot(p_ref[...].astype(jnp.bfloat16), wp_ref[...])


def _post(h, ya, yb, yc, p, g, wa, wb, wc, wg, wp, tm):
    t, d = h.shape
    rows = lambda a: pl.BlockSpec((tm, a.shape[1]), lambda i: (i, 0))
    whole = lambda a: pl.BlockSpec(a.shape, lambda i: (0, 0), pipeline_mode=pl.Buffered(1))
    return pl.pallas_call(
        _post_kernel,
        out_shape=jax.ShapeDtypeStruct((t, d), jnp.float32),
        grid=(t // tm,),
        in_specs=[rows(h), rows(ya), rows(yb), rows(yc), rows(p), whole(g),
                  whole(wa), whole(wb), whole(wc), whole(wg), whole(wp)],
        out_specs=rows(h),
        compiler_params=_cparams(("parallel",)),
        name="post",
    )(h, ya, yb, yc, p, g, wa, wb, wc, wg, wp)


def _regroup_w_in(w):
    d = w.shape[0]
    w = w.astype(jnp.bfloat16)
    seg = lambda a, b: w[:, a:b]
    parts = [
        seg(_O_CQ, _O_CKV), seg(_O_CKV, _O_KPE),
        seg(_O_CX, _O_CB), seg(_O_CB, _O_CC), seg(_O_CC, _O_CONVZ), seg(_O_CONVZ, _O_DQ),
        seg(_O_MLAZ, _O_CX), seg(_O_DQ, _O_DK), seg(_O_DSAZ, _O_IQ),
        seg(_O_DK, _O_DV), seg(_O_DV, _O_DSAZ),
        seg(_O_KPE, _O_MLAZ), seg(_O_IK, _N_IN), seg(_O_IW, _O_IK),
        jnp.zeros((d, Z_IQ - Z_IW - IDX_HEADS), w.dtype),
        seg(_O_IQ, _O_IW),
    ]
    out = jnp.concatenate(parts, axis=1).astype(jnp.bfloat16)
    assert out.shape[1] == Z_WIDTH
    return out


def _pair_layout(a):
    lead = a.shape[:-1]
    a = a.reshape(lead + (MLA_HEADS, MLA_QK))
    nope, x1, x2 = a[..., :MLA_NOPE], a[..., MLA_NOPE:MLA_NOPE + 32], a[..., MLA_NOPE + 32:]
    return jnp.concatenate([nope, x1, x2, x2, x1], axis=-1).reshape(lead + (MLA_HEADS * 256,))


_B_C32, _B_S32, _B_C16, _B_S16, _B_C8, _B_S8, _B_ONE, _B_ROWS = 0, 32, 64, 80, 96, 104, 112, 128


def _table_selector():
    sel = np.zeros((_B_ROWS, N_TABLES * LANES), np.float32)

    def put(table, lane, base, n, coef=1.0):
        for j in range(n):
            sel[base + (j if base != _B_ONE else 0), table * LANES + lane + j] = coef

    c32, s32, c16, s16, c8, s8, one = _B_C32, _B_S32, _B_C16, _B_S16, _B_C8, _B_S8, _B_ONE
    put(TB_CSQ, 0, c32, 32); put(TB_CSQ, 32, c32, 32); put(TB_CSQ, 64, s32, 32, -1.0); put(TB_CSQ, 96, s32, 32)
    put(TB_C46, 0, c32, 32); put(TB_C46, 32, c32, 32); put(TB_C46, 64, c8, 8); put(TB_C46, 72, c8, 8)
    put(TB_C46, 80, one, 48)
    put(TB_S1A, 0, s32, 32, -1.0)
    put(TB_S2A, 32, s32, 32)
    put(TB_S1B, 64, s8, 8, -1.0)
    put(TB_S2B, 72, s8, 8)
    put(TB_CD, 0, c16, 16); put(TB_CD, 16, c16, 16); put(TB_CD, 32, one, 96)
    put(TB_S1D, 0, s16, 16, -1.0)
    put(TB_S2D, 16, s16, 16)
    qs = IDX_DIM ** -0.5
    for off in (0, 64):
        put(TB_CI, off, c8, 8, qs); put(TB_CI, off + 8, c8, 8, qs); put(TB_CI, off + 16, one, 48, qs)
        put(TB_S1I, off, s8, 8, -qs)
        put(TB_S2I, off + 8, s8, 8, qs)
    return sel


def _rope_basis(positions):
    f32 = jnp.float32
    pos = positions.reshape(1, -1).astype(f32)
    t = pos.shape[1]
    inv = jnp.concatenate([ROPE_THETA ** (-jnp.arange(half, dtype=f32) / half)
                           for half in (MLA_ROPE // 2, DSA_ROT // 2, IDX_ROT // 2)])
    ang = inv[:, None] * pos
    c, s = jnp.cos(ang), jnp.sin(ang)
    basis = jnp.concatenate([c[0:32], s[0:32], c[32:48], s[32:48], c[48:56], s[48:56],
                             jnp.ones((1, t), f32), jnp.zeros((_B_ROWS - _B_ONE - 1, t), f32)], axis=0)
    return basis.T, jnp.asarray(_table_selector(), jnp.bfloat16)


def _layer(h, p_i, tabs, batch, seq, k_top, norm_in, w_in, mla_gq, mla_w_uq, mla_gkv, mla_w_ukv, mla_qn, mla_kn,
           conv_w, dsa_qn, dsa_kn, w_out, ple_norm, ple_w_gate, ple_w_proj):
    f32, bf16 = jnp.float32, jnp.bfloat16
    t = h.shape[0]
    tm_big = min(1024, t)

    z = _inproj(h, norm_in[None, :], _regroup_w_in(w_in), tm_big, 1024)

    gq = _pair_layout(jnp.tile(mla_qn, MLA_HEADS))[None, :] * (MLA_QK ** -0.5 * LOG2E)
    g46 = jnp.concatenate([mla_kn[MLA_NOPE:], jnp.ones((64,), f32)])[None, :]
    qm, km, vm, qd, kd, iqp, ikd = _prep(
        z, *tabs, mla_gq[None, :], mla_gkv[None, :], _pair_layout(mla_w_uq).astype(bf16), mla_w_ukv.astype(bf16),
        gq, mla_kn[None, :MLA_NOPE], g46, dsa_qn[None, :] * (DSA_HEAD_DIM ** -0.5 * LOG2E), dsa_kn[None, :], 256)

    y_b = _conv(z, jnp.pad(conv_w, ((0, 8 - CONV_K), (0, 0))), seq, 512)
    y_a = _mla(qm, km, vm, z, batch, seq, 512)
    y_c = _dsa(iqp, ikd, qd, kd, z, batch, seq, 256, k_top)

    wo = w_out.astype(bf16)
    return _post(h, y_a, y_b, y_c, p_i, ple_norm[None, :],
                 wo[:MLA_WIDTH], wo[MLA_WIDTH:MLA_WIDTH + CONV_CH], wo[MLA_WIDTH + CONV_CH:],
                 ple_w_gate.astype(bf16), ple_w_proj.astype(bf16), min(512, t))


def kernel(x, p, positions, norm_in, w_in, mla_gq, mla_w_uq, mla_gkv, mla_w_ukv, mla_qn, mla_kn, conv_w, dsa_qn,
           dsa_kn, w_out, ple_norm, ple_w_gate, ple_w_proj):
    batch, seq, d = x.shape
    depth = p.shape[0]
    t = batch * seq
    k_top = min(TOPK_MAX, seq // 4)
    tabs = _rope_basis(positions)
    h = x.reshape(t, d)
    for i in range(depth):
        h = _layer(h, p[i].reshape(t, PLE_DIM), tabs, batch, seq, k_top, norm_in[i], w_in[i], mla_gq[i],
                   mla_w_uq[i], mla_gkv[i], mla_w_ukv[i], mla_qn[i], mla_kn[i], conv_w[i], dsa_qn[i], dsa_kn[i],
                   w_out[i], ple_norm[i], ple_w_gate[i], ple_w_proj[i])
    return h.reshape(batch, seq, d)
```

```python
import functools

import numpy as np
import jax
import jax.numpy as jnp
from jax import lax
from jax.experimental import pallas as pl
from jax.experimental.pallas import tpu as pltpu

D_MODEL = 2048
PLE_DIM = 256
ROPE_THETA = 500000.0
EPS = 1e-6

MLA_HEADS = 6
MLA_Q_LORA = 512
MLA_KV_LORA = 512
MLA_NOPE = 128
MLA_ROPE = 64
MLA_V = 128
MLA_QK = MLA_NOPE + MLA_ROPE
MLA_WIDTH = MLA_HEADS * MLA_V

CONV_CH = 512
CONV_K = 3

DSA_HEADS = 6
DSA_KV_HEADS = 2
DSA_GROUP = DSA_HEADS // DSA_KV_HEADS
DSA_HEAD_DIM = 128
DSA_WIDTH = DSA_HEADS * DSA_HEAD_DIM
DSA_ROT = DSA_HEAD_DIM // 4
IDX_HEADS = 16
IDX_DIM = 64
IDX_ROT = IDX_DIM // 4
TOPK_MAX = 256

LANES = 128

_O_CQ, _O_CKV, _O_KPE, _O_MLAZ = 0, 512, 1024, 1088
_O_CX, _O_CB, _O_CC, _O_CONVZ = 1856, 2368, 2880, 3392
_O_DQ, _O_DK, _O_DV, _O_DSAZ = 3904, 4672, 4928, 5184
_O_IQ, _O_IW, _O_IK, _N_IN = 5952, 6976, 6992, 7056

Z_CQ, Z_CKV = 0, 512
Z_CX, Z_CB, Z_CC, Z_CONVZ = 1024, 1536, 2048, 2560
Z_MLAZ, Z_DQ, Z_DSAZ = 3072, 3840, 4608
Z_DK, Z_DV = 5376, 5632
Z_KPE_IK, Z_IW, Z_IQ = 5888, 6016, 6144
Z_WIDTH = 7168

LOG2E = 1.4426950408889634
INT_MIN = -(2 ** 31)
NEG_BIG = -1e30

VMEM_LIMIT = 56 * 1024 * 1024


def _cparams(sem):
    return pltpu.CompilerParams(dimension_semantics=sem, vmem_limit_bytes=VMEM_LIMIT)


def _dot(a, b):
    return jnp.dot(a, b, preferred_element_type=jnp.float32)


def _dot_nt(a, b):
    return lax.dot_general(a, b, (((1,), (1,)), ((), ())), preferred_element_type=jnp.float32)


def _roll(x, shift):
    return pltpu.roll(x, shift, 1)


def _inproj_kernel(x_ref, g_ref, w_ref, o_ref, a_ref):
    @pl.when(pl.program_id(1) == 0)
    def _():
        x = x_ref[...]
        r = lax.rsqrt(jnp.mean(x * x, axis=-1, keepdims=True) + EPS)
        a_ref[...] = (x * r * g_ref[...]).astype(jnp.bfloat16)

    o_ref[...] = _dot(a_ref[...], w_ref[...]).astype(o_ref.dtype)


def _inproj(h, g, w, tm, tn):
    t, d = h.shape
    n = w.shape[1]
    return pl.pallas_call(
        _inproj_kernel,
        out_shape=jax.ShapeDtypeStruct((t, n), jnp.bfloat16),
        grid=(t // tm, n // tn),
        in_specs=[
            pl.BlockSpec((tm, d), lambda i, j: (i, 0)),
            pl.BlockSpec((1, d), lambda i, j: (0, 0)),
            pl.BlockSpec((d, tn), lambda i, j: (0, j)),
        ],
        out_specs=pl.BlockSpec((tm, tn), lambda i, j: (i, j)),
        scratch_shapes=[pltpu.VMEM((tm, d), jnp.bfloat16)],
        compiler_params=_cparams(("parallel", "arbitrary")),
        name="inproj",
    )(h, g, w)


TB_CSQ = 0
TB_C46, TB_S1A, TB_S2A, TB_S1B, TB_S2B = 1, 2, 3, 4, 5
TB_CD, TB_S1D, TB_S2D = 6, 7, 8
TB_CI, TB_S1I, TB_S2I = 9, 10, 11
N_TABLES = 12


def _prep_kernel(cq_ref, ckv_ref, t46_ref, dq_ref, dk_ref, iq_ref, basis_ref, sel_ref,
                 glq_ref, glkv_ref, wuq_ref, wukv_ref, gq_ref, gkn_ref, g46_ref, gdq_ref, gdk_ref,
                 qm_ref, km_ref, vm_ref, qd_ref, kd_ref, iqp_ref, ikd_ref):
    f32, bf16 = jnp.float32, jnp.bfloat16
    tm = cq_ref.shape[0]
    lane = lax.broadcasted_iota(jnp.int32, (tm, LANES), 1)
    low = lane < 64

    b0 = basis_ref[...]
    b_hi = b0.astype(bf16)
    b1 = b0 - b_hi.astype(f32)
    b_mid = b1.astype(bf16)
    b_lo = (b1 - b_mid.astype(f32)).astype(bf16)
    tabs = _dot(b_hi, sel_ref[...]) + _dot(b_mid, sel_ref[...]) + _dot(b_lo, sel_ref[...])
    tab = lambda k: tabs[:, LANES * k:LANES * (k + 1)]

    def rms(x, n):
        return lax.rsqrt(jnp.sum(x * x, axis=-1, keepdims=True) * (1.0 / n) + EPS)

    cq = cq_ref[...].astype(f32)
    aq = (cq * rms(cq, MLA_Q_LORA) * glq_ref[...]).astype(bf16)
    qx = _dot(aq, wuq_ref[...])
    csq = tab(TB_CSQ)
    for h in range(MLA_HEADS):
        nope = qx[:, 256 * h:256 * h + 128]
        pair = qx[:, 256 * h + 128:256 * h + 256]
        ss = jnp.sum(nope * nope, axis=-1, keepdims=True) + 0.5 * jnp.sum(pair * pair, axis=-1, keepdims=True)
        r = lax.rsqrt(ss * (1.0 / MLA_QK) + EPS)
        qm_ref[h, :, 0:128] = (nope * r * gq_ref[:, 256 * h:256 * h + 128]).astype(bf16)
        qm_ref[h, :, 128:256] = (pair * r * gq_ref[:, 256 * h + 128:256 * h + 256] * csq).astype(bf16)

    t46 = t46_ref[...].astype(f32)
    ss_pe = jnp.sum(jnp.where(low, t46 * t46, 0.0), axis=-1, keepdims=True)
    y46 = t46 * g46_ref[...]
    r46 = (y46 * tab(TB_C46)
           + _roll(y46, 96) * tab(TB_S1A) + _roll(y46, 32) * tab(TB_S2A)
           + _roll(y46, 120) * tab(TB_S1B) + _roll(y46, 8) * tab(TB_S2B))
    sw = _roll(r46, 64)
    kpe2 = jnp.where(low, r46, sw)
    ikd_ref[...] = jnp.where(low, sw, r46).astype(bf16)

    ckv = ckv_ref[...].astype(f32)
    akv = (ckv * rms(ckv, MLA_KV_LORA) * glkv_ref[...]).astype(bf16)
    kvx = _dot(akv, wukv_ref[...])
    for h in range(MLA_HEADS):
        kn = kvx[:, 256 * h:256 * h + 128]
        ss = jnp.sum(kn * kn, axis=-1, keepdims=True) + ss_pe
        r = lax.rsqrt(ss * (1.0 / MLA_QK) + EPS)
        km_ref[h, :, 0:128] = (kn * r * gkn_ref[...]).astype(bf16)
        km_ref[h, :, 128:256] = (kpe2 * r).astype(bf16)
        vm_ref[h] = kvx[:, 256 * h + 128:256 * h + 256].astype(bf16)

    cd, s1d, s2d = tab(TB_CD), tab(TB_S1D), tab(TB_S2D)

    def dsa_head(x, g):
        y = x * rms(x, DSA_HEAD_DIM) * g
        return y * cd + _roll(y, 112) * s1d + _roll(y, 16) * s2d

    for h in range(DSA_HEADS):
        x = dq_ref[:, 128 * h:128 * h + 128].astype(f32)
        qd_ref[h] = dsa_head(x, gdq_ref[...]).astype(bf16)
    for g in range(DSA_KV_HEADS):
        x = dk_ref[:, 128 * g:128 * g + 128].astype(f32)
        kd_ref[g] = dsa_head(x, gdk_ref[...]).astype(bf16)

    ci, s1i, s2i = tab(TB_CI), tab(TB_S1I), tab(TB_S2I)
    for j in range(IDX_HEADS // 2):
        x = iq_ref[:, 128 * j:128 * j + 128].astype(f32)
        y = x * ci + _roll(x, 120) * s1i + _roll(x, 8) * s2i
        iqp_ref[:, 256 * j:256 * j + 128] = jnp.where(low, y, 0.0).astype(bf16)
        iqp_ref[:, 256 * j + 128:256 * j + 256] = jnp.where(low, 0.0, y).astype(bf16)


def _prep(z, basis, sel, glq, glkv, wuq, wukv, gq, gkn, g46, gdq, gdk, tm):
    t = z.shape[0]
    bf16 = jnp.bfloat16
    row = lambda w, c: pl.BlockSpec((tm, w), lambda i, c=c: (i, c))
    full = lambda a: pl.BlockSpec(a.shape, lambda i: (0,) * a.ndim)
    heads = lambda n, w: pl.BlockSpec((n, tm, w), lambda i: (0, i, 0))
    return pl.pallas_call(
        _prep_kernel,
        out_shape=(
            jax.ShapeDtypeStruct((MLA_HEADS, t, 256), bf16),
            jax.ShapeDtypeStruct((MLA_HEADS, t, 256), bf16),
            jax.ShapeDtypeStruct((MLA_HEADS, t, 128), bf16),
            jax.ShapeDtypeStruct((DSA_HEADS, t, 128), bf16),
            jax.ShapeDtypeStruct((DSA_KV_HEADS, t, 128), bf16),
            jax.ShapeDtypeStruct((t, IDX_HEADS * 128), bf16),
            jax.ShapeDtypeStruct((t, 128), bf16),
        ),
        grid=(t // tm,),
        in_specs=[
            row(512, Z_CQ // 512), row(512, Z_CKV // 512), row(128, Z_KPE_IK // 128),
            row(768, Z_DQ // 768), row(256, Z_DK // 256), row(1024, Z_IQ // 1024),
            pl.BlockSpec((tm, LANES), lambda i: (i, 0)), full(sel),
            full(glq), full(glkv), full(wuq), full(wukv), full(gq), full(gkn), full(g46), full(gdq), full(gdk),
        ],
        out_specs=(
            heads(MLA_HEADS, 256), heads(MLA_HEADS, 256), heads(MLA_HEADS, 128),
            heads(DSA_HEADS, 128), heads(DSA_KV_HEADS, 128),
            pl.BlockSpec((tm, IDX_HEADS * 128), lambda i: (i, 0)),
            pl.BlockSpec((tm, 128), lambda i: (i, 0)),
        ),
        compiler_params=_cparams(("parallel",)),
        name="prep",
    )(z, z, z, z, z, z, basis, sel, glq, glkv, wuq, wukv, gq, gkn, g46, gdq, gdk)


def _silu(x):
    return x / (1.0 + jnp.exp(-x))


def _conv_kernel(cx_ref, cb_ref, cc_ref, cz_ref, hx_ref, hc_ref, w_ref, o_ref, *, tiles_per_seq):
    f32 = jnp.float32
    tm, ch = cx_ref.shape
    u = cc_ref[...].astype(f32) * cx_ref[...].astype(f32)
    keep = (pl.program_id(0) % tiles_per_seq != 0).astype(f32)
    hu = hc_ref[...].astype(f32) * hx_ref[...].astype(f32) * keep
    row = lax.broadcasted_iota(jnp.int32, (tm, ch), 0)
    u1 = jnp.where(row == 0, hu[7:8, :], pltpu.roll(u, 1, 0))
    u2 = jnp.where(row == 0, hu[6:7, :], jnp.where(row == 1, hu[7:8, :], pltpu.roll(u, 2, 0)))
    conv = w_ref[0:1, :] * u2 + w_ref[1:2, :] * u1 + w_ref[2:3, :] * u
    o_ref[...] = (cb_ref[...].astype(f32) * conv * _silu(cz_ref[...].astype(f32))).astype(o_ref.dtype)


def _conv(z, w, seq, tm):
    t = z.shape[0]
    col = lambda c: pl.BlockSpec((tm, CONV_CH), lambda i, c=c: (i, c))
    halo = lambda c: pl.BlockSpec((8, CONV_CH), lambda i, c=c: (jnp.maximum(i * (tm // 8) - 1, 0), c))
    return pl.pallas_call(
        functools.partial(_conv_kernel, tiles_per_seq=seq // tm),
        out_shape=jax.ShapeDtypeStruct((t, CONV_CH), jnp.bfloat16),
        grid=(t // tm,),
        in_specs=[col(Z_CX // CONV_CH), col(Z_CB // CONV_CH), col(Z_CC // CONV_CH), col(Z_CONVZ // CONV_CH),
                  halo(Z_CX // CONV_CH), halo(Z_CC // CONV_CH),
                  pl.BlockSpec((8, CONV_CH), lambda i: (0, 0))],
        out_specs=pl.BlockSpec((tm, CONV_CH), lambda i: (i, 0)),
        compiler_params=_cparams(("parallel",)),
        name="conv",
    )(z, z, z, z, z, z, w)


def _tile_lanes(x, n):
    return x if n == 1 else jnp.concatenate([x] * n, axis=1)


def _softmax_step(s, v2, m_ref, acc_ref):
    m_prev = m_ref[...]
    m_new = jnp.maximum(m_prev, jnp.max(s, axis=-1, keepdims=True))
    p = jnp.exp2(s - _tile_lanes(m_new, s.shape[1] // LANES))
    alpha = jnp.exp2(m_prev - m_new)
    acc_ref[...] = _tile_lanes(alpha, 2) * acc_ref[...] + _dot(p.astype(jnp.bfloat16), v2)
    m_ref[...] = m_new


def _with_ones(v):
    return jnp.concatenate([v, jnp.ones_like(v)], axis=1)


MLA_HEADS_PER_STEP = 6


def _mla_kernel(q_ref, k_ref, v_ref, zg_ref, o_ref, m_ref, acc_ref):
    f32 = jnp.float32
    tq = q_ref.shape[1]
    qi = pl.program_id(2)
    m_ref[...] = jnp.full(m_ref.shape, NEG_BIG, f32)
    acc_ref[...] = jnp.zeros(acc_ref.shape, f32)

    def step(ki, masked):
        rows = pl.ds(pl.multiple_of(ki * tq, tq), tq)
        for h in range(MLA_HEADS_PER_STEP):
            s = _dot_nt(q_ref[h], k_ref[h, rows, :])
            if masked:
                r = lax.broadcasted_iota(jnp.int32, s.shape, 0)
                c = lax.broadcasted_iota(jnp.int32, s.shape, 1)
                s = jnp.where(c <= r, s, 2.0 * NEG_BIG)
            _softmax_step(s, _with_ones(v_ref[h, rows, :]), m_ref.at[h], acc_ref.at[h])

    def body(ki, c):
        step(ki, False)
        return c

    lax.fori_loop(0, qi, body, 0)
    step(qi, True)
    for h in range(MLA_HEADS_PER_STEP):
        o = acc_ref[h, :, 0:128] / acc_ref[h, :, 128:256]
        zg = zg_ref[:, 128 * h:128 * h + 128].astype(f32)
        o_ref[:, 128 * h:128 * h + 128] = (o * _silu(zg)).astype(o_ref.dtype)


def _mla(qm, km, vm, z, batch, seq, tq):
    t = z.shape[0]
    nq = seq // tq
    hs = MLA_HEADS_PER_STEP
    return pl.pallas_call(
        _mla_kernel,
        out_shape=jax.ShapeDtypeStruct((t, MLA_WIDTH), jnp.bfloat16),
        grid=(batch, MLA_HEADS // hs, nq),
        in_specs=[
            pl.BlockSpec((hs, tq, 256), lambda b, h, i: (h, b * nq + i, 0)),
            pl.BlockSpec((hs, seq, 256), lambda b, h, i: (h, b, 0)),
            pl.BlockSpec((hs, seq, 128), lambda b, h, i: (h, b, 0)),
            pl.BlockSpec((tq, 128 * hs), lambda b, h, i: (b * nq + i, Z_MLAZ // (128 * hs) + h)),
        ],
        out_specs=pl.BlockSpec((tq, 128 * hs), lambda b, h, i: (b * nq + i, h)),
        scratch_shapes=[pltpu.VMEM((hs, tq, LANES), jnp.float32), pltpu.VMEM((hs, tq, 256), jnp.float32)],
        compiler_params=_cparams(("parallel", "parallel", "arbitrary")),
        name="mla",
    )(qm, km, vm, z)


KC = 256
ATT_KC = 512


def _dsa_kernel(iq_ref, iw_ref, ik_ref, q_ref, k_ref, v0_ref, v1_ref, zg_ref, expand_ref, o_ref,
                key_ref, hi_ref, lo_ref, wb_ref, stage_ref, m_ref, acc_ref, *, k_top):
    f32, i32, bf16 = jnp.float32, jnp.int32, jnp.bfloat16
    tq = iq_ref.shape[0]
    i = pl.program_id(1)
    n_chunks = ((i + 1) * tq + KC - 1) // KC
    q_pos = i * tq + lax.broadcasted_iota(i32, (tq, KC), 0)
    lane_kc = lax.broadcasted_iota(i32, (tq, KC), 1)

    w = (iw_ref[...].astype(f32) * (IDX_HEADS ** -0.5)).astype(bf16)
    wb_ref[...] = _dot(w, expand_ref[...])

    def emit_keys(c):
        start = pl.multiple_of(c * KC, KC)
        bits = lax.bitcast_convert_type(stage_ref[...], i32)
        key = jnp.where(bits < 0, bits ^ jnp.int32(0x7FFFFFFF), bits)
        key = jnp.where(start + lane_kc <= q_pos, key, jnp.int32(INT_MIN))
        key_ref[:, pl.ds(start, KC)] = key
        key_t = key.T
        hi_ref[pl.ds(start, KC), :] = (key_t >> 16).astype(jnp.int16)
        lo_ref[pl.ds(start, KC), :] = ((key_t & 0xFFFF) - 32768).astype(jnp.int16)

    def score_chunk(c, carry):
        emit_keys(jnp.maximum(c - 1, 0))
        ikc = ik_ref[pl.ds(pl.multiple_of(c * KC, KC), KC), :]
        acc = jnp.zeros((tq, KC), f32)
        for h in range(IDX_HEADS):
            lg = _dot_nt(iq_ref[:, 128 * h:128 * h + 128], ikc)
            acc = acc + _tile_lanes(wb_ref[:, 128 * h:128 * h + 128], KC // LANES) * jnp.maximum(lg, 0.0)
        stage_ref[...] = acc
        return carry

    stage_ref[...] = jnp.zeros((tq, KC), f32)
    lax.fori_loop(0, n_chunks, score_chunk, 0)
    emit_keys(n_chunks - 1)

    n_att = (n_chunks * KC + ATT_KC - 1) // ATT_KC

    def fill(c, carry):
        start = pl.multiple_of(c * KC, KC)
        key_ref[:, pl.ds(start, KC)] = jnp.full((tq, KC), INT_MIN, i32)
        hi_ref[pl.ds(start, KC), :] = jnp.full((KC, tq), -32768, jnp.int16)
        lo_ref[pl.ds(start, KC), :] = jnp.full((KC, tq), -32768, jnp.int16)
        return carry

    lax.fori_loop(n_chunks, n_att * (ATT_KC // KC), fill, 0)

    i16 = jnp.int16
    low16 = jnp.full((), -32768, i16)
    one16, zero16 = jnp.ones((), i16), jnp.zeros((), i16)

    def count16(ref, cand):
        c16 = cand.astype(i16)
        def body(c, cnt):
            rows = pl.ds(pl.multiple_of(c * ATT_KC, ATT_KC), ATT_KC)
            hit = jnp.where(ref[rows, :] >= c16, one16, zero16)
            for j in range(ATT_KC // 16):
                cnt = cnt + hit[16 * j:16 * (j + 1)]
            return cnt
        cnt = lax.fori_loop(0, n_att, body, jnp.zeros((16, tq), i16))
        return jnp.sum(cnt.astype(i32), axis=0, keepdims=True)

    def search16(ref, want):
        def step(it, carry):
            r, n_r = carry
            cand = r + lax.shift_left(jnp.int32(1), 15 - it)
            n = count16(ref, cand)
            ok = n >= want
            return jnp.where(ok, cand, r), jnp.where(ok, n, n_r)
        return lax.fori_loop(0, 16, step, (jnp.full((1, tq), -32768, i32), jnp.zeros((1, tq), i32)))

    t_hi, n_hi = search16(hi_ref, k_top)
    n_above = count16(hi_ref, t_hi + 1)
    t_hi16 = t_hi.astype(i16)

    def restrict(c, carry):
        rows = pl.ds(pl.multiple_of(c * ATT_KC, ATT_KC), ATT_KC)
        lo_ref[rows, :] = jnp.where(hi_ref[rows, :] == t_hi16, lo_ref[rows, :], low16)
        return carry

    lax.fori_loop(0, n_att, restrict, 0)
    t_lo, n_lo = search16(lo_ref, k_top - n_above)
    thr = t_hi * 65536 + (t_lo + 32768)

    n_ge = jnp.where(t_lo > -32768, n_above + n_lo, jnp.where(t_hi > -32768, n_hi, n_above))
    has_excess = jnp.max(n_ge) > k_top

    @pl.when(has_excess)
    def _():
        def count_where(hits):
            def body(c, cnt):
                start = pl.multiple_of(c * KC, KC)
                hit = hits(key_ref[:, pl.ds(start, KC)].T, start)
                return cnt + jnp.sum(hit.reshape(KC // 8, 8, tq), axis=0)
            cnt = lax.fori_loop(0, n_chunks, body, jnp.zeros((8, tq), i32))
            return jnp.sum(cnt, axis=0, keepdims=True)

        need = k_top - count_where(lambda kt, _: jnp.where(kt > thr, 1, 0))
        sub = lax.broadcasted_iota(i32, (KC, tq), 0)

        def idx_step(it, bound):
            cand = bound + lax.shift_left(jnp.int32(1), 30 - it)
            n_below = count_where(lambda kt, start: jnp.where(kt == thr, jnp.where(start + sub < cand, 1, 0), 0))
            return jnp.where(n_below <= need, cand, bound)

        bound = lax.fori_loop(0, 31, idx_step, jnp.zeros((1, tq), i32))

        def demote(c, carry):
            start = pl.multiple_of(c * KC, KC)
            kt = key_ref[:, pl.ds(start, KC)].T
            drop = jnp.where(kt == thr, jnp.where(start + sub >= bound, 1, 0), 0)
            key_ref[:, pl.ds(start, KC)] = jnp.where(drop == 1, jnp.int32(INT_MIN), kt).T
            return carry

        lax.fori_loop(0, n_chunks, demote, 0)

    thr_sel = jnp.maximum(thr, jnp.int32(INT_MIN + 1))
    thr_kc = _tile_lanes(jnp.broadcast_to(thr_sel, (LANES, tq)).T, KC // LANES)

    m_ref[...] = jnp.full(m_ref.shape, NEG_BIG, f32)
    acc_ref[...] = jnp.zeros(acc_ref.shape, f32)
    v_refs = (v0_ref, v1_ref)

    thr_att = _tile_lanes(thr_kc, ATT_KC // KC)

    def attn_chunk(c, carry):
        start = pl.multiple_of(c * ATT_KC, ATT_KC)
        bias = jnp.where(key_ref[:, pl.ds(start, ATT_KC)] >= thr_att, 0.0, 2.0 * NEG_BIG)
        for hd in range(DSA_HEADS):
            g = hd // DSA_GROUP
            s = _dot_nt(q_ref[hd], k_ref[g, pl.ds(start, ATT_KC), :]) + bias
            _softmax_step(s, _with_ones(v_refs[g][pl.ds(start, ATT_KC), :]), m_ref.at[hd], acc_ref.at[hd])
        return carry

    lax.fori_loop(0, n_att, attn_chunk, 0)

    for hd in range(DSA_HEADS):
        o = acc_ref[hd, :, 0:128] / acc_ref[hd, :, 128:256]
        zg = zg_ref[:, 128 * hd:128 * hd + 128].astype(f32)
        o_ref[:, 128 * hd:128 * hd + 128] = (o * _silu(zg)).astype(o_ref.dtype)


def _dsa(iqp, ikd, qd, kd, z, batch, seq, tq, k_top):
    t = z.shape[0]
    nq = seq // tq
    f32 = jnp.float32
    expand = np.zeros((LANES, IDX_HEADS * LANES), np.float32)
    for h in range(IDX_HEADS):
        expand[h, LANES * h:LANES * (h + 1)] = 1.0
    expand = jnp.asarray(expand, jnp.bfloat16)
    return pl.pallas_call(
        functools.partial(_dsa_kernel, k_top=k_top),
        out_shape=jax.ShapeDtypeStruct((t, DSA_WIDTH), jnp.bfloat16),
        grid=(batch, nq),
        in_specs=[
            pl.BlockSpec((tq, IDX_HEADS * 128), lambda b, i: (b * nq + i, 0)),
            pl.BlockSpec((tq, 128), lambda b, i: (b * nq + i, Z_IW // 128)),
            pl.BlockSpec((seq, 128), lambda b, i: (b, 0)),
            pl.BlockSpec((DSA_HEADS, tq, 128), lambda b, i: (0, b * nq + i, 0)),
            pl.BlockSpec((DSA_KV_HEADS, seq, 128), lambda b, i: (0, b, 0)),
            pl.BlockSpec((seq, 128), lambda b, i: (b, Z_DV // 128)),
            pl.BlockSpec((seq, 128), lambda b, i: (b, Z_DV // 128 + 1)),
            pl.BlockSpec((tq, DSA_WIDTH), lambda b, i: (b * nq + i, Z_DSAZ // DSA_WIDTH)),
            pl.BlockSpec(expand.shape, lambda b, i: (0, 0)),
        ],
        out_specs=pl.BlockSpec((tq, DSA_WIDTH), lambda b, i: (b * nq + i, 0)),
        scratch_shapes=[
            pltpu.VMEM((tq, seq), jnp.int32),
            pltpu.VMEM((seq, tq), jnp.int16),
            pltpu.VMEM((seq, tq), jnp.int16),
            pltpu.VMEM((tq, IDX_HEADS * LANES), f32),
            pltpu.VMEM((tq, KC), f32),
            pltpu.VMEM((DSA_HEADS, tq, LANES), f32),
            pltpu.VMEM((DSA_HEADS, tq, 2 * DSA_HEAD_DIM), f32),
        ],
        compiler_params=_cparams(("parallel", "arbitrary")),
        name="dsa",
    )(iqp, z, ikd, qd, kd, z, z, z, expand)


def _post_kernel(h_ref, ya_ref, yb_ref, yc_ref, p_ref, g_ref, wa_ref, wb_ref, wc_ref, wg_ref, wp_ref, o_ref):
    h1 = h_ref[...] + _dot(ya_ref[...], wa_ref[...])
    h1 = h1 + _dot(yb_ref[...], wb_ref[...])
    h1 = h1 + _dot(yc_ref[...], wc_ref[...])
    r = lax.rsqrt(jnp.mean(h1 * h1, axis=-1, keepdims=True) + EPS)
    a = (h1 * r * g_ref[...]).astype(jnp.bfloat16)
    gate = _dot(a, wg_ref[...])
    gate = 1.0 / (1.0 + jnp.exp(-gate))
    o_ref[...] = h1 + gate * _dot(p_ref[...].astype(jnp.bfloat16), wp_ref[...])


def _post(h, ya, yb, yc, p, g, wa, wb, wc, wg, wp, tm):
    t, d = h.shape
    rows = lambda a: pl.BlockSpec((tm, a.shape[1]), lambda i: (i, 0))
    whole = lambda a: pl.BlockSpec(a.shape, lambda i: (0, 0), pipeline_mode=pl.Buffered(1))
    return pl.pallas_call(
        _post_kernel,
        out_shape=jax.ShapeDtypeStruct((t, d), jnp.float32),
        grid=(t // tm,),
        in_specs=[rows(h), rows(ya), rows(yb), rows(yc), rows(p), whole(g),
                  whole(wa), whole(wb), whole(wc), whole(wg), whole(wp)],
        out_specs=rows(h),
        compiler_params=_cparams(("parallel",)),
        name="post",
    )(h, ya, yb, yc, p, g, wa, wb, wc, wg, wp)


def _regroup_w_in(w):
    d = w.shape[0]
    w = w.astype(jnp.bfloat16)
    seg = lambda a, b: w[:, a:b]
    parts = [
        seg(_O_CQ, _O_CKV), seg(_O_CKV, _O_KPE),
        seg(_O_CX, _O_CB), seg(_O_CB, _O_CC), seg(_O_CC, _O_CONVZ), seg(_O_CONVZ, _O_DQ),
        seg(_O_MLAZ, _O_CX), seg(_O_DQ, _O_DK), seg(_O_DSAZ, _O_IQ),
        seg(_O_DK, _O_DV), seg(_O_DV, _O_DSAZ),
        seg(_O_KPE, _O_MLAZ), seg(_O_IK, _N_IN), seg(_O_IW, _O_IK),
        jnp.zeros((d, Z_IQ - Z_IW - IDX_HEADS), w.dtype),
        seg(_O_IQ, _O_IW),
    ]
    out = jnp.concatenate(parts, axis=1).astype(jnp.bfloat16)
    assert out.shape[1] == Z_WIDTH
    return out


def _pair_layout(a):
    lead = a.shape[:-1]
    a = a.reshape(lead + (MLA_HEADS, MLA_QK))
    nope, x1, x2 = a[..., :MLA_NOPE], a[..., MLA_NOPE:MLA_NOPE + 32], a[..., MLA_NOPE + 32:]
    return jnp.concatenate([nope, x1, x2, x2, x1], axis=-1).reshape(lead + (MLA_HEADS * 256,))


_B_C32, _B_S32, _B_C16, _B_S16, _B_C8, _B_S8, _B_ONE, _B_ROWS = 0, 32, 64, 80, 96, 104, 112, 128


def _table_selector():
    sel = np.zeros((_B_ROWS, N_TABLES * LANES), np.float32)

    def put(table, lane, base, n, coef=1.0):
        for j in range(n):
            sel[base + (j if base != _B_ONE else 0), table * LANES + lane + j] = coef

    c32, s32, c16, s16, c8, s8, one = _B_C32, _B_S32, _B_C16, _B_S16, _B_C8, _B_S8, _B_ONE
    put(TB_CSQ, 0, c32, 32); put(TB_CSQ, 32, c32, 32); put(TB_CSQ, 64, s32, 32, -1.0); put(TB_CSQ, 96, s32, 32)
    put(TB_C46, 0, c32, 32); put(TB_C46, 32, c32, 32); put(TB_C46, 64, c8, 8); put(TB_C46, 72, c8, 8)
    put(TB_C46, 80, one, 48)
    put(TB_S1A, 0, s32, 32, -1.0)
    put(TB_S2A, 32, s32, 32)
    put(TB_S1B, 64, s8, 8, -1.0)
    put(TB_S2B, 72, s8, 8)
    put(TB_CD, 0, c16, 16); put(TB_CD, 16, c16, 16); put(TB_CD, 32, one, 96)
    put(TB_S1D, 0, s16, 16, -1.0)
    put(TB_S2D, 16, s16, 16)
    qs = IDX_DIM ** -0.5
    for off in (0, 64):
        put(TB_CI, off, c8, 8, qs); put(TB_CI, off + 8, c8, 8, qs); put(TB_CI, off + 16, one, 48, qs)
        put(TB_S1I, off, s8, 8, -qs)
        put(TB_S2I, off + 8, s8, 8, qs)
    return sel


def _rope_basis(positions):
    f32 = jnp.float32
    pos = positions.reshape(1, -1).astype(f32)
    t = pos.shape[1]
    inv = jnp.concatenate([ROPE_THETA ** (-jnp.arange(half, dtype=f32) / half)
                           for half in (MLA_ROPE // 2, DSA_ROT // 2, IDX_ROT // 2)])
    ang = inv[:, None] * pos
    c, s = jnp.cos(ang), jnp.sin(ang)
    basis = jnp.concatenate([c[0:32], s[0:32], c[32:48], s[32:48], c[48:56], s[48:56],
                             jnp.ones((1, t), f32), jnp.zeros((_B_ROWS - _B_ONE - 1, t), f32)], axis=0)
    return basis.T, jnp.asarray(_table_selector(), jnp.bfloat16)


def _layer(h, p_i, tabs, batch, seq, k_top, norm_in, w_in, mla_gq, mla_w_uq, mla_gkv, mla_w_ukv, mla_qn, mla_kn,
           conv_w, dsa_qn, dsa_kn, w_out, ple_norm, ple_w_gate, ple_w_proj):
    f32, bf16 = jnp.float32, jnp.bfloat16
    t = h.shape[0]
    tm_big = min(1024, t)

    z = _inproj(h, norm_in[None, :], _regroup_w_in(w_in), tm_big, 1024)

    gq = _pair_layout(jnp.tile(mla_qn, MLA_HEADS))[None, :] * (MLA_QK ** -0.5 * LOG2E)
    g46 = jnp.concatenate([mla_kn[MLA_NOPE:], jnp.ones((64,), f32)])[None, :]
    qm, km, vm, qd, kd, iqp, ikd = _prep(
        z, *tabs, mla_gq[None, :], mla_gkv[None, :], _pair_layout(mla_w_uq).astype(bf16), mla_w_ukv.astype(bf16),
        gq, mla_kn[None, :MLA_NOPE], g46, dsa_qn[None, :] * (DSA_HEAD_DIM ** -0.5 * LOG2E), dsa_kn[None, :], 256)

    y_b = _conv(z, jnp.pad(conv_w, ((0, 8 - CONV_K), (0, 0))), seq, 512)
    y_a = _mla(qm, km, vm, z, batch, seq, 512)
    y_c = _dsa(iqp, ikd, qd, kd, z, batch, seq, 256, k_top)

    wo = w_out.astype(bf16)
    return _post(h, y_a, y_b, y_c, p_i, ple_norm[None, :],
                 wo[:MLA_WIDTH], wo[MLA_WIDTH:MLA_WIDTH + CONV_CH], wo[MLA_WIDTH + CONV_CH:],
                 ple_w_gate.astype(bf16), ple_w_proj.astype(bf16), min(512, t))


def kernel(x, p, positions, norm_in, w_in, mla_gq, mla_w_uq, mla_gkv, mla_w_ukv, mla_qn, mla_kn, conv_w, dsa_qn,
           dsa_kn, w_out, ple_norm, ple_w_gate, ple_w_proj):
    batch, seq, d = x.shape
    depth = p.shape[0]
    t = batch * seq
    k_top = min(TOPK_MAX, seq // 4)
    tabs = _rope_basis(positions)
    h = x.reshape(t, d)
    for i in range(depth):
        h = _layer(h, p[i].reshape(t, PLE_DIM), tabs, batch, seq, k_top, norm_in[i], w_in[i], mla_gq[i],
                   mla_w_uq[i], mla_gkv[i], mla_w_ukv[i], mla_qn[i], mla_kn[i], conv_w[i], dsa_qn[i], dsa_kn[i],
                   w_out[i], ple_norm[i], ple_w_gate[i], ple_w_proj[i])
    return h.reshape(batch, seq, d)
```

```python
import functools

import numpy as np
import jax
import jax.numpy as jnp
from jax import lax
from jax.experimental import pallas as pl
from jax.experimental.pallas import tpu as pltpu

D_MODEL = 2048
PLE_DIM = 256
ROPE_THETA = 500000.0
EPS = 1e-6

MLA_HEADS = 6
MLA_Q_LORA = 512
MLA_KV_LORA = 512
MLA_NOPE = 128
MLA_ROPE = 64
MLA_V = 128
MLA_QK = MLA_NOPE + MLA_ROPE
MLA_WIDTH = MLA_HEADS * MLA_V

CONV_CH = 512
CONV_K = 3

DSA_HEADS = 6
DSA_KV_HEADS = 2
DSA_GROUP = DSA_HEADS // DSA_KV_HEADS
DSA_HEAD_DIM = 128
DSA_WIDTH = DSA_HEADS * DSA_HEAD_DIM
DSA_ROT = DSA_HEAD_DIM // 4
IDX_HEADS = 16
IDX_DIM = 64
IDX_ROT = IDX_DIM // 4
TOPK_MAX = 256

LANES = 128

_O_CQ, _O_CKV, _O_KPE, _O_MLAZ = 0, 512, 1024, 1088
_O_CX, _O_CB, _O_CC, _O_CONVZ = 1856, 2368, 2880, 3392
_O_DQ, _O_DK, _O_DV, _O_DSAZ = 3904, 4672, 4928, 5184
_O_IQ, _O_IW, _O_IK, _N_IN = 5952, 6976, 6992, 7056

Z_CQ, Z_CKV = 0, 512
Z_CX, Z_CB, Z_CC, Z_CONVZ = 1024, 1536, 2048, 2560
Z_MLAZ, Z_DQ, Z_DSAZ = 3072, 3840, 4608
Z_DK, Z_DV = 5376, 5632
Z_KPE_IK, Z_IW, Z_IQ = 5888, 6016, 6144
Z_WIDTH = 7168

LOG2E = 1.4426950408889634
INT_MIN = -(2 ** 31)
NEG_BIG = -1e30

VMEM_LIMIT = 56 * 1024 * 1024


def _cparams(sem):
    return pltpu.CompilerParams(dimension_semantics=sem, vmem_limit_bytes=VMEM_LIMIT)


def _dot(a, b):
    return jnp.dot(a, b, preferred_element_type=jnp.float32)


def _dot_nt(a, b):
    return lax.dot_general(a, b, (((1,), (1,)), ((), ())), preferred_element_type=jnp.float32)


def _roll(x, shift):
    return pltpu.roll(x, shift, 1)


def _inproj_kernel(x_ref, g_ref, w_ref, o_ref, a_ref):
    @pl.when(pl.program_id(1) == 0)
    def _():
        x = x_ref[...]
        r = lax.rsqrt(jnp.mean(x * x, axis=-1, keepdims=True) + EPS)
        a_ref[...] = (x * r * g_ref[...]).astype(jnp.bfloat16)

    o_ref[...] = _dot(a_ref[...], w_ref[...]).astype(o_ref.dtype)


def _inproj(h, g, w, tm, tn):
    t, d = h.shape
    n = w.shape[1]
    return pl.pallas_call(
        _inproj_kernel,
        out_shape=jax.ShapeDtypeStruct((t, n), jnp.bfloat16),
        grid=(t // tm, n // tn),
        in_specs=[
            pl.BlockSpec((tm, d), lambda i, j: (i, 0)),
            pl.BlockSpec((1, d), lambda i, j: (0, 0)),
            pl.BlockSpec((d, tn), lambda i, j: (0, j)),
        ],
        out_specs=pl.BlockSpec((tm, tn), lambda i, j: (i, j)),
        scratch_shapes=[pltpu.VMEM((tm, d), jnp.bfloat16)],
        compiler_params=_cparams(("parallel", "arbitrary")),
        name="inproj",
    )(h, g, w)


TB_CSQ = 0
TB_C46, TB_S1A, TB_S2A, TB_S1B, TB_S2B = 1, 2, 3, 4, 5
TB_CD, TB_S1D, TB_S2D = 6, 7, 8
TB_CI, TB_S1I, TB_S2I = 9, 10, 11
N_TABLES = 12


def _prep_kernel(cq_ref, ckv_ref, t46_ref, dq_ref, dk_ref, iq_ref, basis_ref, sel_ref,
                 glq_ref, glkv_ref, wuq_ref, wukv_ref, gq_ref, gkn_ref, g46_ref, gdq_ref, gdk_ref,
                 qm_ref, km_ref, vm_ref, qd_ref, kd_ref, iqp_ref, ikd_ref):
    f32, bf16 = jnp.float32, jnp.bfloat16
    tm = cq_ref.shape[0]
    lane = lax.broadcasted_iota(jnp.int32, (tm, LANES), 1)
    low = lane < 64

    b0 = basis_ref[...]
    b_hi = b0.astype(bf16)
    b1 = b0 - b_hi.astype(f32)
    b_mid = b1.astype(bf16)
    b_lo = (b1 - b_mid.astype(f32)).astype(bf16)
    tabs = _dot(b_hi, sel_ref[...]) + _dot(b_mid, sel_ref[...]) + _dot(b_lo, sel_ref[...])
    tab = lambda k: tabs[:, LANES * k:LANES * (k + 1)]

    def rms(x, n):
        return lax.rsqrt(jnp.sum(x * x, axis=-1, keepdims=True) * (1.0 / n) + EPS)

    cq = cq_ref[...].astype(f32)
    aq = (cq * rms(cq, MLA_Q_LORA) * glq_ref[...]).astype(bf16)
    qx = _dot(aq, wuq_ref[...])
    csq = tab(TB_CSQ)
    for h in range(MLA_HEADS):
        nope = qx[:, 256 * h:256 * h + 128]
        pair = qx[:, 256 * h + 128:256 * h + 256]
        ss = jnp.sum(nope * nope, axis=-1, keepdims=True) + 0.5 * jnp.sum(pair * pair, axis=-1, keepdims=True)
        r = lax.rsqrt(ss * (1.0 / MLA_QK) + EPS)
        qm_ref[h, :, 0:128] = (nope * r * gq_ref[:, 256 * h:256 * h + 128]).astype(bf16)
        qm_ref[h, :, 128:256] = (pair * r * gq_ref[:, 256 * h + 128:256 * h + 256] * csq).astype(bf16)

    t46 = t46_ref[...].astype(f32)
    ss_pe = jnp.sum(jnp.where(low, t46 * t46, 0.0), axis=-1, keepdims=True)
    y46 = t46 * g46_ref[...]
    r46 = (y46 * tab(TB_C46)
           + _roll(y46, 96) * tab(TB_S1A) + _roll(y46, 32) * tab(TB_S2A)
           + _roll(y46, 120) * tab(TB_S1B) + _roll(y46, 8) * tab(TB_S2B))
    sw = _roll(r46, 64)
    kpe2 = jnp.where(low, r46, sw)
    ikd_ref[...] = jnp.where(low, sw, r46).astype(bf16)

    ckv = ckv_ref[...].astype(f32)
    akv = (ckv * rms(ckv, MLA_KV_LORA) * glkv_ref[...]).astype(bf16)
    kvx = _dot(akv, wukv_ref[...])
    for h in range(MLA_HEADS):
        kn = kvx[:, 256 * h:256 * h + 128]
        ss = jnp.sum(kn * kn, axis=-1, keepdims=True) + ss_pe
        r = lax.rsqrt(ss * (1.0 / MLA_QK) + EPS)
        km_ref[h, :, 0:128] = (kn * r * gkn_ref[...]).astype(bf16)
        km_ref[h, :, 128:256] = (kpe2 * r).astype(bf16)
        vm_ref[h] = kvx[:, 256 * h + 128:256 * h + 256].astype(bf16)

    cd, s1d, s2d = tab(TB_CD), tab(TB_S1D), tab(TB_S2D)

    def dsa_head(x, g):
        y = x * rms(x, DSA_HEAD_DIM) * g
        return y * cd + _roll(y, 112) * s1d + _roll(y, 16) * s2d

    for h in range(DSA_HEADS):
        x = dq_ref[:, 128 * h:128 * h + 128].astype(f32)
        qd_ref[h] = dsa_head(x, gdq_ref[...]).astype(bf16)
    for g in range(DSA_KV_HEADS):
        x = dk_ref[:, 128 * g:128 * g + 128].astype(f32)
        kd_ref[g] = dsa_head(x, gdk_ref[...]).astype(bf16)

    ci, s1i, s2i = tab(TB_CI), tab(TB_S1I), tab(TB_S2I)
    for j in range(IDX_HEADS // 2):
        x = iq_ref[:, 128 * j:128 * j + 128].astype(f32)
        y = x * ci + _roll(x, 120) * s1i + _roll(x, 8) * s2i
        iqp_ref[:, 256 * j:256 * j + 128] = jnp.where(low, y, 0.0).astype(bf16)
        iqp_ref[:, 256 * j + 128:256 * j + 256] = jnp.where(low, 0.0, y).astype(bf16)


def _prep(z, basis, sel, glq, glkv, wuq, wukv, gq, gkn, g46, gdq, gdk, tm):
    t = z.shape[0]
    bf16 = jnp.bfloat16
    row = lambda w, c: pl.BlockSpec((tm, w), lambda i, c=c: (i, c))
    full = lambda a: pl.BlockSpec(a.shape, lambda i: (0,) * a.ndim)
    heads = lambda n, w: pl.BlockSpec((n, tm, w), lambda i: (0, i, 0))
    return pl.pallas_call(
        _prep_kernel,
        out_shape=(
            jax.ShapeDtypeStruct((MLA_HEADS, t, 256), bf16),
            jax.ShapeDtypeStruct((MLA_HEADS, t, 256), bf16),
            jax.ShapeDtypeStruct((MLA_HEADS, t, 128), bf16),
            jax.ShapeDtypeStruct((DSA_HEADS, t, 128), bf16),
            jax.ShapeDtypeStruct((DSA_KV_HEADS, t, 128), bf16),
            jax.ShapeDtypeStruct((t, IDX_HEADS * 128), bf16),
            jax.ShapeDtypeStruct((t, 128), bf16),
        ),
        grid=(t // tm,),
        in_specs=[
            row(512, Z_CQ // 512), row(512, Z_CKV // 512), row(128, Z_KPE_IK // 128),
            row(768, Z_DQ // 768), row(256, Z_DK // 256), row(1024, Z_IQ // 1024),
            pl.BlockSpec((tm, LANES), lambda i: (i, 0)), full(sel),
            full(glq), full(glkv), full(wuq), full(wukv), full(gq), full(gkn), full(g46), full(gdq), full(gdk),
        ],
        out_specs=(
            heads(MLA_HEADS, 256), heads(MLA_HEADS, 256), heads(MLA_HEADS, 128),
            heads(DSA_HEADS, 128), heads(DSA_KV_HEADS, 128),
            pl.BlockSpec((tm, IDX_HEADS * 128), lambda i: (i, 0)),
            pl.BlockSpec((tm, 128), lambda i: (i, 0)),
        ),
        compiler_params=_cparams(("parallel",)),
        name="prep",
    )(z, z, z, z, z, z, basis, sel, glq, glkv, wuq, wukv, gq, gkn, g46, gdq, gdk)


def _silu(x):
    return x / (1.0 + jnp.exp(-x))


def _conv_kernel(cx_ref, cb_ref, cc_ref, cz_ref, hx_ref, hc_ref, w_ref, o_ref, *, tiles_per_seq):
    f32 = jnp.float32
    tm, ch = cx_ref.shape
    u = cc_ref[...].astype(f32) * cx_ref[...].astype(f32)
    keep = (pl.program_id(0) % tiles_per_seq != 0).astype(f32)
    hu = hc_ref[...].astype(f32) * hx_ref[...].astype(f32) * keep
    row = lax.broadcasted_iota(jnp.int32, (tm, ch), 0)
    u1 = jnp.where(row == 0, hu[7:8, :], pltpu.roll(u, 1, 0))
    u2 = jnp.where(row == 0, hu[6:7, :], jnp.where(row == 1, hu[7:8, :], pltpu.roll(u, 2, 0)))
    conv = w_ref[0:1, :] * u2 + w_ref[1:2, :] * u1 + w_ref[2:3, :] * u
    o_ref[...] = (cb_ref[...].astype(f32) * conv * _silu(cz_ref[...].astype(f32))).astype(o_ref.dtype)


def _conv(z, w, seq, tm):
    t = z.shape[0]
    col = lambda c: pl.BlockSpec((tm, CONV_CH), lambda i, c=c: (i, c))
    halo = lambda c: pl.BlockSpec((8, CONV_CH), lambda i, c=c: (jnp.maximum(i * (tm // 8) - 1, 0), c))
    return pl.pallas_call(
        functools.partial(_conv_kernel, tiles_per_seq=seq // tm),
        out_shape=jax.ShapeDtypeStruct((t, CONV_CH), jnp.bfloat16),
        grid=(t // tm,),
        in_specs=[col(Z_CX // CONV_CH), col(Z_CB // CONV_CH), col(Z_CC // CONV_CH), col(Z_CONVZ // CONV_CH),
                  halo(Z_CX // CONV_CH), halo(Z_CC // CONV_CH),
                  pl.BlockSpec((8, CONV_CH), lambda i: (0, 0))],
        out_specs=pl.BlockSpec((tm, CONV_CH), lambda i: (i, 0)),
        compiler_params=_cparams(("parallel",)),
        name="conv",
    )(z, z, z, z, z, z, w)


def _tile_lanes(x, n):
    return x if n == 1 else jnp.concatenate([x] * n, axis=1)


def _softmax_step(s, v2, m_ref, acc_ref):
    m_prev = m_ref[...]
    m_new = jnp.maximum(m_prev, jnp.max(s, axis=-1, keepdims=True))
    p = jnp.exp2(s - _tile_lanes(m_new, s.shape[1] // LANES))
    alpha = jnp.exp2(m_prev - m_new)
    acc_ref[...] = _tile_lanes(alpha, 2) * acc_ref[...] + _dot(p.astype(jnp.bfloat16), v2)
    m_ref[...] = m_new


def _with_ones(v):
    return jnp.concatenate([v, jnp.ones_like(v)], axis=1)


MLA_HEADS_PER_STEP = 6


def _mla_kernel(q_ref, k_ref, v_ref, zg_ref, o_ref, m_ref, acc_ref, *, tk):
    f32 = jnp.float32
    qi = pl.program_id(2)
    m_ref[...] = jnp.full(m_ref.shape, NEG_BIG, f32)
    acc_ref[...] = jnp.zeros(acc_ref.shape, f32)

    def step(ki, row0, masked):
        keys = pl.ds(pl.multiple_of(ki * tk, tk), tk)
        rows = pl.ds(row0, q_ref.shape[1] - row0)
        for h in range(MLA_HEADS_PER_STEP):
            s = _dot_nt(q_ref[h, rows, :], k_ref[h, keys, :])
            if masked:
                r = lax.broadcasted_iota(jnp.int32, s.shape, 0)
                c = lax.broadcasted_iota(jnp.int32, s.shape, 1)
                s = jnp.where(c <= r, s, 2.0 * NEG_BIG)
            _softmax_step(s, _with_ones(v_ref[h, keys, :]), m_ref.at[h, rows, :], acc_ref.at[h, rows, :])

    def body(ki, c):
        step(ki, 0, False)
        return c

    lax.fori_loop(0, 2 * qi, body, 0)
    step(2 * qi, 0, True)
    step(2 * qi + 1, tk, True)
    for h in range(MLA_HEADS_PER_STEP):
        o = acc_ref[h, :, 0:128] / acc_ref[h, :, 128:256]
        zg = zg_ref[:, 128 * h:128 * h + 128].astype(f32)
        o_ref[:, 128 * h:128 * h + 128] = (o * _silu(zg)).astype(o_ref.dtype)


def _mla(qm, km, vm, z, batch, seq, tk):
    t = z.shape[0]
    tq = 2 * tk
    nq = seq // tq
    hs = MLA_HEADS_PER_STEP
    resident = dict(pipeline_mode=pl.Buffered(1))
    return pl.pallas_call(
        functools.partial(_mla_kernel, tk=tk),
        out_shape=jax.ShapeDtypeStruct((t, MLA_WIDTH), jnp.bfloat16),
        grid=(batch, MLA_HEADS // hs, nq),
        in_specs=[
            pl.BlockSpec((hs, tq, 256), lambda b, h, i: (h, b * nq + i, 0)),
            pl.BlockSpec((hs, seq, 256), lambda b, h, i: (h, b, 0), **resident),
            pl.BlockSpec((hs, seq, 128), lambda b, h, i: (h, b, 0), **resident),
            pl.BlockSpec((tq, 128 * hs), lambda b, h, i: (b * nq + i, Z_MLAZ // (128 * hs) + h)),
        ],
        out_specs=pl.BlockSpec((tq, 128 * hs), lambda b, h, i: (b * nq + i, h)),
        scratch_shapes=[pltpu.VMEM((hs, tq, LANES), jnp.float32), pltpu.VMEM((hs, tq, 256), jnp.float32)],
        compiler_params=_cparams(("parallel", "parallel", "arbitrary")),
        name="mla",
    )(qm, km, vm, z)


KC = 256
ATT_KC = 512


def _dsa_kernel(iq_ref, iw_ref, ik_ref, q_ref, k_ref, v0_ref, v1_ref, zg_ref, expand_ref, o_ref,
                key_ref, hi_ref, lo_ref, wb_ref, stage_ref, m_ref, acc_ref, *, k_top):
    f32, i32, bf16 = jnp.float32, jnp.int32, jnp.bfloat16
    tq = iq_ref.shape[0]
    i = pl.program_id(1)
    n_chunks = ((i + 1) * tq + KC - 1) // KC
    q_pos = i * tq + lax.broadcasted_iota(i32, (tq, KC), 0)
    lane_kc = lax.broadcasted_iota(i32, (tq, KC), 1)

    w = (iw_ref[...].astype(f32) * (IDX_HEADS ** -0.5)).astype(bf16)
    wb_ref[...] = _dot(w, expand_ref[...])

    def emit_keys(c):
        start = pl.multiple_of(c * KC, KC)
        bits = lax.bitcast_convert_type(stage_ref[...], i32)
        key = jnp.where(bits < 0, bits ^ jnp.int32(0x7FFFFFFF), bits)
        key = jnp.where(start + lane_kc <= q_pos, key, jnp.int32(INT_MIN))
        key_ref[:, pl.ds(start, KC)] = key
        key_t = key.T
        hi_ref[pl.ds(start, KC), :] = (key_t >> 16).astype(jnp.int16)
        lo_ref[pl.ds(start, KC), :] = ((key_t & 0xFFFF) - 32768).astype(jnp.int16)

    def score_chunk(c, carry):
        emit_keys(jnp.maximum(c - 1, 0))
        ikc = ik_ref[pl.ds(pl.multiple_of(c * KC, KC), KC), :]
        acc = jnp.zeros((tq, KC), f32)
        for h in range(IDX_HEADS):
            lg = _dot_nt(iq_ref[:, 128 * h:128 * h + 128], ikc)
            acc = acc + _tile_lanes(wb_ref[:, 128 * h:128 * h + 128], KC // LANES) * jnp.maximum(lg, 0.0)
        stage_ref[...] = acc
        return carry

    stage_ref[...] = jnp.zeros((tq, KC), f32)
    lax.fori_loop(0, n_chunks, score_chunk, 0)
    emit_keys(n_chunks - 1)

    n_att = (n_chunks * KC + ATT_KC - 1) // ATT_KC

    def fill(c, carry):
        start = pl.multiple_of(c * KC, KC)
        key_ref[:, pl.ds(start, KC)] = jnp.full((tq, KC), INT_MIN, i32)
        hi_ref[pl.ds(start, KC), :] = jnp.full((KC, tq), -32768, jnp.int16)
        lo_ref[pl.ds(start, KC), :] = jnp.full((KC, tq), -32768, jnp.int16)
        return carry

    lax.fori_loop(n_chunks, n_att * (ATT_KC // KC), fill, 0)

    i16 = jnp.int16
    low16 = jnp.full((), -32768, i16)
    one16, zero16 = jnp.ones((), i16), jnp.zeros((), i16)

    def count16(ref, cand):
        c16 = cand.astype(i16)
        def body(c, cnt):
            rows = pl.ds(pl.multiple_of(c * ATT_KC, ATT_KC), ATT_KC)
            hit = jnp.where(ref[rows, :] >= c16, one16, zero16)
            for j in range(ATT_KC // 16):
                cnt = cnt + hit[16 * j:16 * (j + 1)]
            return cnt
        cnt = lax.fori_loop(0, n_att, body, jnp.zeros((16, tq), i16))
        return jnp.sum(cnt.astype(i32), axis=0, keepdims=True)

    def search16(ref, want):
        def step(it, carry):
            r, n_r = carry
            cand = r + lax.shift_left(jnp.int32(1), 15 - it)
            n = count16(ref, cand)
            ok = n >= want
            return jnp.where(ok, cand, r), jnp.where(ok, n, n_r)
        return lax.fori_loop(0, 16, step, (jnp.full((1, tq), -32768, i32), jnp.zeros((1, tq), i32)))

    t_hi, n_hi = search16(hi_ref, k_top)
    n_above = count16(hi_ref, t_hi + 1)
    t_hi16 = t_hi.astype(i16)

    def restrict(c, carry):
        rows = pl.ds(pl.multiple_of(c * ATT_KC, ATT_KC), ATT_KC)
        lo_ref[rows, :] = jnp.where(hi_ref[rows, :] == t_hi16, lo_ref[rows, :], low16)
        return carry

    lax.fori_loop(0, n_att, restrict, 0)
    t_lo, n_lo = search16(lo_ref, k_top - n_above)
    thr = t_hi * 65536 + (t_lo + 32768)

    n_ge = jnp.where(t_lo > -32768, n_above + n_lo, jnp.where(t_hi > -32768, n_hi, n_above))
    has_excess = jnp.max(n_ge) > k_top

    @pl.when(has_excess)
    def _():
        def count_where(hits):
            def body(c, cnt):
                start = pl.multiple_of(c * KC, KC)
                hit = hits(key_ref[:, pl.ds(start, KC)].T, start)
                return cnt + jnp.sum(hit.reshape(KC // 8, 8, tq), axis=0)
            cnt = lax.fori_loop(0, n_chunks, body, jnp.zeros((8, tq), i32))
            return jnp.sum(cnt, axis=0, keepdims=True)

        need = k_top - count_where(lambda kt, _: jnp.where(kt > thr, 1, 0))
        sub = lax.broadcasted_iota(i32, (KC, tq), 0)

        def idx_step(it, bound):
            cand = bound + lax.shift_left(jnp.int32(1), 30 - it)
            n_below = count_where(lambda kt, start: jnp.where(kt == thr, jnp.where(start + sub < cand, 1, 0), 0))
            return jnp.where(n_below <= need, cand, bound)

        bound = lax.fori_loop(0, 31, idx_step, jnp.zeros((1, tq), i32))

        def demote(c, carry):
            start = pl.multiple_of(c * KC, KC)
            kt = key_ref[:, pl.ds(start, KC)].T
            drop = jnp.where(kt == thr, jnp.where(start + sub >= bound, 1, 0), 0)
            key_ref[:, pl.ds(start, KC)] = jnp.where(drop == 1, jnp.int32(INT_MIN), kt).T
            return carry

        lax.fori_loop(0, n_chunks, demote, 0)

    thr_sel = jnp.maximum(thr, jnp.int32(INT_MIN + 1))
    thr_kc = _tile_lanes(jnp.broadcast_to(thr_sel, (LANES, tq)).T, KC // LANES)

    m_ref[...] = jnp.full(m_ref.shape, NEG_BIG, f32)
    acc_ref[...] = jnp.zeros(acc_ref.shape, f32)
    v_refs = (v0_ref, v1_ref)

    thr_att = _tile_lanes(thr_kc, ATT_KC // KC)

    def attn_chunk(c, carry):
        start = pl.multiple_of(c * ATT_KC, ATT_KC)
        bias = jnp.where(key_ref[:, pl.ds(start, ATT_KC)] >= thr_att, 0.0, 2.0 * NEG_BIG)
        for hd in range(DSA_HEADS):
            g = hd // DSA_GROUP
            s = _dot_nt(q_ref[hd], k_ref[g, pl.ds(start, ATT_KC), :]) + bias
            _softmax_step(s, _with_ones(v_refs[g][pl.ds(start, ATT_KC), :]), m_ref.at[hd], acc_ref.at[hd])
        return carry

    lax.fori_loop(0, n_att, attn_chunk, 0)

    for hd in range(DSA_HEADS):
        o = acc_ref[hd, :, 0:128] / acc_ref[hd, :, 128:256]
        zg = zg_ref[:, 128 * hd:128 * hd + 128].astype(f32)
        o_ref[:, 128 * hd:128 * hd + 128] = (o * _silu(zg)).astype(o_ref.dtype)


def _dsa(iqp, ikd, qd, kd, z, batch, seq, tq, k_top):
    t = z.shape[0]
    nq = seq // tq
    f32 = jnp.float32
    expand = np.zeros((LANES, IDX_HEADS * LANES), np.float32)
    for h in range(IDX_HEADS):
        expand[h, LANES * h:LANES * (h + 1)] = 1.0
    expand = jnp.asarray(expand, jnp.bfloat16)
    return pl.pallas_call(
        functools.partial(_dsa_kernel, k_top=k_top),
        out_shape=jax.ShapeDtypeStruct((t, DSA_WIDTH), jnp.bfloat16),
        grid=(batch, nq),
        in_specs=[
            pl.BlockSpec((tq, IDX_HEADS * 128), lambda b, i: (b * nq + i, 0)),
            pl.BlockSpec((tq, 128), lambda b, i: (b * nq + i, Z_IW // 128)),
            pl.BlockSpec((seq, 128), lambda b, i: (b, 0)),
            pl.BlockSpec((DSA_HEADS, tq, 128), lambda b, i: (0, b * nq + i, 0)),
            pl.BlockSpec((DSA_KV_HEADS, seq, 128), lambda b, i: (0, b, 0)),
            pl.BlockSpec((seq, 128), lambda b, i: (b, Z_DV // 128)),
            pl.BlockSpec((seq, 128), lambda b, i: (b, Z_DV // 128 + 1)),
            pl.BlockSpec((tq, DSA_WIDTH), lambda b, i: (b * nq + i, Z_DSAZ // DSA_WIDTH)),
            pl.BlockSpec(expand.shape, lambda b, i: (0, 0)),
        ],
        out_specs=pl.BlockSpec((tq, DSA_WIDTH), lambda b, i: (b * nq + i, 0)),
        scratch_shapes=[
            pltpu.VMEM((tq, seq), jnp.int32),
            pltpu.VMEM((seq, tq), jnp.int16),
            pltpu.VMEM((seq, tq), jnp.int16),
            pltpu.VMEM((tq, IDX_HEADS * LANES), f32),
            pltpu.VMEM((tq, KC), f32),
            pltpu.VMEM((DSA_HEADS, tq, LANES), f32),
            pltpu.VMEM((DSA_HEADS, tq, 2 * DSA_HEAD_DIM), f32),
        ],
        compiler_params=_cparams(("parallel", "arbitrary")),
        name="dsa",
    )(iqp, z, ikd, qd, kd, z, z, z, expand)


def _post_kernel(h_ref, ya_ref, yb_ref, yc_ref, p_ref, g_ref, wa_ref, wb_ref, wc_ref, wg_ref, wp_ref, o_ref):
    h1 = h_ref[...] + _dot(ya_ref[...], wa_ref[...])
    h1 = h1 + _dot(yb_ref[...], wb_ref[...])
    h1 = h1 + _dot(yc_ref[...], wc_ref[...])
    r = lax.rsqrt(jnp.mean(h1 * h1, axis=-1, keepdims=True) + EPS)
    a = (h1 * r * g_ref[...]).astype(jnp.bfloat16)
    gate = _dot(a, wg_ref[...])
    gate = 1.0 / (1.0 + jnp.exp(-gate))
    o_ref[...] = h1 + gate * _dot(p_ref[...].astype(jnp.bfloat16), wp_ref[...])


def _post(h, ya, yb, yc, p, g, wa, wb, wc, wg, wp, tm):
    t, d = h.shape
    rows = lambda a: pl.BlockSpec((tm, a.shape[1]), lambda i: (i, 0))
    whole = lambda a: pl.BlockSpec(a.shape, lambda i: (0, 0), pipeline_mode=pl.Buffered(1))
    return pl.pallas_call(
        _post_kernel,
        out_shape=jax.ShapeDtypeStruct((t, d), jnp.float32),
        grid=(t // tm,),
        in_specs=[rows(h), rows(ya), rows(yb), rows(yc), rows(p), whole(g),
                  whole(wa), whole(wb), whole(wc), whole(wg), whole(wp)],
        out_specs=rows(h),
        compiler_params=_cparams(("parallel",)),
        name="post",
    )(h, ya, yb, yc, p, g, wa, wb, wc, wg, wp)


def _regroup_w_in(w):
    d = w.shape[0]
    w = w.astype(jnp.bfloat16)
    seg = lambda a, b: w[:, a:b]
    parts = [
        seg(_O_CQ, _O_CKV), seg(_O_CKV, _O_KPE),
        seg(_O_CX, _O_CB), seg(_O_CB, _O_CC), seg(_O_CC, _O_CONVZ), seg(_O_CONVZ, _O_DQ),
        seg(_O_MLAZ, _O_CX), seg(_O_DQ, _O_DK), seg(_O_DSAZ, _O_IQ),
        seg(_O_DK, _O_DV), seg(_O_DV, _O_DSAZ),
        seg(_O_KPE, _O_MLAZ), seg(_O_IK, _N_IN), seg(_O_IW, _O_IK),
        jnp.zeros((d, Z_IQ - Z_IW - IDX_HEADS), w.dtype),
        seg(_O_IQ, _O_IW),
    ]
    out = jnp.concatenate(parts, axis=1).astype(jnp.bfloat16)
    assert out.shape[1] == Z_WIDTH
    return out


def _pair_layout(a):
    lead = a.shape[:-1]
    a = a.reshape(lead + (MLA_HEADS, MLA_QK))
    nope, x1, x2 = a[..., :MLA_NOPE], a[..., MLA_NOPE:MLA_NOPE + 32], a[..., MLA_NOPE + 32:]
    return jnp.concatenate([nope, x1, x2, x2, x1], axis=-1).reshape(lead + (MLA_HEADS * 256,))


_B_C32, _B_S32, _B_C16, _B_S16, _B_C8, _B_S8, _B_ONE, _B_ROWS = 0, 32, 64, 80, 96, 104, 112, 128


def _table_selector():
    sel = np.zeros((_B_ROWS, N_TABLES * LANES), np.float32)

    def put(table, lane, base, n, coef=1.0):
        for j in range(n):
            sel[base + (j if base != _B_ONE else 0), table * LANES + lane + j] = coef

    c32, s32, c16, s16, c8, s8, one = _B_C32, _B_S32, _B_C16, _B_S16, _B_C8, _B_S8, _B_ONE
    put(TB_CSQ, 0, c32, 32); put(TB_CSQ, 32, c32, 32); put(TB_CSQ, 64, s32, 32, -1.0); put(TB_CSQ, 96, s32, 32)
    put(TB_C46, 0, c32, 32); put(TB_C46, 32, c32, 32); put(TB_C46, 64, c8, 8); put(TB_C46, 72, c8, 8)
    put(TB_C46, 80, one, 48)
    put(TB_S1A, 0, s32, 32, -1.0)
    put(TB_S2A, 32, s32, 32)
    put(TB_S1B, 64, s8, 8, -1.0)
    put(TB_S2B, 72, s8, 8)
    put(TB_CD, 0, c16, 16); put(TB_CD, 16, c16, 16); put(TB_CD, 32, one, 96)
    put(TB_S1D, 0, s16, 16, -1.0)
    put(TB_S2D, 16, s16, 16)
    qs = IDX_DIM ** -0.5
    for off in (0, 64):
        put(TB_CI, off, c8, 8, qs); put(TB_CI, off + 8, c8, 8, qs); put(TB_CI, off + 16, one, 48, qs)
        put(TB_S1I, off, s8, 8, -qs)
        put(TB_S2I, off + 8, s8, 8, qs)
    return sel


def _rope_basis(positions):
    f32 = jnp.float32
    pos = positions.reshape(1, -1).astype(f32)
    t = pos.shape[1]
    inv = jnp.concatenate([ROPE_THETA ** (-jnp.arange(half, dtype=f32) / half)
                           for half in (MLA_ROPE // 2, DSA_ROT // 2, IDX_ROT // 2)])
    ang = inv[:, None] * pos
    c, s = jnp.cos(ang), jnp.sin(ang)
    basis = jnp.concatenate([c[0:32], s[0:32], c[32:48], s[32:48], c[48:56], s[48:56],
                             jnp.ones((1, t), f32), jnp.zeros((_B_ROWS - _B_ONE - 1, t), f32)], axis=0)
    return basis.T, jnp.asarray(_table_selector(), jnp.bfloat16)


def _layer(h, p_i, tabs, batch, seq, k_top, norm_in, w_in, mla_gq, mla_w_uq, mla_gkv, mla_w_ukv, mla_qn, mla_kn,
           conv_w, dsa_qn, dsa_kn, w_out, ple_norm, ple_w_gate, ple_w_proj):
    f32, bf16 = jnp.float32, jnp.bfloat16
    t = h.shape[0]
    tm_big = min(1024, t)

    z = _inproj(h, norm_in[None, :], _regroup_w_in(w_in), tm_big, Z_WIDTH // 4)

    gq = _pair_layout(jnp.tile(mla_qn, MLA_HEADS))[None, :] * (MLA_QK ** -0.5 * LOG2E)
    g46 = jnp.concatenate([mla_kn[MLA_NOPE:], jnp.ones((64,), f32)])[None, :]
    qm, km, vm, qd, kd, iqp, ikd = _prep(
        z, *tabs, mla_gq[None, :], mla_gkv[None, :], _pair_layout(mla_w_uq).astype(bf16), mla_w_ukv.astype(bf16),
        gq, mla_kn[None, :MLA_NOPE], g46, dsa_qn[None, :] * (DSA_HEAD_DIM ** -0.5 * LOG2E), dsa_kn[None, :], 256)

    y_b = _conv(z, jnp.pad(conv_w, ((0, 8 - CONV_K), (0, 0))), seq, 512)
    y_a = _mla(qm, km, vm, z, batch, seq, 512)
    y_c = _dsa(iqp, ikd, qd, kd, z, batch, seq, 256, k_top)

    wo = w_out.astype(bf16)
    return _post(h, y_a, y_b, y_c, p_i, ple_norm[None, :],
                 wo[:MLA_WIDTH], wo[MLA_WIDTH:MLA_WIDTH + CONV_CH], wo[MLA_WIDTH + CONV_CH:],
                 ple_w_gate.astype(bf16), ple_w_proj.astype(bf16), min(512, t))


def kernel(x, p, positions, norm_in, w_in, mla_gq, mla_w_uq, mla_gkv, mla_w_ukv, mla_qn, mla_kn, conv_w, dsa_qn,
           dsa_kn, w_out, ple_norm, ple_w_gate, ple_w_proj):
    batch, seq, d = x.shape
    depth = p.shape[0]
    t = batch * seq
    k_top = min(TOPK_MAX, seq // 4)
    tabs = _rope_basis(positions)
    h = x.reshape(t, d)
    for i in range(depth):
        h = _layer(h, p[i].reshape(t, PLE_DIM), tabs, batch, seq, k_top, norm_in[i], w_in[i], mla_gq[i],
                   mla_w_uq[i], mla_gkv[i], mla_w_ukv[i], mla_qn[i], mla_kn[i], conv_w[i], dsa_qn[i], dsa_kn[i],
                   w_out[i], ple_norm[i], ple_w_gate[i], ple_w_proj[i])
    return h.reshape(batch, seq, d)
```

```python
import functools

import numpy as np
import jax
import jax.numpy as jnp
from jax import lax
from jax.experimental import pallas as pl
from jax.experimental.pallas import tpu as pltpu

D_MODEL = 2048
PLE_DIM = 256
ROPE_THETA = 500000.0
EPS = 1e-6

MLA_HEADS = 6
MLA_Q_LORA = 512
MLA_KV_LORA = 512
MLA_NOPE = 128
MLA_ROPE = 64
MLA_V = 128
MLA_QK = MLA_NOPE + MLA_ROPE
MLA_WIDTH = MLA_HEADS * MLA_V

CONV_CH = 512
CONV_K = 3

DSA_HEADS = 6
DSA_KV_HEADS = 2
DSA_GROUP = DSA_HEADS // DSA_KV_HEADS
DSA_HEAD_DIM = 128
DSA_WIDTH = DSA_HEADS * DSA_HEAD_DIM
DSA_ROT = DSA_HEAD_DIM // 4
IDX_HEADS = 16
IDX_DIM = 64
IDX_ROT = IDX_DIM // 4
TOPK_MAX = 256

LANES = 128

_O_CQ, _O_CKV, _O_KPE, _O_MLAZ = 0, 512, 1024, 1088
_O_CX, _O_CB, _O_CC, _O_CONVZ = 1856, 2368, 2880, 3392
_O_DQ, _O_DK, _O_DV, _O_DSAZ = 3904, 4672, 4928, 5184
_O_IQ, _O_IW, _O_IK, _N_IN = 5952, 6976, 6992, 7056

Z_CQ, Z_CKV = 0, 512
Z_CX, Z_CB, Z_CC, Z_CONVZ = 1024, 1536, 2048, 2560
Z_MLAZ, Z_DQ, Z_DSAZ = 3072, 3840, 4608
Z_DK, Z_DV = 5376, 5632
Z_KPE_IK, Z_IW, Z_IQ = 5888, 6016, 6144
Z_WIDTH = 7168

LOG2E = 1.4426950408889634
INT_MIN = -(2 ** 31)
NEG_BIG = -1e30

VMEM_LIMIT = 56 * 1024 * 1024


def _cparams(sem):
    return pltpu.CompilerParams(dimension_semantics=sem, vmem_limit_bytes=VMEM_LIMIT)


def _dot(a, b):
    return jnp.dot(a, b, preferred_element_type=jnp.float32)


def _dot_nt(a, b):
    return lax.dot_general(a, b, (((1,), (1,)), ((), ())), preferred_element_type=jnp.float32)


def _roll(x, shift):
    return pltpu.roll(x, shift, 1)


def _inproj_kernel(x_ref, g_ref, w_ref, o_ref, a_ref):
    @pl.when(pl.program_id(1) == 0)
    def _():
        x = x_ref[...]
        r = lax.rsqrt(jnp.mean(x * x, axis=-1, keepdims=True) + EPS)
        a_ref[...] = (x * r * g_ref[...]).astype(jnp.bfloat16)

    o_ref[...] = _dot(a_ref[...], w_ref[...]).astype(o_ref.dtype)


def _inproj(h, g, w, tm, tn):
    t, d = h.shape
    n = w.shape[1]
    return pl.pallas_call(
        _inproj_kernel,
        out_shape=jax.ShapeDtypeStruct((t, n), jnp.bfloat16),
        grid=(t // tm, n // tn),
        in_specs=[
            pl.BlockSpec((tm, d), lambda i, j: (i, 0)),
            pl.BlockSpec((1, d), lambda i, j: (0, 0)),
            pl.BlockSpec((d, tn), lambda i, j: (0, j)),
        ],
        out_specs=pl.BlockSpec((tm, tn), lambda i, j: (i, j)),
        scratch_shapes=[pltpu.VMEM((tm, d), jnp.bfloat16)],
        compiler_params=_cparams(("parallel", "arbitrary")),
        name="inproj",
    )(h, g, w)


TB_CSQ = 0
TB_C46, TB_S1A, TB_S2A, TB_S1B, TB_S2B = 1, 2, 3, 4, 5
TB_CD, TB_S1D, TB_S2D = 6, 7, 8
TB_CI, TB_S1I, TB_S2I = 9, 10, 11
N_TABLES = 12


def _prep_kernel(cq_ref, ckv_ref, t46_ref, dq_ref, dk_ref, iq_ref, basis_ref, sel_ref,
                 glq_ref, glkv_ref, wuq_ref, wukv_ref, gq_ref, gkn_ref, g46_ref, gdq_ref, gdk_ref,
                 qm_ref, km_ref, vm_ref, qd_ref, kd_ref, iqp_ref, ikd_ref):
    f32, bf16 = jnp.float32, jnp.bfloat16
    tm = cq_ref.shape[0]
    lane = lax.broadcasted_iota(jnp.int32, (tm, LANES), 1)
    low = lane < 64

    b0 = basis_ref[...]
    b_hi = b0.astype(bf16)
    b1 = b0 - b_hi.astype(f32)
    b_mid = b1.astype(bf16)
    b_lo = (b1 - b_mid.astype(f32)).astype(bf16)
    tabs = _dot(b_hi, sel_ref[...]) + _dot(b_mid, sel_ref[...]) + _dot(b_lo, sel_ref[...])
    tab = lambda k: tabs[:, LANES * k:LANES * (k + 1)]

    def rms(x, n):
        return lax.rsqrt(jnp.sum(x * x, axis=-1, keepdims=True) * (1.0 / n) + EPS)

    cq = cq_ref[...].astype(f32)
    aq = (cq * rms(cq, MLA_Q_LORA) * glq_ref[...]).astype(bf16)
    qx = _dot(aq, wuq_ref[...])
    csq = tab(TB_CSQ)
    for h in range(MLA_HEADS):
        nope = qx[:, 256 * h:256 * h + 128]
        pair = qx[:, 256 * h + 128:256 * h + 256]
        ss = jnp.sum(nope * nope, axis=-1, keepdims=True) + 0.5 * jnp.sum(pair * pair, axis=-1, keepdims=True)
        r = lax.rsqrt(ss * (1.0 / MLA_QK) + EPS)
        qm_ref[h, :, 0:128] = (nope * r * gq_ref[:, 256 * h:256 * h + 128]).astype(bf16)
        qm_ref[h, :, 128:256] = (pair * r * gq_ref[:, 256 * h + 128:256 * h + 256] * csq).astype(bf16)

    t46 = t46_ref[...].astype(f32)
    ss_pe = jnp.sum(jnp.where(low, t46 * t46, 0.0), axis=-1, keepdims=True)
    y46 = t46 * g46_ref[...]
    r46 = (y46 * tab(TB_C46)
           + _roll(y46, 96) * tab(TB_S1A) + _roll(y46, 32) * tab(TB_S2A)
           + _roll(y46, 120) * tab(TB_S1B) + _roll(y46, 8) * tab(TB_S2B))
    sw = _roll(r46, 64)
    kpe2 = jnp.where(low, r46, sw)
    ikd_ref[...] = jnp.where(low, sw, r46).astype(bf16)

    ckv = ckv_ref[...].astype(f32)
    akv = (ckv * rms(ckv, MLA_KV_LORA) * glkv_ref[...]).astype(bf16)
    kvx = _dot(akv, wukv_ref[...])
    for h in range(MLA_HEADS):
        kn = kvx[:, 256 * h:256 * h + 128]
        ss = jnp.sum(kn * kn, axis=-1, keepdims=True) + ss_pe
        r = lax.rsqrt(ss * (1.0 / MLA_QK) + EPS)
        km_ref[h, :, 0:128] = (kn * r * gkn_ref[...]).astype(bf16)
        km_ref[h, :, 128:256] = (kpe2 * r).astype(bf16)
        vm_ref[h] = kvx[:, 256 * h + 128:256 * h + 256].astype(bf16)

    cd, s1d, s2d = tab(TB_CD), tab(TB_S1D), tab(TB_S2D)

    def dsa_head(x, g):
        y = x * rms(x, DSA_HEAD_DIM) * g
        return y * cd + _roll(y, 112) * s1d + _roll(y, 16) * s2d

    for h in range(DSA_HEADS):
        x = dq_ref[:, 128 * h:128 * h + 128].astype(f32)
        qd_ref[h] = dsa_head(x, gdq_ref[...]).astype(bf16)
    for g in range(DSA_KV_HEADS):
        x = dk_ref[:, 128 * g:128 * g + 128].astype(f32)
        kd_ref[g] = dsa_head(x, gdk_ref[...]).astype(bf16)

    ci, s1i, s2i = tab(TB_CI), tab(TB_S1I), tab(TB_S2I)
    for j in range(IDX_HEADS // 2):
        x = iq_ref[:, 128 * j:128 * j + 128].astype(f32)
        y = x * ci + _roll(x, 120) * s1i + _roll(x, 8) * s2i
        iqp_ref[:, 256 * j:256 * j + 128] = jnp.where(low, y, 0.0).astype(bf16)
        iqp_ref[:, 256 * j + 128:256 * j + 256] = jnp.where(low, 0.0, y).astype(bf16)


def _prep(z, basis, sel, glq, glkv, wuq, wukv, gq, gkn, g46, gdq, gdk, tm):
    t = z.shape[0]
    bf16 = jnp.bfloat16
    row = lambda w, c: pl.BlockSpec((tm, w), lambda i, c=c: (i, c))
    full = lambda a: pl.BlockSpec(a.shape, lambda i: (0,) * a.ndim)
    heads = lambda n, w: pl.BlockSpec((n, tm, w), lambda i: (0, i, 0))
    return pl.pallas_call(
        _prep_kernel,
        out_shape=(
            jax.ShapeDtypeStruct((MLA_HEADS, t, 256), bf16),
            jax.ShapeDtypeStruct((MLA_HEADS, t, 256), bf16),
            jax.ShapeDtypeStruct((MLA_HEADS, t, 128), bf16),
            jax.ShapeDtypeStruct((DSA_HEADS, t, 128), bf16),
            jax.ShapeDtypeStruct((DSA_KV_HEADS, t, 128), bf16),
            jax.ShapeDtypeStruct((t, IDX_HEADS * 128), bf16),
            jax.ShapeDtypeStruct((t, 128), bf16),
        ),
        grid=(t // tm,),
        in_specs=[
            row(512, Z_CQ // 512), row(512, Z_CKV // 512), row(128, Z_KPE_IK // 128),
            row(768, Z_DQ // 768), row(256, Z_DK // 256), row(1024, Z_IQ // 1024),
            pl.BlockSpec((tm, LANES), lambda i: (i, 0)), full(sel),
            full(glq), full(glkv), full(wuq), full(wukv), full(gq), full(gkn), full(g46), full(gdq), full(gdk),
        ],
        out_specs=(
            heads(MLA_HEADS, 256), heads(MLA_HEADS, 256), heads(MLA_HEADS, 128),
            heads(DSA_HEADS, 128), heads(DSA_KV_HEADS, 128),
            pl.BlockSpec((tm, IDX_HEADS * 128), lambda i: (i, 0)),
            pl.BlockSpec((tm, 128), lambda i: (i, 0)),
        ),
        compiler_params=_cparams(("parallel",)),
        name="prep",
    )(z, z, z, z, z, z, basis, sel, glq, glkv, wuq, wukv, gq, gkn, g46, gdq, gdk)


def _silu(x):
    return x / (1.0 + jnp.exp(-x))


def _conv_kernel(cx_ref, cb_ref, cc_ref, cz_ref, hx_ref, hc_ref, w_ref, o_ref, *, tiles_per_seq):
    f32 = jnp.float32
    tm, ch = cx_ref.shape
    u = cc_ref[...].astype(f32) * cx_ref[...].astype(f32)
    keep = (pl.program_id(0) % tiles_per_seq != 0).astype(f32)
    hu = hc_ref[...].astype(f32) * hx_ref[...].astype(f32) * keep
    row = lax.broadcasted_iota(jnp.int32, (tm, ch), 0)
    u1 = jnp.where(row == 0, hu[7:8, :], pltpu.roll(u, 1, 0))
    u2 = jnp.where(row == 0, hu[6:7, :], jnp.where(row == 1, hu[7:8, :], pltpu.roll(u, 2, 0)))
    conv = w_ref[0:1, :] * u2 + w_ref[1:2, :] * u1 + w_ref[2:3, :] * u
    o_ref[...] = (cb_ref[...].astype(f32) * conv * _silu(cz_ref[...].astype(f32))).astype(o_ref.dtype)


def _conv(z, w, seq, tm):
    t = z.shape[0]
    col = lambda c: pl.BlockSpec((tm, CONV_CH), lambda i, c=c: (i, c))
    halo = lambda c: pl.BlockSpec((8, CONV_CH), lambda i, c=c: (jnp.maximum(i * (tm // 8) - 1, 0), c))
    return pl.pallas_call(
        functools.partial(_conv_kernel, tiles_per_seq=seq // tm),
        out_shape=jax.ShapeDtypeStruct((t, CONV_CH), jnp.bfloat16),
        grid=(t // tm,),
        in_specs=[col(Z_CX // CONV_CH), col(Z_CB // CONV_CH), col(Z_CC // CONV_CH), col(Z_CONVZ // CONV_CH),
                  halo(Z_CX // CONV_CH), halo(Z_CC // CONV_CH),
                  pl.BlockSpec((8, CONV_CH), lambda i: (0, 0))],
        out_specs=pl.BlockSpec((tm, CONV_CH), lambda i: (i, 0)),
        compiler_params=_cparams(("parallel",)),
        name="conv",
    )(z, z, z, z, z, z, w)


def _tile_lanes(x, n):
    return x if n == 1 else jnp.concatenate([x] * n, axis=1)


def _softmax_step(s, v2, m_ref, acc_ref):
    m_prev = m_ref[...]
    m_new = jnp.maximum(m_prev, jnp.max(s, axis=-1, keepdims=True))
    p = jnp.exp2(s - _tile_lanes(m_new, s.shape[1] // LANES))
    alpha = jnp.exp2(m_prev - m_new)
    acc_ref[...] = _tile_lanes(alpha, 2) * acc_ref[...] + _dot(p.astype(jnp.bfloat16), v2)
    m_ref[...] = m_new


def _with_ones(v):
    return jnp.concatenate([v, jnp.ones_like(v)], axis=1)


MLA_HEADS_PER_STEP = 6


def _mla_kernel(q_ref, k_ref, v_ref, zg_ref, o_ref, m_ref, acc_ref, *, tk):
    f32 = jnp.float32
    qi = pl.program_id(2)
    m_ref[...] = jnp.full(m_ref.shape, NEG_BIG, f32)
    acc_ref[...] = jnp.zeros(acc_ref.shape, f32)

    def step(ki, row0, masked):
        keys = pl.ds(pl.multiple_of(ki * tk, tk), tk)
        rows = pl.ds(row0, q_ref.shape[1] - row0)
        for h in range(MLA_HEADS_PER_STEP):
            s = _dot_nt(q_ref[h, rows, :], k_ref[h, keys, :])
            if masked:
                r = lax.broadcasted_iota(jnp.int32, s.shape, 0)
                c = lax.broadcasted_iota(jnp.int32, s.shape, 1)
                s = jnp.where(c <= r, s, 2.0 * NEG_BIG)
            _softmax_step(s, _with_ones(v_ref[h, keys, :]), m_ref.at[h, rows, :], acc_ref.at[h, rows, :])

    def body(ki, c):
        step(ki, 0, False)
        return c

    lax.fori_loop(0, 2 * qi, body, 0)
    step(2 * qi, 0, True)
    step(2 * qi + 1, tk, True)
    for h in range(MLA_HEADS_PER_STEP):
        o = acc_ref[h, :, 0:128] / acc_ref[h, :, 128:256]
        zg = zg_ref[:, 128 * h:128 * h + 128].astype(f32)
        o_ref[:, 128 * h:128 * h + 128] = (o * _silu(zg)).astype(o_ref.dtype)


def _mla(qm, km, vm, z, batch, seq, tk):
    t = z.shape[0]
    tq = 2 * tk
    nq = seq // tq
    hs = MLA_HEADS_PER_STEP
    resident = dict(pipeline_mode=pl.Buffered(1))
    return pl.pallas_call(
        functools.partial(_mla_kernel, tk=tk),
        out_shape=jax.ShapeDtypeStruct((t, MLA_WIDTH), jnp.bfloat16),
        grid=(batch, MLA_HEADS // hs, nq),
        in_specs=[
            pl.BlockSpec((hs, tq, 256), lambda b, h, i: (h, b * nq + i, 0)),
            pl.BlockSpec((hs, seq, 256), lambda b, h, i: (h, b, 0), **resident),
            pl.BlockSpec((hs, seq, 128), lambda b, h, i: (h, b, 0), **resident),
            pl.BlockSpec((tq, 128 * hs), lambda b, h, i: (b * nq + i, Z_MLAZ // (128 * hs) + h)),
        ],
        out_specs=pl.BlockSpec((tq, 128 * hs), lambda b, h, i: (b * nq + i, h)),
        scratch_shapes=[pltpu.VMEM((hs, tq, LANES), jnp.float32), pltpu.VMEM((hs, tq, 256), jnp.float32)],
        compiler_params=_cparams(("parallel", "parallel", "arbitrary")),
        name="mla",
    )(qm, km, vm, z)


KC = 256
ATT_KC = 512
COUNT_FANIN = 8
SCORE_ROWS = 256


def _dsa_kernel(iq_ref, iw_ref, ik_ref, q_ref, k_ref, v0_ref, v1_ref, zg_ref, expand_ref, o_ref,
                key_ref, hi_ref, lo_ref, wb_ref, stage_ref, m_ref, acc_ref, *, k_top):
    f32, i32, bf16 = jnp.float32, jnp.int32, jnp.bfloat16
    tq = iq_ref.shape[0]
    i = pl.program_id(1)
    n_chunks = ((i + 1) * tq + KC - 1) // KC
    q_pos = i * tq + lax.broadcasted_iota(i32, (tq, KC), 0)
    lane_kc = lax.broadcasted_iota(i32, (tq, KC), 1)

    w = (iw_ref[...].astype(f32) * (IDX_HEADS ** -0.5)).astype(bf16)
    wb_ref[...] = _dot(w, expand_ref[...])

    def emit_keys(c):
        start = pl.multiple_of(c * KC, KC)
        bits = lax.bitcast_convert_type(stage_ref[...], i32)
        key = jnp.where(bits < 0, bits ^ jnp.int32(0x7FFFFFFF), bits)
        key = jnp.where(start + lane_kc <= q_pos, key, jnp.int32(INT_MIN))
        key_ref[:, pl.ds(start, KC)] = key
        key_t = key.T
        hi_ref[pl.ds(start, KC), :] = (key_t >> 16).astype(jnp.int16)
        lo_ref[pl.ds(start, KC), :] = ((key_t & 0xFFFF) - 32768).astype(jnp.int16)

    def score_chunk(c, carry):
        emit_keys(jnp.maximum(c - 1, 0))
        ikc = ik_ref[pl.ds(pl.multiple_of(c * KC, KC), KC), :]
        for r0 in range(0, tq, SCORE_ROWS):
            rows = pl.ds(r0, SCORE_ROWS)
            acc = jnp.zeros((SCORE_ROWS, KC), f32)
            for h in range(IDX_HEADS):
                lg = _dot_nt(iq_ref[rows, 128 * h:128 * h + 128], ikc)
                acc = acc + _tile_lanes(wb_ref[rows, 128 * h:128 * h + 128], KC // LANES) * jnp.maximum(lg, 0.0)
            stage_ref[rows, :] = acc
        return carry

    stage_ref[...] = jnp.zeros((tq, KC), f32)
    lax.fori_loop(0, n_chunks, score_chunk, 0)
    emit_keys(n_chunks - 1)

    n_att = (n_chunks * KC + ATT_KC - 1) // ATT_KC

    def fill(c, carry):
        start = pl.multiple_of(c * KC, KC)
        key_ref[:, pl.ds(start, KC)] = jnp.full((tq, KC), INT_MIN, i32)
        hi_ref[pl.ds(start, KC), :] = jnp.full((KC, tq), -32768, jnp.int16)
        lo_ref[pl.ds(start, KC), :] = jnp.full((KC, tq), -32768, jnp.int16)
        return carry

    lax.fori_loop(n_chunks, n_att * (ATT_KC // KC), fill, 0)

    i16 = jnp.int16
    low16 = jnp.full((), -32768, i16)
    one16, zero16 = jnp.ones((), i16), jnp.zeros((), i16)

    def count16(ref, cand):
        c16 = cand.astype(i16)
        def body(c, cnt):
            start = pl.multiple_of(c * ATT_KC, ATT_KC)
            for g in range(0, ATT_KC, 16 * COUNT_FANIN):
                hit = jnp.where(ref[pl.ds(start + g, 16 * COUNT_FANIN), :] >= c16, one16, zero16)
                part = [hit[16 * j:16 * (j + 1)] for j in range(COUNT_FANIN)]
                while len(part) > 1:
                    part = [a + b for a, b in zip(part[0::2], part[1::2])]
                cnt = cnt + part[0]
            return cnt
        cnt = lax.fori_loop(0, n_att, body, jnp.zeros((16, tq), i16))
        return jnp.sum(cnt.astype(i32), axis=0, keepdims=True)

    def search16(ref, want):
        def step(it, carry):
            r, n_r = carry
            cand = r + lax.shift_left(jnp.int32(1), 15 - it)
            n = count16(ref, cand)
            ok = n >= want
            return jnp.where(ok, cand, r), jnp.where(ok, n, n_r)
        return lax.fori_loop(0, 16, step, (jnp.full((1, tq), -32768, i32), jnp.zeros((1, tq), i32)))

    t_hi, n_hi = search16(hi_ref, k_top)
    n_above = count16(hi_ref, t_hi + 1)
    t_hi16 = t_hi.astype(i16)

    def restrict(c, carry):
        rows = pl.ds(pl.multiple_of(c * ATT_KC, ATT_KC), ATT_KC)
        lo_ref[rows, :] = jnp.where(hi_ref[rows, :] == t_hi16, lo_ref[rows, :], low16)
        return carry

    lax.fori_loop(0, n_att, restrict, 0)
    t_lo, n_lo = search16(lo_ref, k_top - n_above)
    thr = t_hi * 65536 + (t_lo + 32768)

    n_ge = jnp.where(t_lo > -32768, n_above + n_lo, jnp.where(t_hi > -32768, n_hi, n_above))
    has_excess = jnp.max(n_ge) > k_top

    @pl.when(has_excess)
    def _():
        def count_where(hits):
            def body(c, cnt):
                start = pl.multiple_of(c * KC, KC)
                hit = hits(key_ref[:, pl.ds(start, KC)].T, start)
                return cnt + jnp.sum(hit.reshape(KC // 8, 8, tq), axis=0)
            cnt = lax.fori_loop(0, n_chunks, body, jnp.zeros((8, tq), i32))
            return jnp.sum(cnt, axis=0, keepdims=True)

        need = k_top - count_where(lambda kt, _: jnp.where(kt > thr, 1, 0))
        sub = lax.broadcasted_iota(i32, (KC, tq), 0)

        def idx_step(it, bound):
            cand = bound + lax.shift_left(jnp.int32(1), 30 - it)
            n_below = count_where(lambda kt, start: jnp.where(kt == thr, jnp.where(start + sub < cand, 1, 0), 0))
            return jnp.where(n_below <= need, cand, bound)

        bound = lax.fori_loop(0, 31, idx_step, jnp.zeros((1, tq), i32))

        def demote(c, carry):
            start = pl.multiple_of(c * KC, KC)
            kt = key_ref[:, pl.ds(start, KC)].T
            drop = jnp.where(kt == thr, jnp.where(start + sub >= bound, 1, 0), 0)
            key_ref[:, pl.ds(start, KC)] = jnp.where(drop == 1, jnp.int32(INT_MIN), kt).T
            return carry

        lax.fori_loop(0, n_chunks, demote, 0)

    thr_sel = jnp.maximum(thr, jnp.int32(INT_MIN + 1))
    thr_kc = _tile_lanes(jnp.broadcast_to(thr_sel, (LANES, tq)).T, KC // LANES)

    m_ref[...] = jnp.full(m_ref.shape, NEG_BIG, f32)
    acc_ref[...] = jnp.zeros(acc_ref.shape, f32)
    v_refs = (v0_ref, v1_ref)

    thr_att = _tile_lanes(thr_kc, ATT_KC // KC)

    def attn_chunk(c, carry):
        start = pl.multiple_of(c * ATT_KC, ATT_KC)
        bias = jnp.where(key_ref[:, pl.ds(start, ATT_KC)] >= thr_att, 0.0, 2.0 * NEG_BIG)
        for hd in range(DSA_HEADS):
            g = hd // DSA_GROUP
            s = _dot_nt(q_ref[hd], k_ref[g, pl.ds(start, ATT_KC), :]) + bias
            _softmax_step(s, _with_ones(v_refs[g][pl.ds(start, ATT_KC), :]), m_ref.at[hd], acc_ref.at[hd])
        return carry

    lax.fori_loop(0, n_att, attn_chunk, 0)

    for hd in range(DSA_HEADS):
        o = acc_ref[hd, :, 0:128] / acc_ref[hd, :, 128:256]
        zg = zg_ref[:, 128 * hd:128 * hd + 128].astype(f32)
        o_ref[:, 128 * hd:128 * hd + 128] = (o * _silu(zg)).astype(o_ref.dtype)


def _dsa(iqp, ikd, qd, kd, z, batch, seq, tq, k_top):
    t = z.shape[0]
    nq = seq // tq
    f32 = jnp.float32
    expand = np.zeros((LANES, IDX_HEADS * LANES), np.float32)
    for h in range(IDX_HEADS):
        expand[h, LANES * h:LANES * (h + 1)] = 1.0
    expand = jnp.asarray(expand, jnp.bfloat16)
    return pl.pallas_call(
        functools.partial(_dsa_kernel, k_top=k_top),
        out_shape=jax.ShapeDtypeStruct((t, DSA_WIDTH), jnp.bfloat16),
        grid=(batch, nq),
        in_specs=[
            pl.BlockSpec((tq, IDX_HEADS * 128), lambda b, i: (b * nq + i, 0)),
            pl.BlockSpec((tq, 128), lambda b, i: (b * nq + i, Z_IW // 128)),
            pl.BlockSpec((seq, 128), lambda b, i: (b, 0)),
            pl.BlockSpec((DSA_HEADS, tq, 128), lambda b, i: (0, b * nq + i, 0)),
            pl.BlockSpec((DSA_KV_HEADS, seq, 128), lambda b, i: (0, b, 0)),
            pl.BlockSpec((seq, 128), lambda b, i: (b, Z_DV // 128)),
            pl.BlockSpec((seq, 128), lambda b, i: (b, Z_DV // 128 + 1)),
            pl.BlockSpec((tq, DSA_WIDTH), lambda b, i: (b * nq + i, Z_DSAZ // DSA_WIDTH)),
            pl.BlockSpec(expand.shape, lambda b, i: (0, 0)),
        ],
        out_specs=pl.BlockSpec((tq, DSA_WIDTH), lambda b, i: (b * nq + i, 0)),
        scratch_shapes=[
            pltpu.VMEM((tq, seq), jnp.int32),
            pltpu.VMEM((seq, tq), jnp.int16),
            pltpu.VMEM((seq, tq), jnp.int16),
            pltpu.VMEM((tq, IDX_HEADS * LANES), f32),
            pltpu.VMEM((tq, KC), f32),
            pltpu.VMEM((DSA_HEADS, tq, LANES), f32),
            pltpu.VMEM((DSA_HEADS, tq, 2 * DSA_HEAD_DIM), f32),
        ],
        compiler_params=_cparams(("parallel", "arbitrary")),
        name="dsa",
    )(iqp, z, ikd, qd, kd, z, z, z, expand)


def _post_kernel(h_ref, ya_ref, yb_ref, yc_ref, p_ref, g_ref, wa_ref, wb_ref, wc_ref, wg_ref, wp_ref, o_ref):
    h1 = h_ref[...] + _dot(ya_ref[...], wa_ref[...])
    h1 = h1 + _dot(yb_ref[...], wb_ref[...])
    h1 = h1 + _dot(yc_ref[...], wc_ref[...])
    r = lax.rsqrt(jnp.mean(h1 * h1, axis=-1, keepdims=True) + EPS)
    a = (h1 * r * g_ref[...]).astype(jnp.bfloat16)
    gate = _dot(a, wg_ref[...])
    gate = 1.0 / (1.0 + jnp.exp(-gate))
    o_ref[...] = h1 + gate * _dot(p_ref[...].astype(jnp.bfloat16), wp_ref[...])


def _post(h, ya, yb, yc, p, g, wa, wb, wc, wg, wp, tm):
    t, d = h.shape
    rows = lambda a: pl.BlockSpec((tm, a.shape[1]), lambda i: (i, 0))
    whole = lambda a: pl.BlockSpec(a.shape, lambda i: (0, 0), pipeline_mode=pl.Buffered(1))
    return pl.pallas_call(
        _post_kernel,
        out_shape=jax.ShapeDtypeStruct((t, d), jnp.float32),
        grid=(t // tm,),
        in_specs=[rows(h), rows(ya), rows(yb), rows(yc), rows(p), whole(g),
                  whole(wa), whole(wb), whole(wc), whole(wg), whole(wp)],
        out_specs=rows(h),
        compiler_params=_cparams(("parallel",)),
        name="post",
    )(h, ya, yb, yc, p, g, wa, wb, wc, wg, wp)


def _regroup_w_in(w):
    d = w.shape[0]
    w = w.astype(jnp.bfloat16)
    seg = lambda a, b: w[:, a:b]
    parts = [
        seg(_O_CQ, _O_CKV), seg(_O_CKV, _O_KPE),
        seg(_O_CX, _O_CB), seg(_O_CB, _O_CC), seg(_O_CC, _O_CONVZ), seg(_O_CONVZ, _O_DQ),
        seg(_O_MLAZ, _O_CX), seg(_O_DQ, _O_DK), seg(_O_DSAZ, _O_IQ),
        seg(_O_DK, _O_DV), seg(_O_DV, _O_DSAZ),
        seg(_O_KPE, _O_MLAZ), seg(_O_IK, _N_IN), seg(_O_IW, _O_IK),
        jnp.zeros((d, Z_IQ - Z_IW - IDX_HEADS), w.dtype),
        seg(_O_IQ, _O_IW),
    ]
    out = jnp.concatenate(parts, axis=1).astype(jnp.bfloat16)
    assert out.shape[1] == Z_WIDTH
    return out


def _pair_layout(a):
    lead = a.shape[:-1]
    a = a.reshape(lead + (MLA_HEADS, MLA_QK))
    nope, x1, x2 = a[..., :MLA_NOPE], a[..., MLA_NOPE:MLA_NOPE + 32], a[..., MLA_NOPE + 32:]
    return jnp.concatenate([nope, x1, x2, x2, x1], axis=-1).reshape(lead + (MLA_HEADS * 256,))


_B_C32, _B_S32, _B_C16, _B_S16, _B_C8, _B_S8, _B_ONE, _B_ROWS = 0, 32, 64, 80, 96, 104, 112, 128


def _table_selector():
    sel = np.zeros((_B_ROWS, N_TABLES * LANES), np.float32)

    def put(table, lane, base, n, coef=1.0):
        for j in range(n):
            sel[base + (j if base != _B_ONE else 0), table * LANES + lane + j] = coef

    c32, s32, c16, s16, c8, s8, one = _B_C32, _B_S32, _B_C16, _B_S16, _B_C8, _B_S8, _B_ONE
    put(TB_CSQ, 0, c32, 32); put(TB_CSQ, 32, c32, 32); put(TB_CSQ, 64, s32, 32, -1.0); put(TB_CSQ, 96, s32, 32)
    put(TB_C46, 0, c32, 32); put(TB_C46, 32, c32, 32); put(TB_C46, 64, c8, 8); put(TB_C46, 72, c8, 8)
    put(TB_C46, 80, one, 48)
    put(TB_S1A, 0, s32, 32, -1.0)
    put(TB_S2A, 32, s32, 32)
    put(TB_S1B, 64, s8, 8, -1.0)
    put(TB_S2B, 72, s8, 8)
    put(TB_CD, 0, c16, 16); put(TB_CD, 16, c16, 16); put(TB_CD, 32, one, 96)
    put(TB_S1D, 0, s16, 16, -1.0)
    put(TB_S2D, 16, s16, 16)
    qs = IDX_DIM ** -0.5
    for off in (0, 64):
        put(TB_CI, off, c8, 8, qs); put(TB_CI, off + 8, c8, 8, qs); put(TB_CI, off + 16, one, 48, qs)
        put(TB_S1I, off, s8, 8, -qs)
        put(TB_S2I, off + 8, s8, 8, qs)
    return sel


def _rope_basis(positions):
    f32 = jnp.float32
    pos = positions.reshape(-1, 1).astype(f32)
    freqs = [ROPE_THETA ** (-jnp.arange(half, dtype=f32) / half) for half in (MLA_ROPE // 2, DSA_ROT // 2, IDX_ROT // 2)]
    inv = jnp.concatenate([freqs[0], freqs[0], freqs[1], freqs[1], freqs[2], freqs[2], jnp.zeros((_B_ROWS - _B_ONE,), f32)])
    is_cos = np.zeros((_B_ROWS,), bool)
    for start, stop in ((_B_C32, _B_S32), (_B_C16, _B_S16), (_B_C8, _B_S8), (_B_ONE, _B_ONE + 1)):
        is_cos[start:stop] = True
    ang = pos * inv[None, :]
    basis = jnp.where(jnp.asarray(is_cos)[None, :], jnp.cos(ang), jnp.sin(ang))
    return basis, jnp.asarray(_table_selector(), jnp.bfloat16)


def _layer(h, p_i, tabs, batch, seq, k_top, norm_in, w_in, mla_gq, mla_w_uq, mla_gkv, mla_w_ukv, mla_qn, mla_kn,
           conv_w, dsa_qn, dsa_kn, w_out, ple_norm, ple_w_gate, ple_w_proj):
    f32, bf16 = jnp.float32, jnp.bfloat16
    t = h.shape[0]
    tm_big = min(1024, t)

    z = _inproj(h, norm_in[None, :], _regroup_w_in(w_in), tm_big, Z_WIDTH // 4)

    gq = _pair_layout(jnp.tile(mla_qn, MLA_HEADS))[None, :] * (MLA_QK ** -0.5 * LOG2E)
    g46 = jnp.concatenate([mla_kn[MLA_NOPE:], jnp.ones((64,), f32)])[None, :]
    qm, km, vm, qd, kd, iqp, ikd = _prep(
        z, *tabs, mla_gq[None, :], mla_gkv[None, :], _pair_layout(mla_w_uq).astype(bf16), mla_w_ukv.astype(bf16),
        gq, mla_kn[None, :MLA_NOPE], g46, dsa_qn[None, :] * (DSA_HEAD_DIM ** -0.5 * LOG2E), dsa_kn[None, :], 256)

    y_b = _conv(z, jnp.pad(conv_w, ((0, 8 - CONV_K), (0, 0))), seq, 512)
    y_a = _mla(qm, km, vm, z, batch, seq, 512)
    y_c = _dsa(iqp, ikd, qd, kd, z, batch, seq, 512, k_top)

    wo = w_out.astype(bf16)
    return _post(h, y_a, y_b, y_c, p_i, ple_norm[None, :],
                 wo[:MLA_WIDTH], wo[MLA_WIDTH:MLA_WIDTH + CONV_CH], wo[MLA_WIDTH + CONV_CH:],
                 ple_w_gate.astype(bf16), ple_w_proj.astype(bf16), min(512, t))


def kernel(x, p, positions, norm_in, w_in, mla_gq, mla_w_uq, mla_gkv, mla_w_ukv, mla_qn, mla_kn, conv_w, dsa_qn,
           dsa_kn, w_out, ple_norm, ple_w_gate, ple_w_proj):
    batch, seq, d = x.shape
    depth = p.shape[0]
    t = batch * seq
    k_top = min(TOPK_MAX, seq // 4)
    tabs = _rope_basis(positions)
    h = x.reshape(t, d)
    for i in range(depth):
        h = _layer(h, p[i].reshape(t, PLE_DIM), tabs, batch, seq, k_top, norm_in[i], w_in[i], mla_gq[i],
                   mla_w_uq[i], mla_gkv[i], mla_w_ukv[i], mla_qn[i], mla_kn[i], conv_w[i], dsa_qn[i], dsa_kn[i],
                   w_out[i], ple_norm[i], ple_w_gate[i], ple_w_proj[i])
    return h.reshape(batch, seq, d)
```

```python
import functools

import numpy as np
import jax
import jax.numpy as jnp
from jax import lax
from jax.experimental import pallas as pl
from jax.experimental.pallas import tpu as pltpu

D_MODEL = 2048
PLE_DIM = 256
ROPE_THETA = 500000.0
EPS = 1e-6

MLA_HEADS = 6
MLA_Q_LORA = 512
MLA_KV_LORA = 512
MLA_NOPE = 128
MLA_ROPE = 64
MLA_V = 128
MLA_QK = MLA_NOPE + MLA_ROPE
MLA_WIDTH = MLA_HEADS * MLA_V

CONV_CH = 512
CONV_K = 3

DSA_HEADS = 6
DSA_KV_HEADS = 2
DSA_GROUP = DSA_HEADS // DSA_KV_HEADS
DSA_HEAD_DIM = 128
DSA_WIDTH = DSA_HEADS * DSA_HEAD_DIM
DSA_ROT = DSA_HEAD_DIM // 4
IDX_HEADS = 16
IDX_DIM = 64
IDX_ROT = IDX_DIM // 4
TOPK_MAX = 256

LANES = 128

_O_CQ, _O_CKV, _O_KPE, _O_MLAZ = 0, 512, 1024, 1088
_O_CX, _O_CB, _O_CC, _O_CONVZ = 1856, 2368, 2880, 3392
_O_DQ, _O_DK, _O_DV, _O_DSAZ = 3904, 4672, 4928, 5184
_O_IQ, _O_IW, _O_IK, _N_IN = 5952, 6976, 6992, 7056

Z_CQ, Z_CKV = 0, 512
Z_CX, Z_CB, Z_CC, Z_CONVZ = 1024, 1536, 2048, 2560
Z_MLAZ, Z_DQ, Z_DSAZ = 3072, 3840, 4608
Z_DK, Z_DV = 5376, 5632
Z_KPE_IK, Z_IW, Z_IQ = 5888, 6016, 6144
Z_WIDTH = 7168

LOG2E = 1.4426950408889634
INT_MIN = -(2 ** 31)
NEG_BIG = -1e30

VMEM_LIMIT = 56 * 1024 * 1024


def _cparams(sem):
    return pltpu.CompilerParams(dimension_semantics=sem, vmem_limit_bytes=VMEM_LIMIT)


def _dot(a, b):
    return jnp.dot(a, b, preferred_element_type=jnp.float32)


def _dot_nt(a, b):
    return lax.dot_general(a, b, (((1,), (1,)), ((), ())), preferred_element_type=jnp.float32)


def _roll(x, shift):
    return pltpu.roll(x, shift, 1)


def _inproj_kernel(x_ref, g_ref, w_ref, o_ref, a_ref):
    @pl.when(pl.program_id(1) == 0)
    def _():
        x = x_ref[...]
        r = lax.rsqrt(jnp.mean(x * x, axis=-1, keepdims=True) + EPS)
        a_ref[...] = (x * r * g_ref[...]).astype(jnp.bfloat16)

    o_ref[...] = _dot(a_ref[...], w_ref[...]).astype(o_ref.dtype)


def _inproj(h, g, w, tm, tn):
    t, d = h.shape
    n = w.shape[1]
    return pl.pallas_call(
        _inproj_kernel,
        out_shape=jax.ShapeDtypeStruct((t, n), jnp.bfloat16),
        grid=(t // tm, n // tn),
        in_specs=[
            pl.BlockSpec((tm, d), lambda i, j: (i, 0)),
            pl.BlockSpec((1, d), lambda i, j: (0, 0)),
            pl.BlockSpec((d, tn), lambda i, j: (0, j)),
        ],
        out_specs=pl.BlockSpec((tm, tn), lambda i, j: (i, j)),
        scratch_shapes=[pltpu.VMEM((tm, d), jnp.bfloat16)],
        compiler_params=_cparams(("parallel", "arbitrary")),
        name="inproj",
    )(h, g, w)


TB_CSQ = 0
TB_C46, TB_S1A, TB_S2A, TB_S1B, TB_S2B = 1, 2, 3, 4, 5
TB_CD, TB_S1D, TB_S2D = 6, 7, 8
TB_CI, TB_S1I, TB_S2I = 9, 10, 11
N_TABLES = 12


def _prep_kernel(cq_ref, ckv_ref, t46_ref, dq_ref, dk_ref, iq_ref, basis_ref, sel_ref,
                 glq_ref, glkv_ref, wuq_ref, wukv_ref, gq_ref, gkn_ref, g46_ref, gdq_ref, gdk_ref,
                 qm_ref, km_ref, vm_ref, qd_ref, kd_ref, iqp_ref, ikd_ref):
    f32, bf16 = jnp.float32, jnp.bfloat16
    tm = cq_ref.shape[0]
    lane = lax.broadcasted_iota(jnp.int32, (tm, LANES), 1)
    low = lane < 64

    b0 = basis_ref[...]
    b_hi = b0.astype(bf16)
    b1 = b0 - b_hi.astype(f32)
    b_mid = b1.astype(bf16)
    b_lo = (b1 - b_mid.astype(f32)).astype(bf16)
    tabs = _dot(b_hi, sel_ref[...]) + _dot(b_mid, sel_ref[...]) + _dot(b_lo, sel_ref[...])
    tab = lambda k: tabs[:, LANES * k:LANES * (k + 1)]

    def rms(x, n):
        return lax.rsqrt(jnp.sum(x * x, axis=-1, keepdims=True) * (1.0 / n) + EPS)

    cq = cq_ref[...].astype(f32)
    aq = (cq * rms(cq, MLA_Q_LORA) * glq_ref[...]).astype(bf16)
    qx = _dot(aq, wuq_ref[...])
    csq = tab(TB_CSQ)
    for h in range(MLA_HEADS):
        nope = qx[:, 256 * h:256 * h + 128]
        pair = qx[:, 256 * h + 128:256 * h + 256]
        ss = jnp.sum(nope * nope, axis=-1, keepdims=True) + 0.5 * jnp.sum(pair * pair, axis=-1, keepdims=True)
        r = lax.rsqrt(ss * (1.0 / MLA_QK) + EPS)
        qm_ref[h, :, 0:128] = (nope * r * gq_ref[:, 256 * h:256 * h + 128]).astype(bf16)
        qm_ref[h, :, 128:256] = (pair * r * gq_ref[:, 256 * h + 128:256 * h + 256] * csq).astype(bf16)

    t46 = t46_ref[...].astype(f32)
    ss_pe = jnp.sum(jnp.where(low, t46 * t46, 0.0), axis=-1, keepdims=True)
    y46 = t46 * g46_ref[...]
    r46 = (y46 * tab(TB_C46)
           + _roll(y46, 96) * tab(TB_S1A) + _roll(y46, 32) * tab(TB_S2A)
           + _roll(y46, 120) * tab(TB_S1B) + _roll(y46, 8) * tab(TB_S2B))
    sw = _roll(r46, 64)
    kpe2 = jnp.where(low, r46, sw)
    ikd_ref[...] = jnp.where(low, sw, r46).astype(bf16)

    ckv = ckv_ref[...].astype(f32)
    akv = (ckv * rms(ckv, MLA_KV_LORA) * glkv_ref[...]).astype(bf16)
    kvx = _dot(akv, wukv_ref[...])
    for h in range(MLA_HEADS):
        kn = kvx[:, 256 * h:256 * h + 128]
        ss = jnp.sum(kn * kn, axis=-1, keepdims=True) + ss_pe
        r = lax.rsqrt(ss * (1.0 / MLA_QK) + EPS)
        km_ref[h, :, 0:128] = (kn * r * gkn_ref[...]).astype(bf16)
        km_ref[h, :, 128:256] = (kpe2 * r).astype(bf16)
        vm_ref[h] = kvx[:, 256 * h + 128:256 * h + 256].astype(bf16)

    cd, s1d, s2d = tab(TB_CD), tab(TB_S1D), tab(TB_S2D)

    def dsa_head(x, g):
        y = x * rms(x, DSA_HEAD_DIM) * g
        return y * cd + _roll(y, 112) * s1d + _roll(y, 16) * s2d

    for h in range(DSA_HEADS):
        x = dq_ref[:, 128 * h:128 * h + 128].astype(f32)
        qd_ref[h] = dsa_head(x, gdq_ref[...]).astype(bf16)
    for g in range(DSA_KV_HEADS):
        x = dk_ref[:, 128 * g:128 * g + 128].astype(f32)
        kd_ref[g] = dsa_head(x, gdk_ref[...]).astype(bf16)

    ci, s1i, s2i = tab(TB_CI), tab(TB_S1I), tab(TB_S2I)
    for j in range(IDX_HEADS // 2):
        x = iq_ref[:, 128 * j:128 * j + 128].astype(f32)
        y = x * ci + _roll(x, 120) * s1i + _roll(x, 8) * s2i
        iqp_ref[:, 256 * j:256 * j + 128] = jnp.where(low, y, 0.0).astype(bf16)
        iqp_ref[:, 256 * j + 128:256 * j + 256] = jnp.where(low, 0.0, y).astype(bf16)


def _prep(z, basis, sel, glq, glkv, wuq, wukv, gq, gkn, g46, gdq, gdk, tm):
    t = z.shape[0]
    bf16 = jnp.bfloat16
    row = lambda w, c: pl.BlockSpec((tm, w), lambda i, c=c: (i, c))
    full = lambda a: pl.BlockSpec(a.shape, lambda i: (0,) * a.ndim)
    heads = lambda n, w: pl.BlockSpec((n, tm, w), lambda i: (0, i, 0))
    return pl.pallas_call(
        _prep_kernel,
        out_shape=(
            jax.ShapeDtypeStruct((MLA_HEADS, t, 256), bf16),
            jax.ShapeDtypeStruct((MLA_HEADS, t, 256), bf16),
            jax.ShapeDtypeStruct((MLA_HEADS, t, 128), bf16),
            jax.ShapeDtypeStruct((DSA_HEADS, t, 128), bf16),
            jax.ShapeDtypeStruct((DSA_KV_HEADS, t, 128), bf16),
            jax.ShapeDtypeStruct((t, IDX_HEADS * 128), bf16),
            jax.ShapeDtypeStruct((t, 128), bf16),
        ),
        grid=(t // tm,),
        in_specs=[
            row(512, Z_CQ // 512), row(512, Z_CKV // 512), row(128, Z_KPE_IK // 128),
            row(768, Z_DQ // 768), row(256, Z_DK // 256), row(1024, Z_IQ // 1024),
            pl.BlockSpec((tm, LANES), lambda i: (i, 0)), full(sel),
            full(glq), full(glkv), full(wuq), full(wukv), full(gq), full(gkn), full(g46), full(gdq), full(gdk),
        ],
        out_specs=(
            heads(MLA_HEADS, 256), heads(MLA_HEADS, 256), heads(MLA_HEADS, 128),
            heads(DSA_HEADS, 128), heads(DSA_KV_HEADS, 128),
            pl.BlockSpec((tm, IDX_HEADS * 128), lambda i: (i, 0)),
            pl.BlockSpec((tm, 128), lambda i: (i, 0)),
        ),
        compiler_params=_cparams(("parallel",)),
        name="prep",
    )(z, z, z, z, z, z, basis, sel, glq, glkv, wuq, wukv, gq, gkn, g46, gdq, gdk)


def _silu(x):
    return x / (1.0 + jnp.exp(-x))


def _conv_kernel(cx_ref, cb_ref, cc_ref, cz_ref, hx_ref, hc_ref, w_ref, o_ref, *, tiles_per_seq):
    f32 = jnp.float32
    tm, ch = cx_ref.shape
    u = cc_ref[...].astype(f32) * cx_ref[...].astype(f32)
    keep = (pl.program_id(0) % tiles_per_seq != 0).astype(f32)
    hu = hc_ref[...].astype(f32) * hx_ref[...].astype(f32) * keep
    row = lax.broadcasted_iota(jnp.int32, (tm, ch), 0)
    u1 = jnp.where(row == 0, hu[7:8, :], pltpu.roll(u, 1, 0))
    u2 = jnp.where(row == 0, hu[6:7, :], jnp.where(row == 1, hu[7:8, :], pltpu.roll(u, 2, 0)))
    conv = w_ref[0:1, :] * u2 + w_ref[1:2, :] * u1 + w_ref[2:3, :] * u
    o_ref[...] = (cb_ref[...].astype(f32) * conv * _silu(cz_ref[...].astype(f32))).astype(o_ref.dtype)


def _conv(z, w, seq, tm):
    t = z.shape[0]
    col = lambda c: pl.BlockSpec((tm, CONV_CH), lambda i, c=c: (i, c))
    halo = lambda c: pl.BlockSpec((8, CONV_CH), lambda i, c=c: (jnp.maximum(i * (tm // 8) - 1, 0), c))
    return pl.pallas_call(
        functools.partial(_conv_kernel, tiles_per_seq=seq // tm),
        out_shape=jax.ShapeDtypeStruct((t, CONV_CH), jnp.bfloat16),
        grid=(t // tm,),
        in_specs=[col(Z_CX // CONV_CH), col(Z_CB // CONV_CH), col(Z_CC // CONV_CH), col(Z_CONVZ // CONV_CH),
                  halo(Z_CX // CONV_CH), halo(Z_CC // CONV_CH),
                  pl.BlockSpec((8, CONV_CH), lambda i: (0, 0))],
        out_specs=pl.BlockSpec((tm, CONV_CH), lambda i: (i, 0)),
        compiler_params=_cparams(("parallel",)),
        name="conv",
    )(z, z, z, z, z, z, w)


def _tile_lanes(x, n):
    return x if n == 1 else jnp.concatenate([x] * n, axis=1)


def _softmax_step(s, v2, m_ref, acc_ref):
    m_prev = m_ref[...]
    m_new = jnp.maximum(m_prev, jnp.max(s, axis=-1, keepdims=True))
    p = jnp.exp2(s - _tile_lanes(m_new, s.shape[1] // LANES))
    alpha = jnp.exp2(m_prev - m_new)
    acc_ref[...] = _tile_lanes(alpha, 2) * acc_ref[...] + _dot(p.astype(jnp.bfloat16), v2)
    m_ref[...] = m_new


def _with_ones(v):
    return jnp.concatenate([v, jnp.ones_like(v)], axis=1)


MLA_HEADS_PER_STEP = 6


def _mla_kernel(q_ref, k_ref, v_ref, zg_ref, o_ref, m_ref, acc_ref, *, tk):
    f32 = jnp.float32
    qi = pl.program_id(2)
    m_ref[...] = jnp.full(m_ref.shape, NEG_BIG, f32)
    acc_ref[...] = jnp.zeros(acc_ref.shape, f32)

    def step(ki, row0, masked):
        keys = pl.ds(pl.multiple_of(ki * tk, tk), tk)
        rows = pl.ds(row0, q_ref.shape[1] - row0)
        for h in range(MLA_HEADS_PER_STEP):
            s = _dot_nt(q_ref[h, rows, :], k_ref[h, keys, :])
            if masked:
                r = lax.broadcasted_iota(jnp.int32, s.shape, 0)
                c = lax.broadcasted_iota(jnp.int32, s.shape, 1)
                s = jnp.where(c <= r, s, 2.0 * NEG_BIG)
            _softmax_step(s, _with_ones(v_ref[h, keys, :]), m_ref.at[h, rows, :], acc_ref.at[h, rows, :])

    def body(ki, c):
        step(ki, 0, False)
        return c

    lax.fori_loop(0, 2 * qi, body, 0)
    step(2 * qi, 0, True)
    step(2 * qi + 1, tk, True)
    for h in range(MLA_HEADS_PER_STEP):
        o = acc_ref[h, :, 0:128] / acc_ref[h, :, 128:256]
        zg = zg_ref[:, 128 * h:128 * h + 128].astype(f32)
        o_ref[:, 128 * h:128 * h + 128] = (o * _silu(zg)).astype(o_ref.dtype)


def _mla(qm, km, vm, z, batch, seq, tk):
    t = z.shape[0]
    tq = 2 * tk
    nq = seq // tq
    hs = MLA_HEADS_PER_STEP
    resident = dict(pipeline_mode=pl.Buffered(1))
    return pl.pallas_call(
        functools.partial(_mla_kernel, tk=tk),
        out_shape=jax.ShapeDtypeStruct((t, MLA_WIDTH), jnp.bfloat16),
        grid=(batch, MLA_HEADS // hs, nq),
        in_specs=[
            pl.BlockSpec((hs, tq, 256), lambda b, h, i: (h, b * nq + i, 0)),
            pl.BlockSpec((hs, seq, 256), lambda b, h, i: (h, b, 0), **resident),
            pl.BlockSpec((hs, seq, 128), lambda b, h, i: (h, b, 0), **resident),
            pl.BlockSpec((tq, 128 * hs), lambda b, h, i: (b * nq + i, Z_MLAZ // (128 * hs) + h)),
        ],
        out_specs=pl.BlockSpec((tq, 128 * hs), lambda b, h, i: (b * nq + i, h)),
        scratch_shapes=[pltpu.VMEM((hs, tq, LANES), jnp.float32), pltpu.VMEM((hs, tq, 256), jnp.float32)],
        compiler_params=_cparams(("parallel", "parallel", "arbitrary")),
        name="mla",
    )(qm, km, vm, z)


KC = 256
ATT_KC = 512
COUNT_FANIN = 8
SCORE_ROWS = 256


def _dsa_kernel(iq_ref, iw_ref, ik_ref, q_ref, k_ref, v0_ref, v1_ref, zg_ref, expand_ref, o_ref,
                key_ref, hi_ref, lo_ref, wb_ref, stage_ref, m_ref, acc_ref, *, k_top):
    f32, i32, bf16 = jnp.float32, jnp.int32, jnp.bfloat16
    tq = iq_ref.shape[0]
    i = pl.program_id(1)
    n_chunks = ((i + 1) * tq + KC - 1) // KC
    q_pos = i * tq + lax.broadcasted_iota(i32, (tq, KC), 0)
    lane_kc = lax.broadcasted_iota(i32, (tq, KC), 1)

    w = (iw_ref[...].astype(f32) * (IDX_HEADS ** -0.5)).astype(bf16)
    wb_ref[...] = _dot(w, expand_ref[...])

    def emit_keys(c):
        start = pl.multiple_of(c * KC, KC)
        bits = lax.bitcast_convert_type(stage_ref[...], i32)
        key = jnp.where(bits < 0, bits ^ jnp.int32(0x7FFFFFFF), bits)
        key = jnp.where(start + lane_kc <= q_pos, key, jnp.int32(INT_MIN))
        key_ref[:, pl.ds(start, KC)] = key
        key_t = key.T
        hi_ref[pl.ds(start, KC), :] = (key_t >> 16).astype(jnp.int16)
        lo_ref[pl.ds(start, KC), :] = ((key_t & 0xFFFF) - 32768).astype(jnp.int16)

    def score_chunk(c, carry):
        emit_keys(jnp.maximum(c - 1, 0))
        ikc = ik_ref[pl.ds(pl.multiple_of(c * KC, KC), KC), :]
        for r0 in range(0, tq, SCORE_ROWS):
            rows = pl.ds(r0, SCORE_ROWS)
            acc = jnp.zeros((SCORE_ROWS, KC), f32)
            for h in range(IDX_HEADS):
                lg = _dot_nt(iq_ref[rows, 128 * h:128 * h + 128], ikc)
                acc = acc + _tile_lanes(wb_ref[rows, 128 * h:128 * h + 128], KC // LANES) * jnp.maximum(lg, 0.0)
            stage_ref[rows, :] = acc
        return carry

    stage_ref[...] = jnp.zeros((tq, KC), f32)
    lax.fori_loop(0, n_chunks, score_chunk, 0)
    emit_keys(n_chunks - 1)

    n_att = (n_chunks * KC + ATT_KC - 1) // ATT_KC

    def fill(c, carry):
        start = pl.multiple_of(c * KC, KC)
        key_ref[:, pl.ds(start, KC)] = jnp.full((tq, KC), INT_MIN, i32)
        hi_ref[pl.ds(start, KC), :] = jnp.full((KC, tq), -32768, jnp.int16)
        lo_ref[pl.ds(start, KC), :] = jnp.full((KC, tq), -32768, jnp.int16)
        return carry

    lax.fori_loop(n_chunks, n_att * (ATT_KC // KC), fill, 0)

    i16 = jnp.int16
    low16 = jnp.full((), -32768, i16)
    one16, zero16 = jnp.ones((), i16), jnp.zeros((), i16)

    def count16(ref, cand):
        c16 = cand.astype(i16)
        def body(c, cnt):
            start = pl.multiple_of(c * ATT_KC, ATT_KC)
            for g in range(0, ATT_KC, 16 * COUNT_FANIN):
                hit = jnp.where(ref[pl.ds(start + g, 16 * COUNT_FANIN), :] >= c16, one16, zero16)
                part = [hit[16 * j:16 * (j + 1)] for j in range(COUNT_FANIN)]
                while len(part) > 1:
                    part = [a + b for a, b in zip(part[0::2], part[1::2])]
                cnt = cnt + part[0]
            return cnt
        cnt = lax.fori_loop(0, n_att, body, jnp.zeros((16, tq), i16))
        return jnp.sum(cnt.astype(i32), axis=0, keepdims=True)

    def search16(ref, want):
        def step(it, carry):
            r, n_r = carry
            cand = r + lax.shift_left(jnp.int32(1), 15 - it)
            n = count16(ref, cand)
            ok = n >= want
            return jnp.where(ok, cand, r), jnp.where(ok, n, n_r)
        return lax.fori_loop(0, 16, step, (jnp.full((1, tq), -32768, i32), jnp.zeros((1, tq), i32)))

    t_hi, n_hi = search16(hi_ref, k_top)
    n_above = count16(hi_ref, t_hi + 1)
    t_hi16 = t_hi.astype(i16)

    def restrict(c, carry):
        rows = pl.ds(pl.multiple_of(c * ATT_KC, ATT_KC), ATT_KC)
        lo_ref[rows, :] = jnp.where(hi_ref[rows, :] == t_hi16, lo_ref[rows, :], low16)
        return carry

    lax.fori_loop(0, n_att, restrict, 0)
    t_lo, n_lo = search16(lo_ref, k_top - n_above)
    thr = t_hi * 65536 + (t_lo + 32768)

    n_ge = jnp.where(t_lo > -32768, n_above + n_lo, jnp.where(t_hi > -32768, n_hi, n_above))
    has_excess = jnp.max(n_ge) > k_top

    @pl.when(has_excess)
    def _():
        def count_where(hits):
            def body(c, cnt):
                start = pl.multiple_of(c * KC, KC)
                hit = hits(key_ref[:, pl.ds(start, KC)].T, start)
                return cnt + jnp.sum(hit.reshape(KC // 8, 8, tq), axis=0)
            cnt = lax.fori_loop(0, n_chunks, body, jnp.zeros((8, tq), i32))
            return jnp.sum(cnt, axis=0, keepdims=True)

        need = k_top - count_where(lambda kt, _: jnp.where(kt > thr, 1, 0))
        sub = lax.broadcasted_iota(i32, (KC, tq), 0)

        def idx_step(it, bound):
            cand = bound + lax.shift_left(jnp.int32(1), 30 - it)
            n_below = count_where(lambda kt, start: jnp.where(kt == thr, jnp.where(start + sub < cand, 1, 0), 0))
            return jnp.where(n_below <= need, cand, bound)

        bound = lax.fori_loop(0, 31, idx_step, jnp.zeros((1, tq), i32))

        def demote(c, carry):
            start = pl.multiple_of(c * KC, KC)
            kt = key_ref[:, pl.ds(start, KC)].T
            drop = jnp.where(kt == thr, jnp.where(start + sub >= bound, 1, 0), 0)
            key_ref[:, pl.ds(start, KC)] = jnp.where(drop == 1, jnp.int32(INT_MIN), kt).T
            return carry

        lax.fori_loop(0, n_chunks, demote, 0)

    thr_sel = jnp.maximum(thr, jnp.int32(INT_MIN + 1))
    thr_kc = _tile_lanes(jnp.broadcast_to(thr_sel, (LANES, tq)).T, KC // LANES)

    m_ref[...] = jnp.full(m_ref.shape, NEG_BIG, f32)
    acc_ref[...] = jnp.zeros(acc_ref.shape, f32)
    v_refs = (v0_ref, v1_ref)

    thr_att = _tile_lanes(thr_kc, ATT_KC // KC)

    def attn_chunk(c, carry):
        start = pl.multiple_of(c * ATT_KC, ATT_KC)
        bias = jnp.where(key_ref[:, pl.ds(start, ATT_KC)] >= thr_att, 0.0, 2.0 * NEG_BIG)
        for hd in range(DSA_HEADS):
            g = hd // DSA_GROUP
            s = _dot_nt(q_ref[hd], k_ref[g, pl.ds(start, ATT_KC), :]) + bias
            _softmax_step(s, _with_ones(v_refs[g][pl.ds(start, ATT_KC), :]), m_ref.at[hd], acc_ref.at[hd])
        return carry

    lax.fori_loop(0, n_att, attn_chunk, 0)

    for hd in range(DSA_HEADS):
        o = acc_ref[hd, :, 0:128] / acc_ref[hd, :, 128:256]
        zg = zg_ref[:, 128 * hd:128 * hd + 128].astype(f32)
        o_ref[:, 128 * hd:128 * hd + 128] = (o * _silu(zg)).astype(o_ref.dtype)


def _dsa(iqp, ikd, qd, kd, z, batch, seq, tq, k_top):
    t = z.shape[0]
    nq = seq // tq
    f32 = jnp.float32
    expand = np.zeros((LANES, IDX_HEADS * LANES), np.float32)
    for h in range(IDX_HEADS):
        expand[h, LANES * h:LANES * (h + 1)] = 1.0
    expand = jnp.asarray(expand, jnp.bfloat16)
    return pl.pallas_call(
        functools.partial(_dsa_kernel, k_top=k_top),
        out_shape=jax.ShapeDtypeStruct((t, DSA_WIDTH), jnp.bfloat16),
        grid=(batch, nq),
        in_specs=[
            pl.BlockSpec((tq, IDX_HEADS * 128), lambda b, i: (b * nq + i, 0)),
            pl.BlockSpec((tq, 128), lambda b, i: (b * nq + i, Z_IW // 128)),
            pl.BlockSpec((seq, 128), lambda b, i: (b, 0)),
            pl.BlockSpec((DSA_HEADS, tq, 128), lambda b, i: (0, b * nq + i, 0)),
            pl.BlockSpec((DSA_KV_HEADS, seq, 128), lambda b, i: (0, b, 0)),
            pl.BlockSpec((seq, 128), lambda b, i: (b, Z_DV // 128)),
            pl.BlockSpec((seq, 128), lambda b, i: (b, Z_DV // 128 + 1)),
            pl.BlockSpec((tq, DSA_WIDTH), lambda b, i: (b * nq + i, Z_DSAZ // DSA_WIDTH)),
            pl.BlockSpec(expand.shape, lambda b, i: (0, 0)),
        ],
        out_specs=pl.BlockSpec((tq, DSA_WIDTH), lambda b, i: (b * nq + i, 0)),
        scratch_shapes=[
            pltpu.VMEM((tq, seq), jnp.int32),
            pltpu.VMEM((seq, tq), jnp.int16),
            pltpu.VMEM((seq, tq), jnp.int16),
            pltpu.VMEM((tq, IDX_HEADS * LANES), f32),
            pltpu.VMEM((tq, KC), f32),
            pltpu.VMEM((DSA_HEADS, tq, LANES), f32),
            pltpu.VMEM((DSA_HEADS, tq, 2 * DSA_HEAD_DIM), f32),
        ],
        compiler_params=_cparams(("parallel", "arbitrary")),
        name="dsa",
    )(iqp, z, ikd, qd, kd, z, z, z, expand)


def _post_kernel(h_ref, ya_ref, yb_ref, yc_ref, p_ref, g_ref, wa_ref, wb_ref, wc_ref, wg_ref, wp_ref, o_ref):
    h1 = h_ref[...] + _dot(ya_ref[...], wa_ref[...])
    h1 = h1 + _dot(yb_ref[...], wb_ref[...])
    h1 = h1 + _dot(yc_ref[...], wc_ref[...])
    r = lax.rsqrt(jnp.mean(h1 * h1, axis=-1, keepdims=True) + EPS)
    a = (h1 * r * g_ref[...]).astype(jnp.bfloat16)
    gate = _dot(a, wg_ref[...])
    gate = 1.0 / (1.0 + jnp.exp(-gate))
    o_ref[...] = h1 + gate * _dot(p_ref[...].astype(jnp.bfloat16), wp_ref[...])


def _post(h, ya, yb, yc, p, g, wa, wb, wc, wg, wp, tm):
    t, d = h.shape
    rows = lambda a: pl.BlockSpec((tm, a.shape[1]), lambda i: (i, 0))
    whole = lambda a: pl.BlockSpec(a.shape, lambda i: (0, 0), pipeline_mode=pl.Buffered(1))
    return pl.pallas_call(
        _post_kernel,
        out_shape=jax.ShapeDtypeStruct((t, d), jnp.float32),
        grid=(t // tm,),
        in_specs=[rows(h), rows(ya), rows(yb), rows(yc), rows(p), whole(g),
                  whole(wa), whole(wb), whole(wc), whole(wg), whole(wp)],
        out_specs=rows(h),
        compiler_params=_cparams(("parallel",)),
        name="post",
    )(h, ya, yb, yc, p, g, wa, wb, wc, wg, wp)


_REGROUP = (
    (Z_CQ, _O_CQ, 512), (Z_CKV, _O_CKV, 512),
    (Z_CX, _O_CX, 512), (Z_CB, _O_CB, 512), (Z_CC, _O_CC, 512), (Z_CONVZ, _O_CONVZ, 512),
    (Z_MLAZ, _O_MLAZ, 768), (Z_DQ, _O_DQ, 768), (Z_DSAZ, _O_DSAZ, 768),
    (Z_DK, _O_DK, 256), (Z_DV, _O_DV, 256), (Z_IQ, _O_IQ, 1024),
)


def _regroup_kernel(w_ref, o_ref):
    bf16 = jnp.bfloat16
    for dst, src, n in _REGROUP:
        o_ref[0, :, dst:dst + n] = w_ref[0, :, src:src + n].astype(bf16)
    rows = w_ref.shape[1]
    o_ref[0, :, Z_KPE_IK:Z_KPE_IK + 128] = jnp.concatenate(
        [w_ref[0, :, _O_KPE:_O_KPE + MLA_ROPE], w_ref[0, :, _O_IK:_O_IK + IDX_DIM]], axis=1).astype(bf16)
    o_ref[0, :, Z_IW:Z_IW + 128] = jnp.concatenate(
        [w_ref[0, :, _O_IW:_O_IW + IDX_HEADS], jnp.zeros((rows, 128 - IDX_HEADS), jnp.float32)], axis=1).astype(bf16)


def _regroup_w_in(w, tm=256):
    depth, d, n = w.shape
    return pl.pallas_call(
        _regroup_kernel,
        out_shape=jax.ShapeDtypeStruct((depth, d, Z_WIDTH), jnp.bfloat16),
        grid=(depth, d // tm),
        in_specs=[pl.BlockSpec((1, tm, n), lambda l, i: (l, i, 0))],
        out_specs=pl.BlockSpec((1, tm, Z_WIDTH), lambda l, i: (l, i, 0)),
        compiler_params=_cparams(("parallel", "parallel")),
        name="regroup",
    )(w)


def _pair_layout(a):
    lead = a.shape[:-1]
    a = a.reshape(lead + (MLA_HEADS, MLA_QK))
    nope, x1, x2 = a[..., :MLA_NOPE], a[..., MLA_NOPE:MLA_NOPE + 32], a[..., MLA_NOPE + 32:]
    return jnp.concatenate([nope, x1, x2, x2, x1], axis=-1).reshape(lead + (MLA_HEADS * 256,))


_B_C32, _B_S32, _B_C16, _B_S16, _B_C8, _B_S8, _B_ONE, _B_ROWS = 0, 32, 64, 80, 96, 104, 112, 128


def _table_selector():
    sel = np.zeros((_B_ROWS, N_TABLES * LANES), np.float32)

    def put(table, lane, base, n, coef=1.0):
        for j in range(n):
            sel[base + (j if base != _B_ONE else 0), table * LANES + lane + j] = coef

    c32, s32, c16, s16, c8, s8, one = _B_C32, _B_S32, _B_C16, _B_S16, _B_C8, _B_S8, _B_ONE
    put(TB_CSQ, 0, c32, 32); put(TB_CSQ, 32, c32, 32); put(TB_CSQ, 64, s32, 32, -1.0); put(TB_CSQ, 96, s32, 32)
    put(TB_C46, 0, c32, 32); put(TB_C46, 32, c32, 32); put(TB_C46, 64, c8, 8); put(TB_C46, 72, c8, 8)
    put(TB_C46, 80, one, 48)
    put(TB_S1A, 0, s32, 32, -1.0)
    put(TB_S2A, 32, s32, 32)
    put(TB_S1B, 64, s8, 8, -1.0)
    put(TB_S2B, 72, s8, 8)
    put(TB_CD, 0, c16, 16); put(TB_CD, 16, c16, 16); put(TB_CD, 32, one, 96)
    put(TB_S1D, 0, s16, 16, -1.0)
    put(TB_S2D, 16, s16, 16)
    qs = IDX_DIM ** -0.5
    for off in (0, 64):
        put(TB_CI, off, c8, 8, qs); put(TB_CI, off + 8, c8, 8, qs); put(TB_CI, off + 16, one, 48, qs)
        put(TB_S1I, off, s8, 8, -qs)
        put(TB_S2I, off + 8, s8, 8, qs)
    return sel


def _rope_basis(positions):
    f32 = jnp.float32
    pos = positions.reshape(-1, 1).astype(f32)
    freqs = [ROPE_THETA ** (-jnp.arange(half, dtype=f32) / half) for half in (MLA_ROPE // 2, DSA_ROT // 2, IDX_ROT // 2)]
    inv = jnp.concatenate([freqs[0], freqs[0], freqs[1], freqs[1], freqs[2], freqs[2], jnp.zeros((_B_ROWS - _B_ONE,), f32)])
    is_cos = np.zeros((_B_ROWS,), bool)
    for start, stop in ((_B_C32, _B_S32), (_B_C16, _B_S16), (_B_C8, _B_S8), (_B_ONE, _B_ONE + 1)):
        is_cos[start:stop] = True
    ang = pos * inv[None, :]
    basis = jnp.where(jnp.asarray(is_cos)[None, :], jnp.cos(ang), jnp.sin(ang))
    return basis, jnp.asarray(_table_selector(), jnp.bfloat16)


def _layer(h, p_i, tabs, batch, seq, k_top, norm_in, w_in, mla_gq, mla_w_uq, mla_gkv, mla_w_ukv, mla_qn, mla_kn,
           conv_w, dsa_qn, dsa_kn, w_out, ple_norm, ple_w_gate, ple_w_proj):
    f32, bf16 = jnp.float32, jnp.bfloat16
    t = h.shape[0]
    tm_big = min(1024, t)

    z = _inproj(h, norm_in[None, :], w_in, tm_big, Z_WIDTH // 4)

    gq = _pair_layout(jnp.tile(mla_qn, MLA_HEADS))[None, :] * (MLA_QK ** -0.5 * LOG2E)
    g46 = jnp.concatenate([mla_kn[MLA_NOPE:], jnp.ones((64,), f32)])[None, :]
    qm, km, vm, qd, kd, iqp, ikd = _prep(
        z, *tabs, mla_gq[None, :], mla_gkv[None, :], _pair_layout(mla_w_uq).astype(bf16), mla_w_ukv.astype(bf16),
        gq, mla_kn[None, :MLA_NOPE], g46, dsa_qn[None, :] * (DSA_HEAD_DIM ** -0.5 * LOG2E), dsa_kn[None, :], 512)

    y_b = _conv(z, jnp.pad(conv_w, ((0, 8 - CONV_K), (0, 0))), seq, min(2048, seq))
    y_a = _mla(qm, km, vm, z, batch, seq, 512)
    y_c = _dsa(iqp, ikd, qd, kd, z, batch, seq, 512, k_top)

    wo = w_out.astype(bf16)
    return _post(h, y_a, y_b, y_c, p_i, ple_norm[None, :],
                 wo[:MLA_WIDTH], wo[MLA_WIDTH:MLA_WIDTH + CONV_CH], wo[MLA_WIDTH + CONV_CH:],
                 ple_w_gate.astype(bf16), ple_w_proj.astype(bf16), min(512, t))


def kernel(x, p, positions, norm_in, w_in, mla_gq, mla_w_uq, mla_gkv, mla_w_ukv, mla_qn, mla_kn, conv_w, dsa_qn,
           dsa_kn, w_out, ple_norm, ple_w_gate, ple_w_proj):
    batch, seq, d = x.shape
    depth = p.shape[0]
    t = batch * seq
    k_top = min(TOPK_MAX, seq // 4)
    tabs = _rope_basis(positions)
    w_in_z = _regroup_w_in(w_in)
    h = x.reshape(t, d)
    for i in range(depth):
        h = _layer(h, p[i].reshape(t, PLE_DIM), tabs, batch, seq, k_top, norm_in[i], w_in_z[i], mla_gq[i],
                   mla_w_uq[i], mla_gkv[i], mla_w_ukv[i], mla_qn[i], mla_kn[i], conv_w[i], dsa_qn[i], dsa_kn[i],
                   w_out[i], ple_norm[i], ple_w_gate[i], ple_w_proj[i])
    return h.reshape(batch, seq, d)
```

```python
import functools

import numpy as np
import jax
import jax.numpy as jnp
from jax import lax
from jax.experimental import pallas as pl
from jax.experimental.pallas import tpu as pltpu

D_MODEL = 2048
PLE_DIM = 256
ROPE_THETA = 500000.0
EPS = 1e-6

MLA_HEADS = 6
MLA_Q_LORA = 512
MLA_KV_LORA = 512
MLA_NOPE = 128
MLA_ROPE = 64
MLA_V = 128
MLA_QK = MLA_NOPE + MLA_ROPE
MLA_WIDTH = MLA_HEADS * MLA_V

CONV_CH = 512
CONV_K = 3

DSA_HEADS = 6
DSA_KV_HEADS = 2
DSA_GROUP = DSA_HEADS // DSA_KV_HEADS
DSA_HEAD_DIM = 128
DSA_WIDTH = DSA_HEADS * DSA_HEAD_DIM
DSA_ROT = DSA_HEAD_DIM // 4
IDX_HEADS = 16
IDX_DIM = 64
IDX_ROT = IDX_DIM // 4
TOPK_MAX = 256

LANES = 128

_O_CQ, _O_CKV, _O_KPE, _O_MLAZ = 0, 512, 1024, 1088
_O_CX, _O_CB, _O_CC, _O_CONVZ = 1856, 2368, 2880, 3392
_O_DQ, _O_DK, _O_DV, _O_DSAZ = 3904, 4672, 4928, 5184
_O_IQ, _O_IW, _O_IK, _N_IN = 5952, 6976, 6992, 7056

Z_CQ, Z_CKV = 0, 512
Z_CX, Z_CB, Z_CC, Z_CONVZ = 1024, 1536, 2048, 2560
Z_MLAZ, Z_DQ, Z_DSAZ = 3072, 3840, 4608
Z_DK, Z_DV = 5376, 5632
Z_KPE_IK, Z_IW, Z_IQ = 5888, 6016, 6144
Z_WIDTH = 7168

LOG2E = 1.4426950408889634
INT_MIN = -(2 ** 31)
NEG_BIG = -1e30

VMEM_LIMIT = 56 * 1024 * 1024

TM_INPROJ, TN_INPROJ = 1024, Z_WIDTH // 4
TM_REGROUP = 256
TM_PREP = 512
TM_CONV = 2048
TK_MLA = 512
TQ_DSA = 512
TM_POST = 512


def _cparams(sem):
    return pltpu.CompilerParams(dimension_semantics=sem, vmem_limit_bytes=VMEM_LIMIT)


def _dot(a, b):
    return jnp.dot(a, b, preferred_element_type=jnp.float32)


def _dot_nt(a, b):
    return lax.dot_general(a, b, (((1,), (1,)), ((), ())), preferred_element_type=jnp.float32)


def _roll(x, shift):
    return pltpu.roll(x, shift, 1)


def _inproj_kernel(x_ref, g_ref, w_ref, o_ref, a_ref):
    @pl.when(pl.program_id(1) == 0)
    def _():
        x = x_ref[...]
        r = lax.rsqrt(jnp.mean(x * x, axis=-1, keepdims=True) + EPS)
        a_ref[...] = (x * r * g_ref[...]).astype(jnp.bfloat16)

    o_ref[...] = _dot(a_ref[...], w_ref[0]).astype(o_ref.dtype)


def _inproj(h, g, w, layer, tm, tn):
    t, d = h.shape
    n = w.shape[2]
    return pl.pallas_call(
        _inproj_kernel,
        out_shape=jax.ShapeDtypeStruct((t, n), jnp.bfloat16),
        grid=(t // tm, n // tn),
        in_specs=[
            pl.BlockSpec((tm, d), lambda i, j: (i, 0)),
            pl.BlockSpec((1, d), lambda i, j: (0, 0)),
            pl.BlockSpec((1, d, tn), lambda i, j: (layer, 0, j)),
        ],
        out_specs=pl.BlockSpec((tm, tn), lambda i, j: (i, j)),
        scratch_shapes=[pltpu.VMEM((tm, d), jnp.bfloat16)],
        compiler_params=_cparams(("parallel", "arbitrary")),
        name="inproj",
    )(h, g, w)


TB_CSQ = 0
TB_C46, TB_S1A, TB_S2A, TB_S1B, TB_S2B = 1, 2, 3, 4, 5
TB_CD, TB_S1D, TB_S2D = 6, 7, 8
TB_CI, TB_S1I, TB_S2I = 9, 10, 11
N_TABLES = 12


def _prep_kernel(cq_ref, ckv_ref, t46_ref, dq_ref, dk_ref, iq_ref, basis_ref, sel_ref,
                 glq_ref, glkv_ref, wuq_ref, wukv_ref, gq_ref, gkn_ref, g46_ref, gdq_ref, gdk_ref,
                 qm_ref, km_ref, vm_ref, qd_ref, kd_ref, iqp_ref, ikd_ref):
    f32, bf16 = jnp.float32, jnp.bfloat16
    tm = cq_ref.shape[0]
    lane = lax.broadcasted_iota(jnp.int32, (tm, LANES), 1)
    low = lane < 64

    b0 = basis_ref[...]
    b_hi = b0.astype(bf16)
    b1 = b0 - b_hi.astype(f32)
    b_mid = b1.astype(bf16)
    b_lo = (b1 - b_mid.astype(f32)).astype(bf16)
    tabs = _dot(b_hi, sel_ref[...]) + _dot(b_mid, sel_ref[...]) + _dot(b_lo, sel_ref[...])
    tab = lambda k: tabs[:, LANES * k:LANES * (k + 1)]

    def rms(x, n):
        return lax.rsqrt(jnp.sum(x * x, axis=-1, keepdims=True) * (1.0 / n) + EPS)

    cq = cq_ref[...].astype(f32)
    aq = (cq * rms(cq, MLA_Q_LORA) * glq_ref[...]).astype(bf16)
    qx = _dot(aq, wuq_ref[...])
    csq = tab(TB_CSQ)
    for h in range(MLA_HEADS):
        nope = qx[:, 256 * h:256 * h + 128]
        pair = qx[:, 256 * h + 128:256 * h + 256]
        ss = jnp.sum(nope * nope, axis=-1, keepdims=True) + 0.5 * jnp.sum(pair * pair, axis=-1, keepdims=True)
        r = lax.rsqrt(ss * (1.0 / MLA_QK) + EPS)
        qm_ref[h, :, 0:128] = (nope * r * gq_ref[:, 256 * h:256 * h + 128]).astype(bf16)
        qm_ref[h, :, 128:256] = (pair * r * gq_ref[:, 256 * h + 128:256 * h + 256] * csq).astype(bf16)

    t46 = t46_ref[...].astype(f32)
    ss_pe = jnp.sum(jnp.where(low, t46 * t46, 0.0), axis=-1, keepdims=True)
    y46 = t46 * g46_ref[...]
    r46 = (y46 * tab(TB_C46)
           + _roll(y46, 96) * tab(TB_S1A) + _roll(y46, 32) * tab(TB_S2A)
           + _roll(y46, 120) * tab(TB_S1B) + _roll(y46, 8) * tab(TB_S2B))
    sw = _roll(r46, 64)
    kpe2 = jnp.where(low, r46, sw)
    ikd_ref[...] = jnp.where(low, sw, r46).astype(bf16)

    ckv = ckv_ref[...].astype(f32)
    akv = (ckv * rms(ckv, MLA_KV_LORA) * glkv_ref[...]).astype(bf16)
    kvx = _dot(akv, wukv_ref[...])
    for h in range(MLA_HEADS):
        kn = kvx[:, 256 * h:256 * h + 128]
        ss = jnp.sum(kn * kn, axis=-1, keepdims=True) + ss_pe
        r = lax.rsqrt(ss * (1.0 / MLA_QK) + EPS)
        km_ref[h, :, 0:128] = (kn * r * gkn_ref[...]).astype(bf16)
        km_ref[h, :, 128:256] = (kpe2 * r).astype(bf16)
        vm_ref[h] = kvx[:, 256 * h + 128:256 * h + 256].astype(bf16)

    cd, s1d, s2d = tab(TB_CD), tab(TB_S1D), tab(TB_S2D)

    def dsa_head(x, g):
        y = x * rms(x, DSA_HEAD_DIM) * g
        return y * cd + _roll(y, 112) * s1d + _roll(y, 16) * s2d

    for h in range(DSA_HEADS):
        x = dq_ref[:, 128 * h:128 * h + 128].astype(f32)
        qd_ref[h] = dsa_head(x, gdq_ref[...]).astype(bf16)
    for g in range(DSA_KV_HEADS):
        x = dk_ref[:, 128 * g:128 * g + 128].astype(f32)
        kd_ref[g] = dsa_head(x, gdk_ref[...]).astype(bf16)

    ci, s1i, s2i = tab(TB_CI), tab(TB_S1I), tab(TB_S2I)
    for j in range(IDX_HEADS // 2):
        x = iq_ref[:, 128 * j:128 * j + 128].astype(f32)
        y = x * ci + _roll(x, 120) * s1i + _roll(x, 8) * s2i
        iqp_ref[:, 256 * j:256 * j + 128] = jnp.where(low, y, 0.0).astype(bf16)
        iqp_ref[:, 256 * j + 128:256 * j + 256] = jnp.where(low, 0.0, y).astype(bf16)


def _prep(z, basis, sel, glq, glkv, wuq, wukv, gq, gkn, g46, gdq, gdk, tm):
    t = z.shape[0]
    bf16 = jnp.bfloat16
    row = lambda w, c: pl.BlockSpec((tm, w), lambda i, c=c: (i, c))
    full = lambda a: pl.BlockSpec(a.shape, lambda i: (0,) * a.ndim)
    heads = lambda n, w: pl.BlockSpec((n, tm, w), lambda i: (0, i, 0))
    return pl.pallas_call(
        _prep_kernel,
        out_shape=(
            jax.ShapeDtypeStruct((MLA_HEADS, t, 256), bf16),
            jax.ShapeDtypeStruct((MLA_HEADS, t, 256), bf16),
            jax.ShapeDtypeStruct((MLA_HEADS, t, 128), bf16),
            jax.ShapeDtypeStruct((DSA_HEADS, t, 128), bf16),
            jax.ShapeDtypeStruct((DSA_KV_HEADS, t, 128), bf16),
            jax.ShapeDtypeStruct((t, IDX_HEADS * 128), bf16),
            jax.ShapeDtypeStruct((t, 128), bf16),
        ),
        grid=(t // tm,),
        in_specs=[
            row(512, Z_CQ // 512), row(512, Z_CKV // 512), row(128, Z_KPE_IK // 128),
            row(768, Z_DQ // 768), row(256, Z_DK // 256), row(1024, Z_IQ // 1024),
            pl.BlockSpec((tm, LANES), lambda i: (i, 0)), full(sel),
            full(glq), full(glkv), full(wuq), full(wukv), full(gq), full(gkn), full(g46), full(gdq), full(gdk),
        ],
        out_specs=(
            heads(MLA_HEADS, 256), heads(MLA_HEADS, 256), heads(MLA_HEADS, 128),
            heads(DSA_HEADS, 128), heads(DSA_KV_HEADS, 128),
            pl.BlockSpec((tm, IDX_HEADS * 128), lambda i: (i, 0)),
            pl.BlockSpec((tm, 128), lambda i: (i, 0)),
        ),
        compiler_params=_cparams(("parallel",)),
        name="prep",
    )(z, z, z, z, z, z, basis, sel, glq, glkv, wuq, wukv, gq, gkn, g46, gdq, gdk)


def _silu(x):
    return x / (1.0 + jnp.exp(-x))


def _conv_kernel(cx_ref, cb_ref, cc_ref, cz_ref, hx_ref, hc_ref, w_ref, o_ref, *, tiles_per_seq):
    f32 = jnp.float32
    tm, ch = cx_ref.shape
    u = cc_ref[...].astype(f32) * cx_ref[...].astype(f32)
    keep = (pl.program_id(0) % tiles_per_seq != 0).astype(f32)
    hu = hc_ref[...].astype(f32) * hx_ref[...].astype(f32) * keep
    row = lax.broadcasted_iota(jnp.int32, (tm, ch), 0)
    u1 = jnp.where(row == 0, hu[7:8, :], pltpu.roll(u, 1, 0))
    u2 = jnp.where(row == 0, hu[6:7, :], jnp.where(row == 1, hu[7:8, :], pltpu.roll(u, 2, 0)))
    conv = w_ref[0:1, :] * u2 + w_ref[1:2, :] * u1 + w_ref[2:3, :] * u
    o_ref[...] = (cb_ref[...].astype(f32) * conv * _silu(cz_ref[...].astype(f32))).astype(o_ref.dtype)


def _conv(z, w, seq, tm):
    t = z.shape[0]
    col = lambda c: pl.BlockSpec((tm, CONV_CH), lambda i, c=c: (i, c))
    halo = lambda c: pl.BlockSpec((8, CONV_CH), lambda i, c=c: (jnp.maximum(i * (tm // 8) - 1, 0), c))
    return pl.pallas_call(
        functools.partial(_conv_kernel, tiles_per_seq=seq // tm),
        out_shape=jax.ShapeDtypeStruct((t, CONV_CH), jnp.bfloat16),
        grid=(t // tm,),
        in_specs=[col(Z_CX // CONV_CH), col(Z_CB // CONV_CH), col(Z_CC // CONV_CH), col(Z_CONVZ // CONV_CH),
                  halo(Z_CX // CONV_CH), halo(Z_CC // CONV_CH),
                  pl.BlockSpec((8, CONV_CH), lambda i: (0, 0))],
        out_specs=pl.BlockSpec((tm, CONV_CH), lambda i: (i, 0)),
        compiler_params=_cparams(("parallel",)),
        name="conv",
    )(z, z, z, z, z, z, w)


def _tile_lanes(x, n):
    return x if n == 1 else jnp.concatenate([x] * n, axis=1)


def _softmax_step(s, v2, m_ref, acc_ref):
    m_prev = m_ref[...]
    m_new = jnp.maximum(m_prev, jnp.max(s, axis=-1, keepdims=True))
    p = jnp.exp2(s - _tile_lanes(m_new, s.shape[1] // LANES))
    alpha = jnp.exp2(m_prev - m_new)
    acc_ref[...] = _tile_lanes(alpha, 2) * acc_ref[...] + _dot(p.astype(jnp.bfloat16), v2)
    m_ref[...] = m_new


def _with_ones(v):
    return jnp.concatenate([v, jnp.ones_like(v)], axis=1)


MLA_HEADS_PER_STEP = 6


def _mla_kernel(q_ref, k_ref, v_ref, zg_ref, o_ref, m_ref, acc_ref, *, tk):
    f32 = jnp.float32
    qi = pl.program_id(2)
    m_ref[...] = jnp.full(m_ref.shape, NEG_BIG, f32)
    acc_ref[...] = jnp.zeros(acc_ref.shape, f32)

    def step(ki, row0, masked):
        keys = pl.ds(pl.multiple_of(ki * tk, tk), tk)
        rows = pl.ds(row0, q_ref.shape[1] - row0)
        for h in range(MLA_HEADS_PER_STEP):
            s = _dot_nt(q_ref[h, rows, :], k_ref[h, keys, :])
            if masked:
                r = lax.broadcasted_iota(jnp.int32, s.shape, 0)
                c = lax.broadcasted_iota(jnp.int32, s.shape, 1)
                s = jnp.where(c <= r, s, 2.0 * NEG_BIG)
            _softmax_step(s, _with_ones(v_ref[h, keys, :]), m_ref.at[h, rows, :], acc_ref.at[h, rows, :])

    def body(ki, c):
        step(ki, 0, False)
        return c

    lax.fori_loop(0, 2 * qi, body, 0)
    step(2 * qi, 0, True)
    step(2 * qi + 1, tk, True)
    for h in range(MLA_HEADS_PER_STEP):
        o = acc_ref[h, :, 0:128] / acc_ref[h, :, 128:256]
        zg = zg_ref[:, 128 * h:128 * h + 128].astype(f32)
        o_ref[:, 128 * h:128 * h + 128] = (o * _silu(zg)).astype(o_ref.dtype)


def _mla(qm, km, vm, z, batch, seq, tk):
    t = z.shape[0]
    tq = 2 * tk
    nq = seq // tq
    hs = MLA_HEADS_PER_STEP
    resident = dict(pipeline_mode=pl.Buffered(1))
    return pl.pallas_call(
        functools.partial(_mla_kernel, tk=tk),
        out_shape=jax.ShapeDtypeStruct((t, MLA_WIDTH), jnp.bfloat16),
        grid=(batch, MLA_HEADS // hs, nq),
        in_specs=[
            pl.BlockSpec((hs, tq, 256), lambda b, h, i: (h, b * nq + i, 0)),
            pl.BlockSpec((hs, seq, 256), lambda b, h, i: (h, b, 0), **resident),
            pl.BlockSpec((hs, seq, 128), lambda b, h, i: (h, b, 0), **resident),
            pl.BlockSpec((tq, 128 * hs), lambda b, h, i: (b * nq + i, Z_MLAZ // (128 * hs) + h)),
        ],
        out_specs=pl.BlockSpec((tq, 128 * hs), lambda b, h, i: (b * nq + i, h)),
        scratch_shapes=[pltpu.VMEM((hs, tq, LANES), jnp.float32), pltpu.VMEM((hs, tq, 256), jnp.float32)],
        compiler_params=_cparams(("parallel", "parallel", "arbitrary")),
        name="mla",
    )(qm, km, vm, z)


KC = 256
ATT_KC = 512
COUNT_FANIN = 8
SCORE_ROWS = 256


def _dsa_kernel(iq_ref, iw_ref, ik_ref, q_ref, k_ref, v0_ref, v1_ref, zg_ref, expand_ref, o_ref,
                key_ref, hi_ref, lo_ref, wb_ref, stage_ref, m_ref, acc_ref, *, k_top):
    f32, i32, bf16 = jnp.float32, jnp.int32, jnp.bfloat16
    tq = iq_ref.shape[0]
    i = pl.program_id(1)
    n_chunks = ((i + 1) * tq + KC - 1) // KC
    q_pos = i * tq + lax.broadcasted_iota(i32, (tq, KC), 0)
    lane_kc = lax.broadcasted_iota(i32, (tq, KC), 1)

    w = (iw_ref[...].astype(f32) * (IDX_HEADS ** -0.5)).astype(bf16)
    wb_ref[...] = _dot(w, expand_ref[...])

    def emit_keys(c):
        start = pl.multiple_of(c * KC, KC)
        bits = lax.bitcast_convert_type(stage_ref[...], i32)
        key = jnp.where(bits < 0, bits ^ jnp.int32(0x7FFFFFFF), bits)
        key = jnp.where(start + lane_kc <= q_pos, key, jnp.int32(INT_MIN))
        key_ref[:, pl.ds(start, KC)] = key
        key_t = key.T
        hi_ref[pl.ds(start, KC), :] = (key_t >> 16).astype(jnp.int16)
        lo_ref[pl.ds(start, KC), :] = ((key_t & 0xFFFF) - 32768).astype(jnp.int16)

    def score_chunk(c, carry):
        emit_keys(jnp.maximum(c - 1, 0))
        ikc = ik_ref[pl.ds(pl.multiple_of(c * KC, KC), KC), :]
        for r0 in range(0, tq, SCORE_ROWS):
            rows = pl.ds(r0, SCORE_ROWS)
            acc = jnp.zeros((SCORE_ROWS, KC), f32)
            for h in range(IDX_HEADS):
                lg = _dot_nt(iq_ref[rows, 128 * h:128 * h + 128], ikc)
                acc = acc + _tile_lanes(wb_ref[rows, 128 * h:128 * h + 128], KC // LANES) * jnp.maximum(lg, 0.0)
            stage_ref[rows, :] = acc
        return carry

    stage_ref[...] = jnp.zeros((tq, KC), f32)
    lax.fori_loop(0, n_chunks, score_chunk, 0)
    emit_keys(n_chunks - 1)

    n_att = (n_chunks * KC + ATT_KC - 1) // ATT_KC

    def fill(c, carry):
        start = pl.multiple_of(c * KC, KC)
        key_ref[:, pl.ds(start, KC)] = jnp.full((tq, KC), INT_MIN, i32)
        hi_ref[pl.ds(start, KC), :] = jnp.full((KC, tq), -32768, jnp.int16)
        lo_ref[pl.ds(start, KC), :] = jnp.full((KC, tq), -32768, jnp.int16)
        return carry

    lax.fori_loop(n_chunks, n_att * (ATT_KC // KC), fill, 0)

    i16 = jnp.int16
    low16 = jnp.full((), -32768, i16)
    one16, zero16 = jnp.ones((), i16), jnp.zeros((), i16)

    def count16(ref, cand):
        c16 = cand.astype(i16)
        def body(c, cnt):
            start = pl.multiple_of(c * ATT_KC, ATT_KC)
            for g in range(0, ATT_KC, 16 * COUNT_FANIN):
                hit = jnp.where(ref[pl.ds(start + g, 16 * COUNT_FANIN), :] >= c16, one16, zero16)
                part = [hit[16 * j:16 * (j + 1)] for j in range(COUNT_FANIN)]
                while len(part) > 1:
                    part = [a + b for a, b in zip(part[0::2], part[1::2])]
                cnt = cnt + part[0]
            return cnt
        cnt = lax.fori_loop(0, n_att, body, jnp.zeros((16, tq), i16))
        return jnp.sum(cnt.astype(i32), axis=0, keepdims=True)

    def search16(ref, want):
        def step(it, carry):
            r, n_r = carry
            cand = r + lax.shift_left(jnp.int32(1), 15 - it)
            n = count16(ref, cand)
            ok = n >= want
            return jnp.where(ok, cand, r), jnp.where(ok, n, n_r)
        return lax.fori_loop(0, 16, step, (jnp.full((1, tq), -32768, i32), jnp.zeros((1, tq), i32)))

    t_hi, n_hi = search16(hi_ref, k_top)
    n_above = count16(hi_ref, t_hi + 1)
    t_hi16 = t_hi.astype(i16)

    def restrict(c, carry):
        rows = pl.ds(pl.multiple_of(c * ATT_KC, ATT_KC), ATT_KC)
        lo_ref[rows, :] = jnp.where(hi_ref[rows, :] == t_hi16, lo_ref[rows, :], low16)
        return carry

    lax.fori_loop(0, n_att, restrict, 0)
    t_lo, n_lo = search16(lo_ref, k_top - n_above)
    thr = t_hi * 65536 + (t_lo + 32768)

    n_ge = jnp.where(t_lo > -32768, n_above + n_lo, jnp.where(t_hi > -32768, n_hi, n_above))
    has_excess = jnp.max(n_ge) > k_top

    @pl.when(has_excess)
    def _():
        def count_where(hits):
            def body(c, cnt):
                start = pl.multiple_of(c * KC, KC)
                hit = hits(key_ref[:, pl.ds(start, KC)].T, start)
                return cnt + jnp.sum(hit.reshape(KC // 8, 8, tq), axis=0)
            cnt = lax.fori_loop(0, n_chunks, body, jnp.zeros((8, tq), i32))
            return jnp.sum(cnt, axis=0, keepdims=True)

        need = k_top - count_where(lambda kt, _: jnp.where(kt > thr, 1, 0))
        sub = lax.broadcasted_iota(i32, (KC, tq), 0)

        def idx_step(it, bound):
            cand = bound + lax.shift_left(jnp.int32(1), 30 - it)
            n_below = count_where(lambda kt, start: jnp.where(kt == thr, jnp.where(start + sub < cand, 1, 0), 0))
            return jnp.where(n_below <= need, cand, bound)

        bound = lax.fori_loop(0, 31, idx_step, jnp.zeros((1, tq), i32))

        def demote(c, carry):
            start = pl.multiple_of(c * KC, KC)
            kt = key_ref[:, pl.ds(start, KC)].T
            drop = jnp.where(kt == thr, jnp.where(start + sub >= bound, 1, 0), 0)
            key_ref[:, pl.ds(start, KC)] = jnp.where(drop == 1, jnp.int32(INT_MIN), kt).T
            return carry

        lax.fori_loop(0, n_chunks, demote, 0)

    thr_sel = jnp.maximum(thr, jnp.int32(INT_MIN + 1))
    thr_kc = _tile_lanes(jnp.broadcast_to(thr_sel, (LANES, tq)).T, KC // LANES)

    m_ref[...] = jnp.full(m_ref.shape, NEG_BIG, f32)
    acc_ref[...] = jnp.zeros(acc_ref.shape, f32)
    v_refs = (v0_ref, v1_ref)

    thr_att = _tile_lanes(thr_kc, ATT_KC // KC)

    def attn_chunk(c, carry):
        start = pl.multiple_of(c * ATT_KC, ATT_KC)
        bias = jnp.where(key_ref[:, pl.ds(start, ATT_KC)] >= thr_att, 0.0, 2.0 * NEG_BIG)
        for hd in range(DSA_HEADS):
            g = hd // DSA_GROUP
            s = _dot_nt(q_ref[hd], k_ref[g, pl.ds(start, ATT_KC), :]) + bias
            _softmax_step(s, _with_ones(v_refs[g][pl.ds(start, ATT_KC), :]), m_ref.at[hd], acc_ref.at[hd])
        return carry

    lax.fori_loop(0, n_att, attn_chunk, 0)

    for hd in range(DSA_HEADS):
        o = acc_ref[hd, :, 0:128] / acc_ref[hd, :, 128:256]
        zg = zg_ref[:, 128 * hd:128 * hd + 128].astype(f32)
        o_ref[:, 128 * hd:128 * hd + 128] = (o * _silu(zg)).astype(o_ref.dtype)


def _dsa(iqp, ikd, qd, kd, z, batch, seq, tq, k_top):
    t = z.shape[0]
    nq = seq // tq
    f32 = jnp.float32
    expand = np.zeros((LANES, IDX_HEADS * LANES), np.float32)
    for h in range(IDX_HEADS):
        expand[h, LANES * h:LANES * (h + 1)] = 1.0
    expand = jnp.asarray(expand, jnp.bfloat16)
    return pl.pallas_call(
        functools.partial(_dsa_kernel, k_top=k_top),
        out_shape=jax.ShapeDtypeStruct((t, DSA_WIDTH), jnp.bfloat16),
        grid=(batch, nq),
        in_specs=[
            pl.BlockSpec((tq, IDX_HEADS * 128), lambda b, i: (b * nq + i, 0)),
            pl.BlockSpec((tq, 128), lambda b, i: (b * nq + i, Z_IW // 128)),
            pl.BlockSpec((seq, 128), lambda b, i: (b, 0)),
            pl.BlockSpec((DSA_HEADS, tq, 128), lambda b, i: (0, b * nq + i, 0)),
            pl.BlockSpec((DSA_KV_HEADS, seq, 128), lambda b, i: (0, b, 0)),
            pl.BlockSpec((seq, 128), lambda b, i: (b, Z_DV // 128)),
            pl.BlockSpec((seq, 128), lambda b, i: (b, Z_DV // 128 + 1)),
            pl.BlockSpec((tq, DSA_WIDTH), lambda b, i: (b * nq + i, Z_DSAZ // DSA_WIDTH)),
            pl.BlockSpec(expand.shape, lambda b, i: (0, 0)),
        ],
        out_specs=pl.BlockSpec((tq, DSA_WIDTH), lambda b, i: (b * nq + i, 0)),
        scratch_shapes=[
            pltpu.VMEM((tq, seq), jnp.int32),
            pltpu.VMEM((seq, tq), jnp.int16),
            pltpu.VMEM((seq, tq), jnp.int16),
            pltpu.VMEM((tq, IDX_HEADS * LANES), f32),
            pltpu.VMEM((tq, KC), f32),
            pltpu.VMEM((DSA_HEADS, tq, LANES), f32),
            pltpu.VMEM((DSA_HEADS, tq, 2 * DSA_HEAD_DIM), f32),
        ],
        compiler_params=_cparams(("parallel", "arbitrary")),
        name="dsa",
    )(iqp, z, ikd, qd, kd, z, z, z, expand)


def _post_kernel(h_ref, ya_ref, yb_ref, yc_ref, p_ref, g_ref, wa_ref, wb_ref, wc_ref, wg_ref, wp_ref, o_ref):
    h1 = h_ref[...] + _dot(ya_ref[...], wa_ref[...])
    h1 = h1 + _dot(yb_ref[...], wb_ref[...])
    h1 = h1 + _dot(yc_ref[...], wc_ref[...])
    r = lax.rsqrt(jnp.mean(h1 * h1, axis=-1, keepdims=True) + EPS)
    a = (h1 * r * g_ref[...]).astype(jnp.bfloat16)
    gate = _dot(a, wg_ref[...])
    gate = 1.0 / (1.0 + jnp.exp(-gate))
    o_ref[...] = h1 + gate * _dot(p_ref[0].astype(jnp.bfloat16), wp_ref[...])


def _post(h, ya, yb, yc, p, layer, g, wa, wb, wc, wg, wp, tm):
    t, d = h.shape
    rows = lambda a: pl.BlockSpec((tm, a.shape[1]), lambda i: (i, 0))
    whole = lambda a: pl.BlockSpec(a.shape, lambda i: (0, 0), pipeline_mode=pl.Buffered(1))
    return pl.pallas_call(
        _post_kernel,
        out_shape=jax.ShapeDtypeStruct((t, d), jnp.float32),
        grid=(t // tm,),
        in_specs=[rows(h), rows(ya), rows(yb), rows(yc),
                  pl.BlockSpec((1, tm, p.shape[2]), lambda i: (layer, i, 0)), whole(g),
                  whole(wa), whole(wb), whole(wc), whole(wg), whole(wp)],
        out_specs=rows(h),
        compiler_params=_cparams(("parallel",)),
        name="post",
    )(h, ya, yb, yc, p, g, wa, wb, wc, wg, wp)


_REGROUP = (
    (Z_CQ, _O_CQ, 512), (Z_CKV, _O_CKV, 512),
    (Z_CX, _O_CX, 512), (Z_CB, _O_CB, 512), (Z_CC, _O_CC, 512), (Z_CONVZ, _O_CONVZ, 512),
    (Z_MLAZ, _O_MLAZ, 768), (Z_DQ, _O_DQ, 768), (Z_DSAZ, _O_DSAZ, 768),
    (Z_DK, _O_DK, 256), (Z_DV, _O_DV, 256), (Z_IQ, _O_IQ, 1024),
)


def _regroup_kernel(w_ref, o_ref):
    bf16 = jnp.bfloat16
    for dst, src, n in _REGROUP:
        o_ref[0, :, dst:dst + n] = w_ref[0, :, src:src + n].astype(bf16)
    rows = w_ref.shape[1]
    o_ref[0, :, Z_KPE_IK:Z_KPE_IK + 128] = jnp.concatenate(
        [w_ref[0, :, _O_KPE:_O_KPE + MLA_ROPE], w_ref[0, :, _O_IK:_O_IK + IDX_DIM]], axis=1).astype(bf16)
    o_ref[0, :, Z_IW:Z_IW + 128] = jnp.concatenate(
        [w_ref[0, :, _O_IW:_O_IW + IDX_HEADS], jnp.zeros((rows, 128 - IDX_HEADS), jnp.float32)], axis=1).astype(bf16)


def _regroup_w_in(w, tm=TM_REGROUP):
    depth, d, n = w.shape
    return pl.pallas_call(
        _regroup_kernel,
        out_shape=jax.ShapeDtypeStruct((depth, d, Z_WIDTH), jnp.bfloat16),
        grid=(depth, d // tm),
        in_specs=[pl.BlockSpec((1, tm, n), lambda l, i: (l, i, 0))],
        out_specs=pl.BlockSpec((1, tm, Z_WIDTH), lambda l, i: (l, i, 0)),
        compiler_params=_cparams(("parallel", "parallel")),
        name="regroup",
    )(w)


def _pair_layout(a):
    lead = a.shape[:-1]
    a = a.reshape(lead + (MLA_HEADS, MLA_QK))
    nope, x1, x2 = a[..., :MLA_NOPE], a[..., MLA_NOPE:MLA_NOPE + 32], a[..., MLA_NOPE + 32:]
    return jnp.concatenate([nope, x1, x2, x2, x1], axis=-1).reshape(lead + (MLA_HEADS * 256,))


_B_C32, _B_S32, _B_C16, _B_S16, _B_C8, _B_S8, _B_ONE, _B_ROWS = 0, 32, 64, 80, 96, 104, 112, 128


def _table_selector():
    sel = np.zeros((_B_ROWS, N_TABLES * LANES), np.float32)

    def put(table, lane, base, n, coef=1.0):
        for j in range(n):
            sel[base + (j if base != _B_ONE else 0), table * LANES + lane + j] = coef

    c32, s32, c16, s16, c8, s8, one = _B_C32, _B_S32, _B_C16, _B_S16, _B_C8, _B_S8, _B_ONE
    put(TB_CSQ, 0, c32, 32); put(TB_CSQ, 32, c32, 32); put(TB_CSQ, 64, s32, 32, -1.0); put(TB_CSQ, 96, s32, 32)
    put(TB_C46, 0, c32, 32); put(TB_C46, 32, c32, 32); put(TB_C46, 64, c8, 8); put(TB_C46, 72, c8, 8)
    put(TB_C46, 80, one, 48)
    put(TB_S1A, 0, s32, 32, -1.0)
    put(TB_S2A, 32, s32, 32)
    put(TB_S1B, 64, s8, 8, -1.0)
    put(TB_S2B, 72, s8, 8)
    put(TB_CD, 0, c16, 16); put(TB_CD, 16, c16, 16); put(TB_CD, 32, one, 96)
    put(TB_S1D, 0, s16, 16, -1.0)
    put(TB_S2D, 16, s16, 16)
    qs = IDX_DIM ** -0.5
    for off in (0, 64):
        put(TB_CI, off, c8, 8, qs); put(TB_CI, off + 8, c8, 8, qs); put(TB_CI, off + 16, one, 48, qs)
        put(TB_S1I, off, s8, 8, -qs)
        put(TB_S2I, off + 8, s8, 8, qs)
    return sel


def _rope_basis(positions):
    f32 = jnp.float32
    pos = positions.reshape(-1, 1).astype(f32)
    freqs = [ROPE_THETA ** (-jnp.arange(half, dtype=f32) / half) for half in (MLA_ROPE // 2, DSA_ROT // 2, IDX_ROT // 2)]
    inv = jnp.concatenate([freqs[0], freqs[0], freqs[1], freqs[1], freqs[2], freqs[2], jnp.zeros((_B_ROWS - _B_ONE,), f32)])
    is_cos = np.zeros((_B_ROWS,), bool)
    for start, stop in ((_B_C32, _B_S32), (_B_C16, _B_S16), (_B_C8, _B_S8), (_B_ONE, _B_ONE + 1)):
        is_cos[start:stop] = True
    ang = pos * inv[None, :]
    basis = jnp.where(jnp.asarray(is_cos)[None, :], jnp.cos(ang), jnp.sin(ang))
    return basis, jnp.asarray(_table_selector(), jnp.bfloat16)


def _layer(h, layer, p_all, w_in_all, tabs, batch, seq, k_top, norm_in, mla_gq, mla_w_uq, mla_gkv, mla_w_ukv, mla_qn,
           mla_kn, conv_w, dsa_qn, dsa_kn, w_out, ple_norm, ple_w_gate, ple_w_proj):
    f32, bf16 = jnp.float32, jnp.bfloat16
    t = h.shape[0]

    z = _inproj(h, norm_in[None, :], w_in_all, layer, min(TM_INPROJ, t), TN_INPROJ)

    gq = _pair_layout(jnp.tile(mla_qn, MLA_HEADS))[None, :] * (MLA_QK ** -0.5 * LOG2E)
    g46 = jnp.concatenate([mla_kn[MLA_NOPE:], jnp.ones((64,), f32)])[None, :]
    qm, km, vm, qd, kd, iqp, ikd = _prep(
        z, *tabs, mla_gq[None, :], mla_gkv[None, :], _pair_layout(mla_w_uq).astype(bf16), mla_w_ukv.astype(bf16),
        gq, mla_kn[None, :MLA_NOPE], g46, dsa_qn[None, :] * (DSA_HEAD_DIM ** -0.5 * LOG2E), dsa_kn[None, :],
        min(TM_PREP, t))

    y_b = _conv(z, jnp.pad(conv_w, ((0, 8 - CONV_K), (0, 0))), seq, min(TM_CONV, seq))
    y_a = _mla(qm, km, vm, z, batch, seq, TK_MLA)
    y_c = _dsa(iqp, ikd, qd, kd, z, batch, seq, TQ_DSA, k_top)

    wo = w_out.astype(bf16)
    return _post(h, y_a, y_b, y_c, p_all, layer, ple_norm[None, :],
                 wo[:MLA_WIDTH], wo[MLA_WIDTH:MLA_WIDTH + CONV_CH], wo[MLA_WIDTH + CONV_CH:],
                 ple_w_gate.astype(bf16), ple_w_proj.astype(bf16), min(TM_POST, t))


def kernel(x, p, positions, norm_in, w_in, mla_gq, mla_w_uq, mla_gkv, mla_w_ukv, mla_qn, mla_kn, conv_w, dsa_qn,
           dsa_kn, w_out, ple_norm, ple_w_gate, ple_w_proj):
    batch, seq, d = x.shape
    depth = p.shape[0]
    t = batch * seq
    k_top = min(TOPK_MAX, seq // 4)
    tabs = _rope_basis(positions)
    w_in_z = _regroup_w_in(w_in)
    h = x.reshape(t, d)
    p_all = p.reshape(depth, t, PLE_DIM)
    for i in range(depth):
        h = _layer(h, i, p_all, w_in_z, tabs, batch, seq, k_top, norm_in[i], mla_gq[i],
                   mla_w_uq[i], mla_gkv[i], mla_w_ukv[i], mla_qn[i], mla_kn[i], conv_w[i], dsa_qn[i], dsa_kn[i],
                   w_out[i], ple_norm[i], ple_w_gate[i], ple_w_proj[i])
    return h.reshape(batch, seq, d)
```

```python
import functools

import numpy as np
import jax
import jax.numpy as jnp
from jax import lax
from jax.experimental import pallas as pl
from jax.experimental.pallas import tpu as pltpu

D_MODEL = 2048
PLE_DIM = 256
ROPE_THETA = 500000.0
EPS = 1e-6

MLA_HEADS = 6
MLA_Q_LORA = 512
MLA_KV_LORA = 512
MLA_NOPE = 128
MLA_ROPE = 64
MLA_V = 128
MLA_QK = MLA_NOPE + MLA_ROPE
MLA_WIDTH = MLA_HEADS * MLA_V

CONV_CH = 512
CONV_K = 3

DSA_HEADS = 6
DSA_KV_HEADS = 2
DSA_GROUP = DSA_HEADS // DSA_KV_HEADS
DSA_HEAD_DIM = 128
DSA_WIDTH = DSA_HEADS * DSA_HEAD_DIM
DSA_ROT = DSA_HEAD_DIM // 4
IDX_HEADS = 16
IDX_DIM = 64
IDX_ROT = IDX_DIM // 4
TOPK_MAX = 256

LANES = 128

_O_CQ, _O_CKV, _O_KPE, _O_MLAZ = 0, 512, 1024, 1088
_O_CX, _O_CB, _O_CC, _O_CONVZ = 1856, 2368, 2880, 3392
_O_DQ, _O_DK, _O_DV, _O_DSAZ = 3904, 4672, 4928, 5184
_O_IQ, _O_IW, _O_IK, _N_IN = 5952, 6976, 6992, 7056

Z_CQ, Z_CKV = 0, 512
Z_CX, Z_CB, Z_CC, Z_CONVZ = 1024, 1536, 2048, 2560
Z_MLAZ, Z_DQ, Z_DSAZ = 3072, 3840, 4608
Z_DK, Z_DV = 5376, 5632
Z_KPE_IK, Z_IW, Z_IQ = 5888, 6016, 6144
Z_WIDTH = 7168

LOG2E = 1.4426950408889634
INT_MIN = -(2 ** 31)
NEG_BIG = -1e30

VMEM_LIMIT = 56 * 1024 * 1024

TM_INPROJ, TN_INPROJ = 1024, Z_WIDTH // 4
TC_REGROUP = 256
TM_PREP = 512
TM_CONV = 2048
TK_MLA = 512
TQ_DSA = 512
TM_POST = 512


def _cparams(sem):
    return pltpu.CompilerParams(dimension_semantics=sem, vmem_limit_bytes=VMEM_LIMIT)


def _dot(a, b):
    return jnp.dot(a, b, preferred_element_type=jnp.float32)


def _dot_nt(a, b):
    return lax.dot_general(a, b, (((1,), (1,)), ((), ())), preferred_element_type=jnp.float32)


def _roll(x, shift):
    return pltpu.roll(x, shift, 1)


def _inproj_kernel(x_ref, g_ref, w_ref, o_ref, a_ref):
    @pl.when(pl.program_id(1) == 0)
    def _():
        x = x_ref[...]
        r = lax.rsqrt(jnp.mean(x * x, axis=-1, keepdims=True) + EPS)
        a_ref[...] = (x * r * g_ref[...]).astype(jnp.bfloat16)

    o_ref[...] = _dot_nt(a_ref[...], w_ref[0]).astype(o_ref.dtype)


def _inproj(h, g, w, layer, tm, tn):
    t, d = h.shape
    n = w.shape[1]
    return pl.pallas_call(
        _inproj_kernel,
        out_shape=jax.ShapeDtypeStruct((t, n), jnp.bfloat16),
        grid=(t // tm, n // tn),
        in_specs=[
            pl.BlockSpec((tm, d), lambda i, j: (i, 0)),
            pl.BlockSpec((1, d), lambda i, j: (0, 0)),
            pl.BlockSpec((1, tn, d), lambda i, j: (layer, j, 0)),
        ],
        out_specs=pl.BlockSpec((tm, tn), lambda i, j: (i, j)),
        scratch_shapes=[pltpu.VMEM((tm, d), jnp.bfloat16)],
        compiler_params=_cparams(("parallel", "arbitrary")),
        name="inproj",
    )(h, g, w)


TB_CSQ = 0
TB_C46, TB_S1A, TB_S2A, TB_S1B, TB_S2B = 1, 2, 3, 4, 5
TB_CD, TB_S1D, TB_S2D = 6, 7, 8
TB_CI, TB_S1I, TB_S2I = 9, 10, 11
N_TABLES = 12


def _prep_kernel(cq_ref, ckv_ref, t46_ref, dq_ref, dk_ref, iq_ref, basis_ref, sel_ref,
                 glq_ref, glkv_ref, wuq_ref, wukv_ref, gq_ref, gkn_ref, g46_ref, gdq_ref, gdk_ref,
                 qm_ref, km_ref, vm_ref, qd_ref, kd_ref, iqp_ref, ikd_ref):
    f32, bf16 = jnp.float32, jnp.bfloat16
    tm = cq_ref.shape[0]
    lane = lax.broadcasted_iota(jnp.int32, (tm, LANES), 1)
    low = lane < 64

    b0 = basis_ref[...]
    b_hi = b0.astype(bf16)
    b1 = b0 - b_hi.astype(f32)
    b_mid = b1.astype(bf16)
    b_lo = (b1 - b_mid.astype(f32)).astype(bf16)
    tabs = _dot(b_hi, sel_ref[...]) + _dot(b_mid, sel_ref[...]) + _dot(b_lo, sel_ref[...])
    tab = lambda k: tabs[:, LANES * k:LANES * (k + 1)]

    def rms(x, n):
        return lax.rsqrt(jnp.sum(x * x, axis=-1, keepdims=True) * (1.0 / n) + EPS)

    cq = cq_ref[...].astype(f32)
    aq = (cq * rms(cq, MLA_Q_LORA) * glq_ref[...]).astype(bf16)
    qx = _dot(aq, wuq_ref[...])
    csq = tab(TB_CSQ)
    for h in range(MLA_HEADS):
        nope = qx[:, 256 * h:256 * h + 128]
        pair = qx[:, 256 * h + 128:256 * h + 256]
        ss = jnp.sum(nope * nope, axis=-1, keepdims=True) + 0.5 * jnp.sum(pair * pair, axis=-1, keepdims=True)
        r = lax.rsqrt(ss * (1.0 / MLA_QK) + EPS)
        qm_ref[h, :, 0:128] = (nope * r * gq_ref[:, 256 * h:256 * h + 128]).astype(bf16)
        qm_ref[h, :, 128:256] = (pair * r * gq_ref[:, 256 * h + 128:256 * h + 256] * csq).astype(bf16)

    t46 = t46_ref[...].astype(f32)
    ss_pe = jnp.sum(jnp.where(low, t46 * t46, 0.0), axis=-1, keepdims=True)
    y46 = t46 * g46_ref[...]
    r46 = (y46 * tab(TB_C46)
           + _roll(y46, 96) * tab(TB_S1A) + _roll(y46, 32) * tab(TB_S2A)
           + _roll(y46, 120) * tab(TB_S1B) + _roll(y46, 8) * tab(TB_S2B))
    sw = _roll(r46, 64)
    kpe2 = jnp.where(low, r46, sw)
    ikd_ref[...] = jnp.where(low, sw, r46).astype(bf16)

    ckv = ckv_ref[...].astype(f32)
    akv = (ckv * rms(ckv, MLA_KV_LORA) * glkv_ref[...]).astype(bf16)
    kvx = _dot(akv, wukv_ref[...])
    for h in range(MLA_HEADS):
        kn = kvx[:, 256 * h:256 * h + 128]
        ss = jnp.sum(kn * kn, axis=-1, keepdims=True) + ss_pe
        r = lax.rsqrt(ss * (1.0 / MLA_QK) + EPS)
        km_ref[h, :, 0:128] = (kn * r * gkn_ref[...]).astype(bf16)
        km_ref[h, :, 128:256] = (kpe2 * r).astype(bf16)
        vm_ref[h] = kvx[:, 256 * h + 128:256 * h + 256].astype(bf16)

    cd, s1d, s2d = tab(TB_CD), tab(TB_S1D), tab(TB_S2D)

    def dsa_head(x, g):
        y = x * rms(x, DSA_HEAD_DIM) * g
        return y * cd + _roll(y, 112) * s1d + _roll(y, 16) * s2d

    for h in range(DSA_HEADS):
        x = dq_ref[:, 128 * h:128 * h + 128].astype(f32)
        qd_ref[h] = dsa_head(x, gdq_ref[...]).astype(bf16)
    for g in range(DSA_KV_HEADS):
        x = dk_ref[:, 128 * g:128 * g + 128].astype(f32)
        kd_ref[g] = dsa_head(x, gdk_ref[...]).astype(bf16)

    ci, s1i, s2i = tab(TB_CI), tab(TB_S1I), tab(TB_S2I)
    for j in range(IDX_HEADS // 2):
        x = iq_ref[:, 128 * j:128 * j + 128].astype(f32)
        y = x * ci + _roll(x, 120) * s1i + _roll(x, 8) * s2i
        iqp_ref[:, 256 * j:256 * j + 128] = jnp.where(low, y, 0.0).astype(bf16)
        iqp_ref[:, 256 * j + 128:256 * j + 256] = jnp.where(low, 0.0, y).astype(bf16)


def _prep(z, basis, sel, glq, glkv, wuq, wukv, gq, gkn, g46, gdq, gdk, tm):
    t = z.shape[0]
    bf16 = jnp.bfloat16
    row = lambda w, c: pl.BlockSpec((tm, w), lambda i, c=c: (i, c))
    full = lambda a: pl.BlockSpec(a.shape, lambda i: (0,) * a.ndim)
    heads = lambda n, w: pl.BlockSpec((n, tm, w), lambda i: (0, i, 0))
    return pl.pallas_call(
        _prep_kernel,
        out_shape=(
            jax.ShapeDtypeStruct((MLA_HEADS, t, 256), bf16),
            jax.ShapeDtypeStruct((MLA_HEADS, t, 256), bf16),
            jax.ShapeDtypeStruct((MLA_HEADS, t, 128), bf16),
            jax.ShapeDtypeStruct((DSA_HEADS, t, 128), bf16),
            jax.ShapeDtypeStruct((DSA_KV_HEADS, t, 128), bf16),
            jax.ShapeDtypeStruct((t, IDX_HEADS * 128), bf16),
            jax.ShapeDtypeStruct((t, 128), bf16),
        ),
        grid=(t // tm,),
        in_specs=[
            row(512, Z_CQ // 512), row(512, Z_CKV // 512), row(128, Z_KPE_IK // 128),
            row(768, Z_DQ // 768), row(256, Z_DK // 256), row(1024, Z_IQ // 1024),
            pl.BlockSpec((tm, LANES), lambda i: (i, 0)), full(sel),
            full(glq), full(glkv), full(wuq), full(wukv), full(gq), full(gkn), full(g46), full(gdq), full(gdk),
        ],
        out_specs=(
            heads(MLA_HEADS, 256), heads(MLA_HEADS, 256), heads(MLA_HEADS, 128),
            heads(DSA_HEADS, 128), heads(DSA_KV_HEADS, 128),
            pl.BlockSpec((tm, IDX_HEADS * 128), lambda i: (i, 0)),
            pl.BlockSpec((tm, 128), lambda i: (i, 0)),
        ),
        compiler_params=_cparams(("parallel",)),
        name="prep",
    )(z, z, z, z, z, z, basis, sel, glq, glkv, wuq, wukv, gq, gkn, g46, gdq, gdk)


def _silu(x):
    return x / (1.0 + jnp.exp(-x))


def _conv_kernel(cx_ref, cb_ref, cc_ref, cz_ref, hx_ref, hc_ref, w_ref, o_ref, *, tiles_per_seq):
    f32 = jnp.float32
    tm, ch = cx_ref.shape
    u = cc_ref[...].astype(f32) * cx_ref[...].astype(f32)
    keep = (pl.program_id(0) % tiles_per_seq != 0).astype(f32)
    hu = hc_ref[...].astype(f32) * hx_ref[...].astype(f32) * keep
    row = lax.broadcasted_iota(jnp.int32, (tm, ch), 0)
    u1 = jnp.where(row == 0, hu[7:8, :], pltpu.roll(u, 1, 0))
    u2 = jnp.where(row == 0, hu[6:7, :], jnp.where(row == 1, hu[7:8, :], pltpu.roll(u, 2, 0)))
    conv = w_ref[0:1, :] * u2 + w_ref[1:2, :] * u1 + w_ref[2:3, :] * u
    o_ref[...] = (cb_ref[...].astype(f32) * conv * _silu(cz_ref[...].astype(f32))).astype(o_ref.dtype)


def _conv(z, w, seq, tm):
    t = z.shape[0]
    col = lambda c: pl.BlockSpec((tm, CONV_CH), lambda i, c=c: (i, c))
    halo = lambda c: pl.BlockSpec((8, CONV_CH), lambda i, c=c: (jnp.maximum(i * (tm // 8) - 1, 0), c))
    return pl.pallas_call(
        functools.partial(_conv_kernel, tiles_per_seq=seq // tm),
        out_shape=jax.ShapeDtypeStruct((t, CONV_CH), jnp.bfloat16),
        grid=(t // tm,),
        in_specs=[col(Z_CX // CONV_CH), col(Z_CB // CONV_CH), col(Z_CC // CONV_CH), col(Z_CONVZ // CONV_CH),
                  halo(Z_CX // CONV_CH), halo(Z_CC // CONV_CH),
                  pl.BlockSpec((8, CONV_CH), lambda i: (0, 0))],
        out_specs=pl.BlockSpec((tm, CONV_CH), lambda i: (i, 0)),
        compiler_params=_cparams(("parallel",)),
        name="conv",
    )(z, z, z, z, z, z, w)


def _tile_lanes(x, n):
    return x if n == 1 else jnp.concatenate([x] * n, axis=1)


def _softmax_step(s, v2, m_ref, acc_ref):
    m_prev = m_ref[...]
    m_new = jnp.maximum(m_prev, jnp.max(s, axis=-1, keepdims=True))
    p = jnp.exp2(s - _tile_lanes(m_new, s.shape[1] // LANES))
    alpha = jnp.exp2(m_prev - m_new)
    acc_ref[...] = _tile_lanes(alpha, 2) * acc_ref[...] + _dot(p.astype(jnp.bfloat16), v2)
    m_ref[...] = m_new


def _with_ones(v):
    return jnp.concatenate([v, jnp.ones_like(v)], axis=1)


MLA_HEADS_PER_STEP = 6


def _mla_kernel(q_ref, k_ref, v_ref, zg_ref, o_ref, m_ref, acc_ref, *, tk):
    f32 = jnp.float32
    qi = pl.program_id(2)
    m_ref[...] = jnp.full(m_ref.shape, NEG_BIG, f32)
    acc_ref[...] = jnp.zeros(acc_ref.shape, f32)

    def step(ki, row0, masked):
        keys = pl.ds(pl.multiple_of(ki * tk, tk), tk)
        rows = pl.ds(row0, q_ref.shape[1] - row0)
        for h in range(MLA_HEADS_PER_STEP):
            s = _dot_nt(q_ref[h, rows, :], k_ref[h, keys, :])
            if masked:
                r = lax.broadcasted_iota(jnp.int32, s.shape, 0)
                c = lax.broadcasted_iota(jnp.int32, s.shape, 1)
                s = jnp.where(c <= r, s, 2.0 * NEG_BIG)
            _softmax_step(s, _with_ones(v_ref[h, keys, :]), m_ref.at[h, rows, :], acc_ref.at[h, rows, :])

    def body(ki, c):
        step(ki, 0, False)
        return c

    lax.fori_loop(0, 2 * qi, body, 0)
    step(2 * qi, 0, True)
    step(2 * qi + 1, tk, True)
    for h in range(MLA_HEADS_PER_STEP):
        o = acc_ref[h, :, 0:128] / acc_ref[h, :, 128:256]
        zg = zg_ref[:, 128 * h:128 * h + 128].astype(f32)
        o_ref[:, 128 * h:128 * h + 128] = (o * _silu(zg)).astype(o_ref.dtype)


def _mla(qm, km, vm, z, batch, seq, tk):
    t = z.shape[0]
    tq = 2 * tk
    nq = seq // tq
    hs = MLA_HEADS_PER_STEP
    resident = dict(pipeline_mode=pl.Buffered(1))
    return pl.pallas_call(
        functools.partial(_mla_kernel, tk=tk),
        out_shape=jax.ShapeDtypeStruct((t, MLA_WIDTH), jnp.bfloat16),
        grid=(batch, MLA_HEADS // hs, nq),
        in_specs=[
            pl.BlockSpec((hs, tq, 256), lambda b, h, i: (h, b * nq + i, 0)),
            pl.BlockSpec((hs, seq, 256), lambda b, h, i: (h, b, 0), **resident),
            pl.BlockSpec((hs, seq, 128), lambda b, h, i: (h, b, 0), **resident),
            pl.BlockSpec((tq, 128 * hs), lambda b, h, i: (b * nq + i, Z_MLAZ // (128 * hs) + h)),
        ],
        out_specs=pl.BlockSpec((tq, 128 * hs), lambda b, h, i: (b * nq + i, h)),
        scratch_shapes=[pltpu.VMEM((hs, tq, LANES), jnp.float32), pltpu.VMEM((hs, tq, 256), jnp.float32)],
        compiler_params=_cparams(("parallel", "parallel", "arbitrary")),
        name="mla",
    )(qm, km, vm, z)


KC = 256
ATT_KC = 512
COUNT_FANIN = 8
SCORE_ROWS = 256


def _dsa_kernel(iq_ref, iw_ref, ik_ref, q_ref, k_ref, v0_ref, v1_ref, zg_ref, expand_ref, o_ref,
                key_ref, hi_ref, lo_ref, wb_ref, stage_ref, m_ref, acc_ref, *, k_top):
    f32, i32, bf16 = jnp.float32, jnp.int32, jnp.bfloat16
    tq = iq_ref.shape[0]
    i = pl.program_id(1)
    n_chunks = ((i + 1) * tq + KC - 1) // KC
    q_pos = i * tq + lax.broadcasted_iota(i32, (tq, KC), 0)
    lane_kc = lax.broadcasted_iota(i32, (tq, KC), 1)

    w = (iw_ref[...].astype(f32) * (IDX_HEADS ** -0.5)).astype(bf16)
    wb_ref[...] = _dot(w, expand_ref[...])

    def emit_keys(c):
        start = pl.multiple_of(c * KC, KC)
        bits = lax.bitcast_convert_type(stage_ref[...], i32)
        key = jnp.where(bits < 0, bits ^ jnp.int32(0x7FFFFFFF), bits)
        key = jnp.where(start + lane_kc <= q_pos, key, jnp.int32(INT_MIN))
        key_ref[:, pl.ds(start, KC)] = key
        key_t = key.T
        hi_ref[pl.ds(start, KC), :] = (key_t >> 16).astype(jnp.int16)
        lo_ref[pl.ds(start, KC), :] = ((key_t & 0xFFFF) - 32768).astype(jnp.int16)

    def score_chunk(c, carry):
        emit_keys(jnp.maximum(c - 1, 0))
        ikc = ik_ref[pl.ds(pl.multiple_of(c * KC, KC), KC), :]
        for r0 in range(0, tq, SCORE_ROWS):
            rows = pl.ds(r0, SCORE_ROWS)
            acc = jnp.zeros((SCORE_ROWS, KC), f32)
            for h in range(IDX_HEADS):
                lg = _dot_nt(iq_ref[rows, 128 * h:128 * h + 128], ikc)
                acc = acc + _tile_lanes(wb_ref[rows, 128 * h:128 * h + 128], KC // LANES) * jnp.maximum(lg, 0.0)
            stage_ref[rows, :] = acc
        return carry

    stage_ref[...] = jnp.zeros((tq, KC), f32)
    lax.fori_loop(0, n_chunks, score_chunk, 0)
    emit_keys(n_chunks - 1)

    n_att = (n_chunks * KC + ATT_KC - 1) // ATT_KC

    def fill(c, carry):
        start = pl.multiple_of(c * KC, KC)
        key_ref[:, pl.ds(start, KC)] = jnp.full((tq, KC), INT_MIN, i32)
        hi_ref[pl.ds(start, KC), :] = jnp.full((KC, tq), -32768, jnp.int16)
        lo_ref[pl.ds(start, KC), :] = jnp.full((KC, tq), -32768, jnp.int16)
        return carry

    lax.fori_loop(n_chunks, n_att * (ATT_KC // KC), fill, 0)

    i16 = jnp.int16
    low16 = jnp.full((), -32768, i16)
    one16, zero16 = jnp.ones((), i16), jnp.zeros((), i16)

    def count16(ref, cand):
        c16 = cand.astype(i16)
        def body(c, cnt):
            start = pl.multiple_of(c * ATT_KC, ATT_KC)
            for g in range(0, ATT_KC, 16 * COUNT_FANIN):
                hit = jnp.where(ref[pl.ds(start + g, 16 * COUNT_FANIN), :] >= c16, one16, zero16)
                part = [hit[16 * j:16 * (j + 1)] for j in range(COUNT_FANIN)]
                while len(part) > 1:
                    part = [a + b for a, b in zip(part[0::2], part[1::2])]
                cnt = cnt + part[0]
            return cnt
        cnt = lax.fori_loop(0, n_att, body, jnp.zeros((16, tq), i16))
        return jnp.sum(cnt.astype(i32), axis=0, keepdims=True)

    def search16(ref, want):
        def step(it, carry):
            r, n_r = carry
            cand = r + lax.shift_left(jnp.int32(1), 15 - it)
            n = count16(ref, cand)
            ok = n >= want
            return jnp.where(ok, cand, r), jnp.where(ok, n, n_r)
        return lax.fori_loop(0, 16, step, (jnp.full((1, tq), -32768, i32), jnp.zeros((1, tq), i32)))

    t_hi, n_hi = search16(hi_ref, k_top)
    n_above = count16(hi_ref, t_hi + 1)
    t_hi16 = t_hi.astype(i16)

    def restrict(c, carry):
        rows = pl.ds(pl.multiple_of(c * ATT_KC, ATT_KC), ATT_KC)
        lo_ref[rows, :] = jnp.where(hi_ref[rows, :] == t_hi16, lo_ref[rows, :], low16)
        return carry

    lax.fori_loop(0, n_att, restrict, 0)
    t_lo, n_lo = search16(lo_ref, k_top - n_above)
    thr = t_hi * 65536 + (t_lo + 32768)

    n_ge = jnp.where(t_lo > -32768, n_above + n_lo, jnp.where(t_hi > -32768, n_hi, n_above))
    has_excess = jnp.max(n_ge) > k_top

    @pl.when(has_excess)
    def _():
        def count_where(hits):
            def body(c, cnt):
                start = pl.multiple_of(c * KC, KC)
                hit = hits(key_ref[:, pl.ds(start, KC)].T, start)
                return cnt + jnp.sum(hit.reshape(KC // 8, 8, tq), axis=0)
            cnt = lax.fori_loop(0, n_chunks, body, jnp.zeros((8, tq), i32))
            return jnp.sum(cnt, axis=0, keepdims=True)

        need = k_top - count_where(lambda kt, _: jnp.where(kt > thr, 1, 0))
        sub = lax.broadcasted_iota(i32, (KC, tq), 0)

        def idx_step(it, bound):
            cand = bound + lax.shift_left(jnp.int32(1), 30 - it)
            n_below = count_where(lambda kt, start: jnp.where(kt == thr, jnp.where(start + sub < cand, 1, 0), 0))
            return jnp.where(n_below <= need, cand, bound)

        bound = lax.fori_loop(0, 31, idx_step, jnp.zeros((1, tq), i32))

        def demote(c, carry):
            start = pl.multiple_of(c * KC, KC)
            kt = key_ref[:, pl.ds(start, KC)].T
            drop = jnp.where(kt == thr, jnp.where(start + sub >= bound, 1, 0), 0)
            key_ref[:, pl.ds(start, KC)] = jnp.where(drop == 1, jnp.int32(INT_MIN), kt).T
            return carry

        lax.fori_loop(0, n_chunks, demote, 0)

    thr_sel = jnp.maximum(thr, jnp.int32(INT_MIN + 1))
    thr_kc = _tile_lanes(jnp.broadcast_to(thr_sel, (LANES, tq)).T, KC // LANES)

    m_ref[...] = jnp.full(m_ref.shape, NEG_BIG, f32)
    acc_ref[...] = jnp.zeros(acc_ref.shape, f32)
    v_refs = (v0_ref, v1_ref)

    thr_att = _tile_lanes(thr_kc, ATT_KC // KC)

    def attn_chunk(c, carry):
        start = pl.multiple_of(c * ATT_KC, ATT_KC)
        bias = jnp.where(key_ref[:, pl.ds(start, ATT_KC)] >= thr_att, 0.0, 2.0 * NEG_BIG)
        for hd in range(DSA_HEADS):
            g = hd // DSA_GROUP
            s = _dot_nt(q_ref[hd], k_ref[g, pl.ds(start, ATT_KC), :]) + bias
            _softmax_step(s, _with_ones(v_refs[g][pl.ds(start, ATT_KC), :]), m_ref.at[hd], acc_ref.at[hd])
        return carry

    lax.fori_loop(0, n_att, attn_chunk, 0)

    for hd in range(DSA_HEADS):
        o = acc_ref[hd, :, 0:128] / acc_ref[hd, :, 128:256]
        zg = zg_ref[:, 128 * hd:128 * hd + 128].astype(f32)
        o_ref[:, 128 * hd:128 * hd + 128] = (o * _silu(zg)).astype(o_ref.dtype)


def _dsa(iqp, ikd, qd, kd, z, batch, seq, tq, k_top):
    t = z.shape[0]
    nq = seq // tq
    f32 = jnp.float32
    expand = np.zeros((LANES, IDX_HEADS * LANES), np.float32)
    for h in range(IDX_HEADS):
        expand[h, LANES * h:LANES * (h + 1)] = 1.0
    expand = jnp.asarray(expand, jnp.bfloat16)
    return pl.pallas_call(
        functools.partial(_dsa_kernel, k_top=k_top),
        out_shape=jax.ShapeDtypeStruct((t, DSA_WIDTH), jnp.bfloat16),
        grid=(batch, nq),
        in_specs=[
            pl.BlockSpec((tq, IDX_HEADS * 128), lambda b, i: (b * nq + i, 0)),
            pl.BlockSpec((tq, 128), lambda b, i: (b * nq + i, Z_IW // 128)),
            pl.BlockSpec((seq, 128), lambda b, i: (b, 0)),
            pl.BlockSpec((DSA_HEADS, tq, 128), lambda b, i: (0, b * nq + i, 0)),
            pl.BlockSpec((DSA_KV_HEADS, seq, 128), lambda b, i: (0, b, 0)),
            pl.BlockSpec((seq, 128), lambda b, i: (b, Z_DV // 128)),
            pl.BlockSpec((seq, 128), lambda b, i: (b, Z_DV // 128 + 1)),
            pl.BlockSpec((tq, DSA_WIDTH), lambda b, i: (b * nq + i, Z_DSAZ // DSA_WIDTH)),
            pl.BlockSpec(expand.shape, lambda b, i: (0, 0)),
        ],
        out_specs=pl.BlockSpec((tq, DSA_WIDTH), lambda b, i: (b * nq + i, 0)),
        scratch_shapes=[
            pltpu.VMEM((tq, seq), jnp.int32),
            pltpu.VMEM((seq, tq), jnp.int16),
            pltpu.VMEM((seq, tq), jnp.int16),
            pltpu.VMEM((tq, IDX_HEADS * LANES), f32),
            pltpu.VMEM((tq, KC), f32),
            pltpu.VMEM((DSA_HEADS, tq, LANES), f32),
            pltpu.VMEM((DSA_HEADS, tq, 2 * DSA_HEAD_DIM), f32),
        ],
        compiler_params=_cparams(("parallel", "arbitrary")),
        name="dsa",
    )(iqp, z, ikd, qd, kd, z, z, z, expand)


def _post_kernel(h_ref, ya_ref, yb_ref, yc_ref, p_ref, g_ref, wa_ref, wb_ref, wc_ref, wg_ref, wp_ref, o_ref):
    h1 = h_ref[...] + _dot(ya_ref[...], wa_ref[...])
    h1 = h1 + _dot(yb_ref[...], wb_ref[...])
    h1 = h1 + _dot(yc_ref[...], wc_ref[...])
    r = lax.rsqrt(jnp.mean(h1 * h1, axis=-1, keepdims=True) + EPS)
    a = (h1 * r * g_ref[...]).astype(jnp.bfloat16)
    gate = _dot(a, wg_ref[...])
    gate = 1.0 / (1.0 + jnp.exp(-gate))
    o_ref[...] = h1 + gate * _dot(p_ref[0].astype(jnp.bfloat16), wp_ref[...])


def _post(h, ya, yb, yc, p, layer, g, wa, wb, wc, wg, wp, tm):
    t, d = h.shape
    rows = lambda a: pl.BlockSpec((tm, a.shape[1]), lambda i: (i, 0))
    whole = lambda a: pl.BlockSpec(a.shape, lambda i: (0, 0), pipeline_mode=pl.Buffered(1))
    return pl.pallas_call(
        _post_kernel,
        out_shape=jax.ShapeDtypeStruct((t, d), jnp.float32),
        grid=(t // tm,),
        in_specs=[rows(h), rows(ya), rows(yb), rows(yc),
                  pl.BlockSpec((1, tm, p.shape[2]), lambda i: (layer, i, 0)), whole(g),
                  whole(wa), whole(wb), whole(wc), whole(wg), whole(wp)],
        out_specs=rows(h),
        compiler_params=_cparams(("parallel",)),
        name="post",
    )(h, ya, yb, yc, p, g, wa, wb, wc, wg, wp)


_REGROUP = (
    (Z_CQ, _O_CQ, 512), (Z_CKV, _O_CKV, 512),
    (Z_CX, _O_CX, 512), (Z_CB, _O_CB, 512), (Z_CC, _O_CC, 512), (Z_CONVZ, _O_CONVZ, 512),
    (Z_MLAZ, _O_MLAZ, 768), (Z_DQ, _O_DQ, 768), (Z_DSAZ, _O_DSAZ, 768),
    (Z_DK, _O_DK, 256), (Z_DV, _O_DV, 256), (Z_IQ, _O_IQ, 1024),
)


def _regroup_kernel(w_ref, o_ref):
    bf16 = jnp.bfloat16
    cols = w_ref.shape[2]
    for dst, src, n in _REGROUP + ((Z_KPE_IK, _O_KPE, MLA_ROPE), (Z_KPE_IK + MLA_ROPE, _O_IK, IDX_DIM),
                                   (Z_IW, _O_IW, IDX_HEADS)):
        o_ref[0, dst:dst + n, :] = w_ref[0, src:src + n, :].astype(bf16)
    o_ref[0, Z_IW + IDX_HEADS:Z_IQ, :] = jnp.zeros((Z_IQ - Z_IW - IDX_HEADS, cols), bf16)


def _regroup_w_in(w, tc=TC_REGROUP):
    depth, d, n = w.shape
    return pl.pallas_call(
        _regroup_kernel,
        out_shape=jax.ShapeDtypeStruct((depth, Z_WIDTH, d), jnp.bfloat16),
        grid=(depth, d // tc),
        in_specs=[pl.BlockSpec((1, n, tc), lambda l, i: (l, 0, i))],
        out_specs=pl.BlockSpec((1, Z_WIDTH, tc), lambda l, i: (l, 0, i)),
        compiler_params=_cparams(("parallel", "parallel")),
        name="regroup",
    )(jnp.swapaxes(w, 1, 2))


def _pair_layout(a):
    lead = a.shape[:-1]
    a = a.reshape(lead + (MLA_HEADS, MLA_QK))
    nope, x1, x2 = a[..., :MLA_NOPE], a[..., MLA_NOPE:MLA_NOPE + 32], a[..., MLA_NOPE + 32:]
    return jnp.concatenate([nope, x1, x2, x2, x1], axis=-1).reshape(lead + (MLA_HEADS * 256,))


_B_C32, _B_S32, _B_C16, _B_S16, _B_C8, _B_S8, _B_ONE, _B_ROWS = 0, 32, 64, 80, 96, 104, 112, 128


def _table_selector():
    sel = np.zeros((_B_ROWS, N_TABLES * LANES), np.float32)

    def put(table, lane, base, n, coef=1.0):
        for j in range(n):
            sel[base + (j if base != _B_ONE else 0), table * LANES + lane + j] = coef

    c32, s32, c16, s16, c8, s8, one = _B_C32, _B_S32, _B_C16, _B_S16, _B_C8, _B_S8, _B_ONE
    put(TB_CSQ, 0, c32, 32); put(TB_CSQ, 32, c32, 32); put(TB_CSQ, 64, s32, 32, -1.0); put(TB_CSQ, 96, s32, 32)
    put(TB_C46, 0, c32, 32); put(TB_C46, 32, c32, 32); put(TB_C46, 64, c8, 8); put(TB_C46, 72, c8, 8)
    put(TB_C46, 80, one, 48)
    put(TB_S1A, 0, s32, 32, -1.0)
    put(TB_S2A, 32, s32, 32)
    put(TB_S1B, 64, s8, 8, -1.0)
    put(TB_S2B, 72, s8, 8)
    put(TB_CD, 0, c16, 16); put(TB_CD, 16, c16, 16); put(TB_CD, 32, one, 96)
    put(TB_S1D, 0, s16, 16, -1.0)
    put(TB_S2D, 16, s16, 16)
    qs = IDX_DIM ** -0.5
    for off in (0, 64):
        put(TB_CI, off, c8, 8, qs); put(TB_CI, off + 8, c8, 8, qs); put(TB_CI, off + 16, one, 48, qs)
        put(TB_S1I, off, s8, 8, -qs)
        put(TB_S2I, off + 8, s8, 8, qs)
    return sel


def _rope_basis(positions):
    f32 = jnp.float32
    pos = positions.reshape(-1, 1).astype(f32)
    freqs = [ROPE_THETA ** (-jnp.arange(half, dtype=f32) / half) for half in (MLA_ROPE // 2, DSA_ROT // 2, IDX_ROT // 2)]
    inv = jnp.concatenate([freqs[0], freqs[0], freqs[1], freqs[1], freqs[2], freqs[2], jnp.zeros((_B_ROWS - _B_ONE,), f32)])
    is_cos = np.zeros((_B_ROWS,), bool)
    for start, stop in ((_B_C32, _B_S32), (_B_C16, _B_S16), (_B_C8, _B_S8), (_B_ONE, _B_ONE + 1)):
        is_cos[start:stop] = True
    ang = pos * inv[None, :]
    basis = jnp.where(jnp.asarray(is_cos)[None, :], jnp.cos(ang), jnp.sin(ang))
    return basis, jnp.asarray(_table_selector(), jnp.bfloat16)


def _layer(h, layer, p_all, w_in_all, tabs, batch, seq, k_top, norm_in, mla_gq, mla_w_uq, mla_gkv, mla_w_ukv, mla_qn,
           mla_kn, conv_w, dsa_qn, dsa_kn, w_out, ple_norm, ple_w_gate, ple_w_proj):
    f32, bf16 = jnp.float32, jnp.bfloat16
    t = h.shape[0]

    z = _inproj(h, norm_in[None, :], w_in_all, layer, min(TM_INPROJ, t), TN_INPROJ)

    gq = _pair_layout(jnp.tile(mla_qn, MLA_HEADS))[None, :] * (MLA_QK ** -0.5 * LOG2E)
    g46 = jnp.concatenate([mla_kn[MLA_NOPE:], jnp.ones((64,), f32)])[None, :]
    qm, km, vm, qd, kd, iqp, ikd = _prep(
        z, *tabs, mla_gq[None, :], mla_gkv[None, :], _pair_layout(mla_w_uq).astype(bf16), mla_w_ukv.astype(bf16),
        gq, mla_kn[None, :MLA_NOPE], g46, dsa_qn[None, :] * (DSA_HEAD_DIM ** -0.5 * LOG2E), dsa_kn[None, :],
        min(TM_PREP, t))

    y_b = _conv(z, jnp.pad(conv_w, ((0, 8 - CONV_K), (0, 0))), seq, min(TM_CONV, seq))
    y_a = _mla(qm, km, vm, z, batch, seq, TK_MLA)
    y_c = _dsa(iqp, ikd, qd, kd, z, batch, seq, TQ_DSA, k_top)

    wo = w_out.astype(bf16)
    return _post(h, y_a, y_b, y_c, p_all, layer, ple_norm[None, :],
                 wo[:MLA_WIDTH], wo[MLA_WIDTH:MLA_WIDTH + CONV_CH], wo[MLA_WIDTH + CONV_CH:],
                 ple_w_gate.astype(bf16), ple_w_proj.astype(bf16), min(TM_POST, t))


def kernel(x, p, positions, norm_in, w_in, mla_gq, mla_w_uq, mla_gkv, mla_w_ukv, mla_qn, mla_kn, conv_w, dsa_qn,
           dsa_kn, w_out, ple_norm, ple_w_gate, ple_w_proj):
    batch, seq, d = x.shape
    depth = p.shape[0]
    t = batch * seq
    k_top = min(TOPK_MAX, seq // 4)
    tabs = _rope_basis(positions)
    w_in_z = _regroup_w_in(w_in)
    h = x.reshape(t, d)
    p_all = p.reshape(depth, t, PLE_DIM)
    for i in range(depth):
        h = _layer(h, i, p_all, w_in_z, tabs, batch, seq, k_top, norm_in[i], mla_gq[i],
                   mla_w_uq[i], mla_gkv[i], mla_w_ukv[i], mla_qn[i], mla_kn[i], conv_w[i], dsa_qn[i], dsa_kn[i],
                   w_out[i], ple_norm[i], ple_w_gate[i], ple_w_proj[i])
    return h.reshape(batch, seq, d)
```

```python
import functools

import numpy as np
import jax
import jax.numpy as jnp
from jax import lax
from jax.experimental import pallas as pl
from jax.experimental.pallas import tpu as pltpu

D_MODEL = 2048
PLE_DIM = 256
ROPE_THETA = 500000.0
EPS = 1e-6

MLA_HEADS = 6
MLA_Q_LORA = 512
MLA_KV_LORA = 512
MLA_NOPE = 128
MLA_ROPE = 64
MLA_V = 128
MLA_QK = MLA_NOPE + MLA_ROPE
MLA_WIDTH = MLA_HEADS * MLA_V

CONV_CH = 512
CONV_K = 3

DSA_HEADS = 6
DSA_KV_HEADS = 2
DSA_GROUP = DSA_HEADS // DSA_KV_HEADS
DSA_HEAD_DIM = 128
DSA_WIDTH = DSA_HEADS * DSA_HEAD_DIM
DSA_ROT = DSA_HEAD_DIM // 4
IDX_HEADS = 16
IDX_DIM = 64
IDX_ROT = IDX_DIM // 4
TOPK_MAX = 256

LANES = 128

_O_CQ, _O_CKV, _O_KPE, _O_MLAZ = 0, 512, 1024, 1088
_O_CX, _O_CB, _O_CC, _O_CONVZ = 1856, 2368, 2880, 3392
_O_DQ, _O_DK, _O_DV, _O_DSAZ = 3904, 4672, 4928, 5184
_O_IQ, _O_IW, _O_IK, _N_IN = 5952, 6976, 6992, 7056

Z_CQ, Z_CKV = 0, 512
Z_CX, Z_CB, Z_CC, Z_CONVZ = 1024, 1536, 2048, 2560
Z_MLAZ, Z_DQ, Z_DSAZ = 3072, 3840, 4608
Z_DK, Z_DV = 5376, 5632
Z_KPE_IK, Z_IW, Z_IQ = 5888, 6016, 6144
Z_WIDTH = 7168

LOG2E = 1.4426950408889634
INT_MIN = -(2 ** 31)
NEG_BIG = -1e30

VMEM_LIMIT = 56 * 1024 * 1024

TM_INPROJ, TN_INPROJ = 1024, Z_WIDTH // 4
TC_REGROUP = 256
TM_PREP = 512
TK_MLA = 512
TQ_DSA = 512
TM_POST = 512


def _cparams(sem):
    return pltpu.CompilerParams(dimension_semantics=sem, vmem_limit_bytes=VMEM_LIMIT)


def _dot(a, b):
    return jnp.dot(a, b, preferred_element_type=jnp.float32)


def _dot_nt(a, b):
    return lax.dot_general(a, b, (((1,), (1,)), ((), ())), preferred_element_type=jnp.float32)


def _roll(x, shift):
    return pltpu.roll(x, shift, 1)


def _inproj_kernel(x_ref, g_ref, w_ref, o_ref, a_ref):
    @pl.when(pl.program_id(1) == 0)
    def _():
        x = x_ref[...]
        r = lax.rsqrt(jnp.mean(x * x, axis=-1, keepdims=True) + EPS)
        a_ref[...] = (x * r * g_ref[...]).astype(jnp.bfloat16)

    o_ref[...] = _dot_nt(a_ref[...], w_ref[0]).astype(o_ref.dtype)


def _inproj(h, g, w, layer, tm, tn):
    t, d = h.shape
    n = w.shape[1]
    return pl.pallas_call(
        _inproj_kernel,
        out_shape=jax.ShapeDtypeStruct((t, n), jnp.bfloat16),
        grid=(t // tm, n // tn),
        in_specs=[
            pl.BlockSpec((tm, d), lambda i, j: (i, 0)),
            pl.BlockSpec((1, d), lambda i, j: (0, 0)),
            pl.BlockSpec((1, tn, d), lambda i, j: (layer, j, 0)),
        ],
        out_specs=pl.BlockSpec((tm, tn), lambda i, j: (i, j)),
        scratch_shapes=[pltpu.VMEM((tm, d), jnp.bfloat16)],
        compiler_params=_cparams(("parallel", "arbitrary")),
        name="inproj",
    )(h, g, w)


TB_CSQ = 0
TB_C46, TB_S1A, TB_S2A, TB_S1B, TB_S2B = 1, 2, 3, 4, 5
TB_CD, TB_S1D, TB_S2D = 6, 7, 8
TB_CI, TB_S1I, TB_S2I = 9, 10, 11
N_TABLES = 12


def _prep_kernel(cq_ref, ckv_ref, t46_ref, dq_ref, dk_ref, iq_ref, basis_ref, sel_ref,
                 glq_ref, glkv_ref, wuq_ref, wukv_ref, gq_ref, gkn_ref, g46_ref, gdq_ref, gdk_ref,
                 qm_ref, km_ref, vm_ref, qd_ref, kd_ref, iqp_ref, ikd_ref):
    f32, bf16 = jnp.float32, jnp.bfloat16
    tm = cq_ref.shape[0]
    lane = lax.broadcasted_iota(jnp.int32, (tm, LANES), 1)
    low = lane < 64

    b0 = basis_ref[...]
    b_hi = b0.astype(bf16)
    b1 = b0 - b_hi.astype(f32)
    b_mid = b1.astype(bf16)
    b_lo = (b1 - b_mid.astype(f32)).astype(bf16)
    tabs = _dot(b_hi, sel_ref[...]) + _dot(b_mid, sel_ref[...]) + _dot(b_lo, sel_ref[...])
    tab = lambda k: tabs[:, LANES * k:LANES * (k + 1)]

    def rms(x, n):
        return lax.rsqrt(jnp.sum(x * x, axis=-1, keepdims=True) * (1.0 / n) + EPS)

    cq = cq_ref[...].astype(f32)
    aq = (cq * rms(cq, MLA_Q_LORA) * glq_ref[...]).astype(bf16)
    qx = _dot(aq, wuq_ref[...])
    csq = tab(TB_CSQ)
    for h in range(MLA_HEADS):
        nope = qx[:, 256 * h:256 * h + 128]
        pair = qx[:, 256 * h + 128:256 * h + 256]
        ss = jnp.sum(nope * nope, axis=-1, keepdims=True) + 0.5 * jnp.sum(pair * pair, axis=-1, keepdims=True)
        r = lax.rsqrt(ss * (1.0 / MLA_QK) + EPS)
        qm_ref[h, :, 0:128] = (nope * r * gq_ref[:, 256 * h:256 * h + 128]).astype(bf16)
        qm_ref[h, :, 128:256] = (pair * r * gq_ref[:, 256 * h + 128:256 * h + 256] * csq).astype(bf16)

    t46 = t46_ref[...].astype(f32)
    ss_pe = jnp.sum(jnp.where(low, t46 * t46, 0.0), axis=-1, keepdims=True)
    y46 = t46 * g46_ref[...]
    r46 = (y46 * tab(TB_C46)
           + _roll(y46, 96) * tab(TB_S1A) + _roll(y46, 32) * tab(TB_S2A)
           + _roll(y46, 120) * tab(TB_S1B) + _roll(y46, 8) * tab(TB_S2B))
    sw = _roll(r46, 64)
    kpe2 = jnp.where(low, r46, sw)
    ikd_ref[...] = jnp.where(low, sw, r46).astype(bf16)

    ckv = ckv_ref[...].astype(f32)
    akv = (ckv * rms(ckv, MLA_KV_LORA) * glkv_ref[...]).astype(bf16)
    kvx = _dot(akv, wukv_ref[...])
    for h in range(MLA_HEADS):
        kn = kvx[:, 256 * h:256 * h + 128]
        ss = jnp.sum(kn * kn, axis=-1, keepdims=True) + ss_pe
        r = lax.rsqrt(ss * (1.0 / MLA_QK) + EPS)
        km_ref[h, :, 0:128] = (kn * r * gkn_ref[...]).astype(bf16)
        km_ref[h, :, 128:256] = (kpe2 * r).astype(bf16)
        vm_ref[h] = kvx[:, 256 * h + 128:256 * h + 256].astype(bf16)

    cd, s1d, s2d = tab(TB_CD), tab(TB_S1D), tab(TB_S2D)

    def dsa_head(x, g):
        y = x * rms(x, DSA_HEAD_DIM) * g
        return y * cd + _roll(y, 112) * s1d + _roll(y, 16) * s2d

    for h in range(DSA_HEADS):
        x = dq_ref[:, 128 * h:128 * h + 128].astype(f32)
        qd_ref[h] = dsa_head(x, gdq_ref[...]).astype(bf16)
    for g in range(DSA_KV_HEADS):
        x = dk_ref[:, 128 * g:128 * g + 128].astype(f32)
        kd_ref[g] = dsa_head(x, gdk_ref[...]).astype(bf16)

    ci, s1i, s2i = tab(TB_CI), tab(TB_S1I), tab(TB_S2I)
    for j in range(IDX_HEADS // 2):
        x = iq_ref[:, 128 * j:128 * j + 128].astype(f32)
        y = x * ci + _roll(x, 120) * s1i + _roll(x, 8) * s2i
        iqp_ref[:, 256 * j:256 * j + 128] = jnp.where(low, y, 0.0).astype(bf16)
        iqp_ref[:, 256 * j + 128:256 * j + 256] = jnp.where(low, 0.0, y).astype(bf16)


def _prep(z, basis, sel, glq, glkv, wuq, wukv, gq, gkn, g46, gdq, gdk, tm):
    t = z.shape[0]
    bf16 = jnp.bfloat16
    row = lambda w, c: pl.BlockSpec((tm, w), lambda i, c=c: (i, c))
    full = lambda a: pl.BlockSpec(a.shape, lambda i: (0,) * a.ndim)
    heads = lambda n, w: pl.BlockSpec((n, tm, w), lambda i: (0, i, 0))
    return pl.pallas_call(
        _prep_kernel,
        out_shape=(
            jax.ShapeDtypeStruct((MLA_HEADS, t, 256), bf16),
            jax.ShapeDtypeStruct((MLA_HEADS, t, 256), bf16),
            jax.ShapeDtypeStruct((MLA_HEADS, t, 128), bf16),
            jax.ShapeDtypeStruct((DSA_HEADS, t, 128), bf16),
            jax.ShapeDtypeStruct((DSA_KV_HEADS, t, 128), bf16),
            jax.ShapeDtypeStruct((t, IDX_HEADS * 128), bf16),
            jax.ShapeDtypeStruct((t, 128), bf16),
        ),
        grid=(t // tm,),
        in_specs=[
            row(512, Z_CQ // 512), row(512, Z_CKV // 512), row(128, Z_KPE_IK // 128),
            row(768, Z_DQ // 768), row(256, Z_DK // 256), row(1024, Z_IQ // 1024),
            pl.BlockSpec((tm, LANES), lambda i: (i, 0)), full(sel),
            full(glq), full(glkv), full(wuq), full(wukv), full(gq), full(gkn), full(g46), full(gdq), full(gdk),
        ],
        out_specs=(
            heads(MLA_HEADS, 256), heads(MLA_HEADS, 256), heads(MLA_HEADS, 128),
            heads(DSA_HEADS, 128), heads(DSA_KV_HEADS, 128),
            pl.BlockSpec((tm, IDX_HEADS * 128), lambda i: (i, 0)),
            pl.BlockSpec((tm, 128), lambda i: (i, 0)),
        ),
        compiler_params=_cparams(("parallel",)),
        name="prep",
    )(z, z, z, z, z, z, basis, sel, glq, glkv, wuq, wukv, gq, gkn, g46, gdq, gdk)


def _silu(x):
    return x / (1.0 + jnp.exp(-x))


def _gated_conv(cx_ref, cb_ref, cc_ref, cz_ref, hx_ref, hc_ref, w_ref, tiles_per_seq):
    f32 = jnp.float32
    tm, ch = cx_ref.shape
    u = cc_ref[...].astype(f32) * cx_ref[...].astype(f32)
    keep = (pl.program_id(0) % tiles_per_seq != 0).astype(f32)
    hu = hc_ref[...].astype(f32) * hx_ref[...].astype(f32) * keep
    row = lax.broadcasted_iota(jnp.int32, (tm, ch), 0)
    u1 = jnp.where(row == 0, hu[7:8, :], pltpu.roll(u, 1, 0))
    u2 = jnp.where(row == 0, hu[6:7, :], jnp.where(row == 1, hu[7:8, :], pltpu.roll(u, 2, 0)))
    conv = w_ref[0:1, :] * u2 + w_ref[1:2, :] * u1 + w_ref[2:3, :] * u
    return cb_ref[...].astype(f32) * conv * _silu(cz_ref[...].astype(f32))


def _tile_lanes(x, n):
    return x if n == 1 else jnp.concatenate([x] * n, axis=1)


def _softmax_step(s, v2, m_ref, acc_ref):
    m_prev = m_ref[...]
    m_new = jnp.maximum(m_prev, jnp.max(s, axis=-1, keepdims=True))
    p = jnp.exp2(s - _tile_lanes(m_new, s.shape[1] // LANES))
    alpha = jnp.exp2(m_prev - m_new)
    acc_ref[...] = _tile_lanes(alpha, 2) * acc_ref[...] + _dot(p.astype(jnp.bfloat16), v2)
    m_ref[...] = m_new


def _with_ones(v):
    return jnp.concatenate([v, jnp.ones_like(v)], axis=1)


MLA_HEADS_PER_STEP = 6


def _mla_kernel(q_ref, k_ref, v_ref, zg_ref, o_ref, m_ref, acc_ref, *, tk):
    f32 = jnp.float32
    qi = pl.program_id(2)
    m_ref[...] = jnp.full(m_ref.shape, NEG_BIG, f32)
    acc_ref[...] = jnp.zeros(acc_ref.shape, f32)

    def step(ki, row0, masked):
        keys = pl.ds(pl.multiple_of(ki * tk, tk), tk)
        rows = pl.ds(row0, q_ref.shape[1] - row0)
        for h in range(MLA_HEADS_PER_STEP):
            s = _dot_nt(q_ref[h, rows, :], k_ref[h, keys, :])
            if masked:
                r = lax.broadcasted_iota(jnp.int32, s.shape, 0)
                c = lax.broadcasted_iota(jnp.int32, s.shape, 1)
                s = jnp.where(c <= r, s, 2.0 * NEG_BIG)
            _softmax_step(s, _with_ones(v_ref[h, keys, :]), m_ref.at[h, rows, :], acc_ref.at[h, rows, :])

    def body(ki, c):
        step(ki, 0, False)
        return c

    lax.fori_loop(0, 2 * qi, body, 0)
    step(2 * qi, 0, True)
    step(2 * qi + 1, tk, True)
    for h in range(MLA_HEADS_PER_STEP):
        o = acc_ref[h, :, 0:128] / acc_ref[h, :, 128:256]
        zg = zg_ref[:, 128 * h:128 * h + 128].astype(f32)
        o_ref[:, 128 * h:128 * h + 128] = (o * _silu(zg)).astype(o_ref.dtype)


def _mla(qm, km, vm, z, batch, seq, tk):
    t = z.shape[0]
    tq = 2 * tk
    nq = seq // tq
    hs = MLA_HEADS_PER_STEP
    resident = dict(pipeline_mode=pl.Buffered(1))
    return pl.pallas_call(
        functools.partial(_mla_kernel, tk=tk),
        out_shape=jax.ShapeDtypeStruct((t, MLA_WIDTH), jnp.bfloat16),
        grid=(batch, MLA_HEADS // hs, nq),
        in_specs=[
            pl.BlockSpec((hs, tq, 256), lambda b, h, i: (h, b * nq + i, 0)),
            pl.BlockSpec((hs, seq, 256), lambda b, h, i: (h, b, 0), **resident),
            pl.BlockSpec((hs, seq, 128), lambda b, h, i: (h, b, 0), **resident),
            pl.BlockSpec((tq, 128 * hs), lambda b, h, i: (b * nq + i, Z_MLAZ // (128 * hs) + h)),
        ],
        out_specs=pl.BlockSpec((tq, 128 * hs), lambda b, h, i: (b * nq + i, h)),
        scratch_shapes=[pltpu.VMEM((hs, tq, LANES), jnp.float32), pltpu.VMEM((hs, tq, 256), jnp.float32)],
        compiler_params=_cparams(("parallel", "parallel", "arbitrary")),
        name="mla",
    )(qm, km, vm, z)


KC = 256
ATT_KC = 512
COUNT_FANIN = 8
SCORE_ROWS = 256


def _dsa_kernel(iq_ref, iw_ref, ik_ref, q_ref, k_ref, v0_ref, v1_ref, zg_ref, expand_ref, o_ref,
                key_ref, hi_ref, lo_ref, wb_ref, stage_ref, m_ref, acc_ref, *, k_top):
    f32, i32, bf16 = jnp.float32, jnp.int32, jnp.bfloat16
    tq = iq_ref.shape[0]
    i = pl.program_id(1)
    n_chunks = ((i + 1) * tq + KC - 1) // KC
    q_pos = i * tq + lax.broadcasted_iota(i32, (tq, KC), 0)
    lane_kc = lax.broadcasted_iota(i32, (tq, KC), 1)

    w = (iw_ref[...].astype(f32) * (IDX_HEADS ** -0.5)).astype(bf16)
    wb_ref[...] = _dot(w, expand_ref[...])

    def emit_keys(c):
        start = pl.multiple_of(c * KC, KC)
        bits = lax.bitcast_convert_type(stage_ref[...], i32)
        key = jnp.where(bits < 0, bits ^ jnp.int32(0x7FFFFFFF), bits)
        key = jnp.where(start + lane_kc <= q_pos, key, jnp.int32(INT_MIN))
        key_ref[:, pl.ds(start, KC)] = key
        key_t = key.T
        hi_ref[pl.ds(start, KC), :] = (key_t >> 16).astype(jnp.int16)
        lo_ref[pl.ds(start, KC), :] = ((key_t & 0xFFFF) - 32768).astype(jnp.int16)

    def score_chunk(c, first_row=0):
        emit_keys(jnp.maximum(c - 1, 0))
        ikc = ik_ref[pl.ds(pl.multiple_of(c * KC, KC), KC), :]
        for r0 in range(first_row, tq, SCORE_ROWS):
            rows = pl.ds(r0, SCORE_ROWS)
            acc = jnp.zeros((SCORE_ROWS, KC), f32)
            for h in range(IDX_HEADS):
                lg = _dot_nt(iq_ref[rows, 128 * h:128 * h + 128], ikc)
                acc = acc + _tile_lanes(wb_ref[rows, 128 * h:128 * h + 128], KC // LANES) * jnp.maximum(lg, 0.0)
            stage_ref[rows, :] = acc

    stage_ref[...] = jnp.zeros((tq, KC), f32)
    n_full = (i * tq) // KC + 1
    lax.fori_loop(0, n_full, lambda c, carry: (score_chunk(c), carry)[1], 0)
    for extra in range(1, tq // KC):
        score_chunk(n_full + extra - 1, first_row=(extra * KC // SCORE_ROWS) * SCORE_ROWS)
    emit_keys(n_chunks - 1)

    n_att = (n_chunks * KC + ATT_KC - 1) // ATT_KC

    def fill(c, carry):
        start = pl.multiple_of(c * KC, KC)
        key_ref[:, pl.ds(start, KC)] = jnp.full((tq, KC), INT_MIN, i32)
        hi_ref[pl.ds(start, KC), :] = jnp.full((KC, tq), -32768, jnp.int16)
        lo_ref[pl.ds(start, KC), :] = jnp.full((KC, tq), -32768, jnp.int16)
        return carry

    lax.fori_loop(n_chunks, n_att * (ATT_KC // KC), fill, 0)

    i16 = jnp.int16
    low16 = jnp.full((), -32768, i16)
    one16, zero16 = jnp.ones((), i16), jnp.zeros((), i16)

    def count16(ref, cand):
        c16 = cand.astype(i16)
        def body(c, cnt):
            start = pl.multiple_of(c * ATT_KC, ATT_KC)
            for g in range(0, ATT_KC, 16 * COUNT_FANIN):
                hit = jnp.where(ref[pl.ds(start + g, 16 * COUNT_FANIN), :] >= c16, one16, zero16)
                part = [hit[16 * j:16 * (j + 1)] for j in range(COUNT_FANIN)]
                while len(part) > 1:
                    part = [a + b for a, b in zip(part[0::2], part[1::2])]
                cnt = cnt + part[0]
            return cnt
        cnt = lax.fori_loop(0, n_att, body, jnp.zeros((16, tq), i16))
        return jnp.sum(cnt.astype(i32), axis=0, keepdims=True)

    def search16(ref, want):
        def step(it, carry):
            r, n_r = carry
            cand = r + lax.shift_left(jnp.int32(1), 15 - it)
            n = count16(ref, cand)
            ok = n >= want
            return jnp.where(ok, cand, r), jnp.where(ok, n, n_r)
        return lax.fori_loop(0, 16, step, (jnp.full((1, tq), -32768, i32), jnp.zeros((1, tq), i32)))

    t_hi, n_hi = search16(hi_ref, k_top)
    n_above = count16(hi_ref, t_hi + 1)
    t_hi16 = t_hi.astype(i16)

    def restrict(c, carry):
        rows = pl.ds(pl.multiple_of(c * ATT_KC, ATT_KC), ATT_KC)
        lo_ref[rows, :] = jnp.where(hi_ref[rows, :] == t_hi16, lo_ref[rows, :], low16)
        return carry

    lax.fori_loop(0, n_att, restrict, 0)
    t_lo, n_lo = search16(lo_ref, k_top - n_above)
    thr = t_hi * 65536 + (t_lo + 32768)

    n_ge = jnp.where(t_lo > -32768, n_above + n_lo, jnp.where(t_hi > -32768, n_hi, n_above))
    has_excess = jnp.max(n_ge) > k_top

    @pl.when(has_excess)
    def _():
        def count_where(hits):
            def body(c, cnt):
                start = pl.multiple_of(c * KC, KC)
                hit = hits(key_ref[:, pl.ds(start, KC)].T, start)
                return cnt + jnp.sum(hit.reshape(KC // 8, 8, tq), axis=0)
            cnt = lax.fori_loop(0, n_chunks, body, jnp.zeros((8, tq), i32))
            return jnp.sum(cnt, axis=0, keepdims=True)

        need = k_top - count_where(lambda kt, _: jnp.where(kt > thr, 1, 0))
        sub = lax.broadcasted_iota(i32, (KC, tq), 0)

        def idx_step(it, bound):
            cand = bound + lax.shift_left(jnp.int32(1), 30 - it)
            n_below = count_where(lambda kt, start: jnp.where(kt == thr, jnp.where(start + sub < cand, 1, 0), 0))
            return jnp.where(n_below <= need, cand, bound)

        bound = lax.fori_loop(0, 31, idx_step, jnp.zeros((1, tq), i32))

        def demote(c, carry):
            start = pl.multiple_of(c * KC, KC)
            kt = key_ref[:, pl.ds(start, KC)].T
            drop = jnp.where(kt == thr, jnp.where(start + sub >= bound, 1, 0), 0)
            key_ref[:, pl.ds(start, KC)] = jnp.where(drop == 1, jnp.int32(INT_MIN), kt).T
            return carry

        lax.fori_loop(0, n_chunks, demote, 0)

    thr_sel = jnp.maximum(thr, jnp.int32(INT_MIN + 1))
    thr_kc = _tile_lanes(jnp.broadcast_to(thr_sel, (LANES, tq)).T, KC // LANES)

    m_ref[...] = jnp.full(m_ref.shape, NEG_BIG, f32)
    acc_ref[...] = jnp.zeros(acc_ref.shape, f32)
    v_refs = (v0_ref, v1_ref)

    thr_att = _tile_lanes(thr_kc, ATT_KC // KC)

    def attn_chunk(c, carry):
        start = pl.multiple_of(c * ATT_KC, ATT_KC)
        bias = jnp.where(key_ref[:, pl.ds(start, ATT_KC)] >= thr_att, 0.0, 2.0 * NEG_BIG)
        for hd in range(DSA_HEADS):
            g = hd // DSA_GROUP
            s = _dot_nt(q_ref[hd], k_ref[g, pl.ds(start, ATT_KC), :]) + bias
            _softmax_step(s, _with_ones(v_refs[g][pl.ds(start, ATT_KC), :]), m_ref.at[hd], acc_ref.at[hd])
        return carry

    lax.fori_loop(0, n_att, attn_chunk, 0)

    for hd in range(DSA_HEADS):
        o = acc_ref[hd, :, 0:128] / acc_ref[hd, :, 128:256]
        zg = zg_ref[:, 128 * hd:128 * hd + 128].astype(f32)
        o_ref[:, 128 * hd:128 * hd + 128] = (o * _silu(zg)).astype(o_ref.dtype)


def _dsa(iqp, ikd, qd, kd, z, batch, seq, tq, k_top):
    t = z.shape[0]
    nq = seq // tq
    f32 = jnp.float32
    expand = np.zeros((LANES, IDX_HEADS * LANES), np.float32)
    for h in range(IDX_HEADS):
        expand[h, LANES * h:LANES * (h + 1)] = 1.0
    expand = jnp.asarray(expand, jnp.bfloat16)
    return pl.pallas_call(
        functools.partial(_dsa_kernel, k_top=k_top),
        out_shape=jax.ShapeDtypeStruct((t, DSA_WIDTH), jnp.bfloat16),
        grid=(batch, nq),
        in_specs=[
            pl.BlockSpec((tq, IDX_HEADS * 128), lambda b, i: (b * nq + i, 0)),
            pl.BlockSpec((tq, 128), lambda b, i: (b * nq + i, Z_IW // 128)),
            pl.BlockSpec((seq, 128), lambda b, i: (b, 0)),
            pl.BlockSpec((DSA_HEADS, tq, 128), lambda b, i: (0, b * nq + i, 0)),
            pl.BlockSpec((DSA_KV_HEADS, seq, 128), lambda b, i: (0, b, 0)),
            pl.BlockSpec((seq, 128), lambda b, i: (b, Z_DV // 128)),
            pl.BlockSpec((seq, 128), lambda b, i: (b, Z_DV // 128 + 1)),
            pl.BlockSpec((tq, DSA_WIDTH), lambda b, i: (b * nq + i, Z_DSAZ // DSA_WIDTH)),
            pl.BlockSpec(expand.shape, lambda b, i: (0, 0)),
        ],
        out_specs=pl.BlockSpec((tq, DSA_WIDTH), lambda b, i: (b * nq + i, 0)),
        scratch_shapes=[
            pltpu.VMEM((tq, seq), jnp.int32),
            pltpu.VMEM((seq, tq), jnp.int16),
            pltpu.VMEM((seq, tq), jnp.int16),
            pltpu.VMEM((tq, IDX_HEADS * LANES), f32),
            pltpu.VMEM((tq, KC), f32),
            pltpu.VMEM((DSA_HEADS, tq, LANES), f32),
            pltpu.VMEM((DSA_HEADS, tq, 2 * DSA_HEAD_DIM), f32),
        ],
        compiler_params=_cparams(("parallel", "arbitrary")),
        name="dsa",
    )(iqp, z, ikd, qd, kd, z, z, z, expand)


def _post_kernel(h_ref, ya_ref, yc_ref, cx_ref, cb_ref, cc_ref, cz_ref, hx_ref, hc_ref, cw_ref, p_ref, g_ref,
                 wa_ref, wb_ref, wc_ref, wg_ref, wp_ref, o_ref, *, tiles_per_seq):
    yb = _gated_conv(cx_ref, cb_ref, cc_ref, cz_ref, hx_ref, hc_ref, cw_ref, tiles_per_seq).astype(jnp.bfloat16)
    h1 = h_ref[...] + _dot(ya_ref[...], wa_ref[...])
    h1 = h1 + _dot(yb, wb_ref[...])
    h1 = h1 + _dot(yc_ref[...], wc_ref[...])
    r = lax.rsqrt(jnp.mean(h1 * h1, axis=-1, keepdims=True) + EPS)
    a = (h1 * r * g_ref[...]).astype(jnp.bfloat16)
    gate = _dot(a, wg_ref[...])
    gate = 1.0 / (1.0 + jnp.exp(-gate))
    o_ref[...] = h1 + gate * _dot(p_ref[0].astype(jnp.bfloat16), wp_ref[...])


def _post(h, ya, yc, z, conv_w, seq, p, layer, g, wa, wb, wc, wg, wp, tm):
    t, d = h.shape
    rows = lambda a: pl.BlockSpec((tm, a.shape[1]), lambda i: (i, 0))
    zcol = lambda c: pl.BlockSpec((tm, CONV_CH), lambda i, c=c: (i, c // CONV_CH))
    halo = lambda c: pl.BlockSpec((8, CONV_CH), lambda i, c=c: (jnp.maximum(i * (tm // 8) - 1, 0), c // CONV_CH))
    whole = lambda a: pl.BlockSpec(a.shape, lambda i: (0, 0), pipeline_mode=pl.Buffered(1))
    return pl.pallas_call(
        functools.partial(_post_kernel, tiles_per_seq=seq // tm),
        out_shape=jax.ShapeDtypeStruct((t, d), jnp.float32),
        grid=(t // tm,),
        in_specs=[rows(h), rows(ya), rows(yc),
                  zcol(Z_CX), zcol(Z_CB), zcol(Z_CC), zcol(Z_CONVZ), halo(Z_CX), halo(Z_CC), whole(conv_w),
                  pl.BlockSpec((1, tm, p.shape[2]), lambda i: (layer, i, 0)), whole(g),
                  whole(wa), whole(wb), whole(wc), whole(wg), whole(wp)],
        out_specs=rows(h),
        compiler_params=_cparams(("parallel",)),
        name="post",
    )(h, ya, yc, z, z, z, z, z, z, conv_w, p, g, wa, wb, wc, wg, wp)


_REGROUP = (
    (Z_CQ, _O_CQ, 512), (Z_CKV, _O_CKV, 512),
    (Z_CX, _O_CX, 512), (Z_CB, _O_CB, 512), (Z_CC, _O_CC, 512), (Z_CONVZ, _O_CONVZ, 512),
    (Z_MLAZ, _O_MLAZ, 768), (Z_DQ, _O_DQ, 768), (Z_DSAZ, _O_DSAZ, 768),
    (Z_DK, _O_DK, 256), (Z_DV, _O_DV, 256), (Z_IQ, _O_IQ, 1024),
)


def _regroup_kernel(w_ref, o_ref):
    bf16 = jnp.bfloat16
    cols = w_ref.shape[2]
    for dst, src, n in _REGROUP + ((Z_KPE_IK, _O_KPE, MLA_ROPE), (Z_KPE_IK + MLA_ROPE, _O_IK, IDX_DIM),
                                   (Z_IW, _O_IW, IDX_HEADS)):
        o_ref[0, dst:dst + n, :] = w_ref[0, src:src + n, :].astype(bf16)
    o_ref[0, Z_IW + IDX_HEADS:Z_IQ, :] = jnp.zeros((Z_IQ - Z_IW - IDX_HEADS, cols), bf16)


def _regroup_w_in(w, tc=TC_REGROUP):
    depth, d, n = w.shape
    return pl.pallas_call(
        _regroup_kernel,
        out_shape=jax.ShapeDtypeStruct((depth, Z_WIDTH, d), jnp.bfloat16),
        grid=(depth, d // tc),
        in_specs=[pl.BlockSpec((1, n, tc), lambda l, i: (l, 0, i))],
        out_specs=pl.BlockSpec((1, Z_WIDTH, tc), lambda l, i: (l, 0, i)),
        compiler_params=_cparams(("parallel", "parallel")),
        name="regroup",
    )(jnp.swapaxes(w, 1, 2))


def _pair_layout(a):
    lead = a.shape[:-1]
    a = a.reshape(lead + (MLA_HEADS, MLA_QK))
    nope, x1, x2 = a[..., :MLA_NOPE], a[..., MLA_NOPE:MLA_NOPE + 32], a[..., MLA_NOPE + 32:]
    return jnp.concatenate([nope, x1, x2, x2, x1], axis=-1).reshape(lead + (MLA_HEADS * 256,))


_B_C32, _B_S32, _B_C16, _B_S16, _B_C8, _B_S8, _B_ONE, _B_ROWS = 0, 32, 64, 80, 96, 104, 112, 128


def _table_selector():
    sel = np.zeros((_B_ROWS, N_TABLES * LANES), np.float32)

    def put(table, lane, base, n, coef=1.0):
        for j in range(n):
            sel[base + (j if base != _B_ONE else 0), table * LANES + lane + j] = coef

    c32, s32, c16, s16, c8, s8, one = _B_C32, _B_S32, _B_C16, _B_S16, _B_C8, _B_S8, _B_ONE
    put(TB_CSQ, 0, c32, 32); put(TB_CSQ, 32, c32, 32); put(TB_CSQ, 64, s32, 32, -1.0); put(TB_CSQ, 96, s32, 32)
    put(TB_C46, 0, c32, 32); put(TB_C46, 32, c32, 32); put(TB_C46, 64, c8, 8); put(TB_C46, 72, c8, 8)
    put(TB_C46, 80, one, 48)
    put(TB_S1A, 0, s32, 32, -1.0)
    put(TB_S2A, 32, s32, 32)
    put(TB_S1B, 64, s8, 8, -1.0)
    put(TB_S2B, 72, s8, 8)
    put(TB_CD, 0, c16, 16); put(TB_CD, 16, c16, 16); put(TB_CD, 32, one, 96)
    put(TB_S1D, 0, s16, 16, -1.0)
    put(TB_S2D, 16, s16, 16)
    qs = IDX_DIM ** -0.5
    for off in (0, 64):
        put(TB_CI, off, c8, 8, qs); put(TB_CI, off + 8, c8, 8, qs); put(TB_CI, off + 16, one, 48, qs)
        put(TB_S1I, off, s8, 8, -qs)
        put(TB_S2I, off + 8, s8, 8, qs)
    return sel


def _rope_basis(positions):
    f32 = jnp.float32
    pos = positions.reshape(-1, 1).astype(f32)
    freqs = [ROPE_THETA ** (-jnp.arange(half, dtype=f32) / half) for half in (MLA_ROPE // 2, DSA_ROT // 2, IDX_ROT // 2)]
    inv = jnp.concatenate([freqs[0], freqs[0], freqs[1], freqs[1], freqs[2], freqs[2], jnp.zeros((_B_ROWS - _B_ONE,), f32)])
    is_cos = np.zeros((_B_ROWS,), bool)
    for start, stop in ((_B_C32, _B_S32), (_B_C16, _B_S16), (_B_C8, _B_S8), (_B_ONE, _B_ONE + 1)):
        is_cos[start:stop] = True
    ang = pos * inv[None, :]
    basis = jnp.where(jnp.asarray(is_cos)[None, :], jnp.cos(ang), jnp.sin(ang))
    return basis, jnp.asarray(_table_selector(), jnp.bfloat16)


def _layer(h, layer, p_all, w_in_all, tabs, batch, seq, k_top, norm_in, mla_gq, mla_w_uq, mla_gkv, mla_w_ukv, mla_qn,
           mla_kn, conv_w, dsa_qn, dsa_kn, w_out, ple_norm, ple_w_gate, ple_w_proj):
    f32, bf16 = jnp.float32, jnp.bfloat16
    t = h.shape[0]

    z = _inproj(h, norm_in[None, :], w_in_all, layer, min(TM_INPROJ, t), TN_INPROJ)

    gq = _pair_layout(jnp.tile(mla_qn, MLA_HEADS))[None, :] * (MLA_QK ** -0.5 * LOG2E)
    g46 = jnp.concatenate([mla_kn[MLA_NOPE:], jnp.ones((64,), f32)])[None, :]
    qm, km, vm, qd, kd, iqp, ikd = _prep(
        z, *tabs, mla_gq[None, :], mla_gkv[None, :], _pair_layout(mla_w_uq).astype(bf16), mla_w_ukv.astype(bf16),
        gq, mla_kn[None, :MLA_NOPE], g46, dsa_qn[None, :] * (DSA_HEAD_DIM ** -0.5 * LOG2E), dsa_kn[None, :],
        min(TM_PREP, t))

    y_a = _mla(qm, km, vm, z, batch, seq, TK_MLA)
    y_c = _dsa(iqp, ikd, qd, kd, z, batch, seq, TQ_DSA, k_top)

    wo = w_out.astype(bf16)
    return _post(h, y_a, y_c, z, jnp.pad(conv_w, ((0, 8 - CONV_K), (0, 0))), seq, p_all, layer, ple_norm[None, :],
                 wo[:MLA_WIDTH], wo[MLA_WIDTH:MLA_WIDTH + CONV_CH], wo[MLA_WIDTH + CONV_CH:],
                 ple_w_gate.astype(bf16), ple_w_proj.astype(bf16), min(TM_POST, t))


def kernel(x, p, positions, norm_in, w_in, mla_gq, mla_w_uq, mla_gkv, mla_w_ukv, mla_qn, mla_kn, conv_w, dsa_qn,
           dsa_kn, w_out, ple_norm, ple_w_gate, ple_w_proj):
    batch, seq, d = x.shape
    depth = p.shape[0]
    t = batch * seq
    k_top = min(TOPK_MAX, seq // 4)
    tabs = _rope_basis(positions)
    w_in_z = _regroup_w_in(w_in)
    h = x.reshape(t, d)
    p_all = p.reshape(depth, t, PLE_DIM)
    for i in range(depth):
        h = _layer(h, i, p_all, w_in_z, tabs, batch, seq, k_top, norm_in[i], mla_gq[i],
                   mla_w_uq[i], mla_gkv[i], mla_w_ukv[i], mla_qn[i], mla_kn[i], conv_w[i], dsa_qn[i], dsa_kn[i],
                   w_out[i], ple_norm[i], ple_w_gate[i], ple_w_proj[i])
    return h.reshape(batch, seq, d)
```

```python
import functools

import numpy as np
import jax
import jax.numpy as jnp
from jax import lax
from jax.experimental import pallas as pl
from jax.experimental.pallas import tpu as pltpu

D_MODEL = 2048
PLE_DIM = 256
ROPE_THETA = 500000.0
EPS = 1e-6

MLA_HEADS = 6
MLA_Q_LORA = 512
MLA_KV_LORA = 512
MLA_NOPE = 128
MLA_ROPE = 64
MLA_V = 128
MLA_QK = MLA_NOPE + MLA_ROPE
MLA_WIDTH = MLA_HEADS * MLA_V

CONV_CH = 512
CONV_K = 3

DSA_HEADS = 6
DSA_KV_HEADS = 2
DSA_GROUP = DSA_HEADS // DSA_KV_HEADS
DSA_HEAD_DIM = 128
DSA_WIDTH = DSA_HEADS * DSA_HEAD_DIM
DSA_ROT = DSA_HEAD_DIM // 4
IDX_HEADS = 16
IDX_DIM = 64
IDX_ROT = IDX_DIM // 4
TOPK_MAX = 256

LANES = 128

_O_CQ, _O_CKV, _O_KPE, _O_MLAZ = 0, 512, 1024, 1088
_O_CX, _O_CB, _O_CC, _O_CONVZ = 1856, 2368, 2880, 3392
_O_DQ, _O_DK, _O_DV, _O_DSAZ = 3904, 4672, 4928, 5184
_O_IQ, _O_IW, _O_IK, _N_IN = 5952, 6976, 6992, 7056

Z_CQ, Z_CKV = 0, 512
Z_CX, Z_CB, Z_CC, Z_CONVZ = 1024, 1536, 2048, 2560
Z_MLAZ, Z_DQ, Z_DSAZ = 3072, 3840, 4608
Z_DK, Z_DV = 5376, 5632
Z_KPE_IK, Z_IW, Z_IQ = 5888, 6016, 6144
Z_WIDTH = 7168

LOG2E = 1.4426950408889634
INT_MIN = -(2 ** 31)
NEG_BIG = -1e30

VMEM_LIMIT = 56 * 1024 * 1024

TM_INPROJ, TN_INPROJ = 1024, Z_WIDTH // 4
TC_REGROUP = 256
TM_PREP = 512
TK_MLA = 512
TQ_DSA = 512
TM_POST = 512


def _cparams(sem):
    return pltpu.CompilerParams(dimension_semantics=sem, vmem_limit_bytes=VMEM_LIMIT)


def _dot(a, b):
    return jnp.dot(a, b, preferred_element_type=jnp.float32)


def _dot_nt(a, b):
    return lax.dot_general(a, b, (((1,), (1,)), ((), ())), preferred_element_type=jnp.float32)


def _roll(x, shift):
    return pltpu.roll(x, shift, 1)


def _inproj_kernel(x_ref, g_ref, w_ref, o_ref, a_ref):
    @pl.when(pl.program_id(1) == 0)
    def _():
        x = x_ref[...]
        r = lax.rsqrt(jnp.mean(x * x, axis=-1, keepdims=True) + EPS)
        a_ref[...] = (x * r * g_ref[...]).astype(jnp.bfloat16)

    o_ref[...] = _dot_nt(a_ref[...], w_ref[0]).astype(o_ref.dtype)


def _inproj(h, g, w, layer, tm, tn):
    t, d = h.shape
    n = w.shape[1]
    return pl.pallas_call(
        _inproj_kernel,
        out_shape=jax.ShapeDtypeStruct((t, n), jnp.bfloat16),
        grid=(t // tm, n // tn),
        in_specs=[
            pl.BlockSpec((tm, d), lambda i, j: (i, 0)),
            pl.BlockSpec((1, d), lambda i, j: (0, 0)),
            pl.BlockSpec((1, tn, d), lambda i, j: (layer, j, 0)),
        ],
        out_specs=pl.BlockSpec((tm, tn), lambda i, j: (i, j)),
        scratch_shapes=[pltpu.VMEM((tm, d), jnp.bfloat16)],
        compiler_params=_cparams(("parallel", "arbitrary")),
        name="inproj",
    )(h, g, w)


TB_CSQ = 0
TB_C46, TB_S1A, TB_S2A, TB_S1B, TB_S2B = 1, 2, 3, 4, 5
TB_CD, TB_S1D, TB_S2D = 6, 7, 8
TB_CI, TB_S1I, TB_S2I = 9, 10, 11
N_TABLES = 12


def _prep_kernel(cq_ref, ckv_ref, t46_ref, dq_ref, dk_ref, iq_ref, basis_ref, sel_ref,
                 glq_ref, glkv_ref, wuq_ref, wukv_ref, gq_ref, gkn_ref, g46_ref, gdq_ref, gdk_ref,
                 qm_ref, km_ref, vm_ref, qd_ref, kd_ref, iqp_ref, ikd_ref):
    f32, bf16 = jnp.float32, jnp.bfloat16
    tm = cq_ref.shape[0]
    lane = lax.broadcasted_iota(jnp.int32, (tm, LANES), 1)
    low = lane < 64

    b0 = basis_ref[...]
    b_hi = b0.astype(bf16)
    b1 = b0 - b_hi.astype(f32)
    b_mid = b1.astype(bf16)
    b_lo = (b1 - b_mid.astype(f32)).astype(bf16)
    tabs = _dot(b_hi, sel_ref[...]) + _dot(b_mid, sel_ref[...]) + _dot(b_lo, sel_ref[...])
    tab = lambda k: tabs[:, LANES * k:LANES * (k + 1)]

    def rms(x, n):
        return lax.rsqrt(jnp.sum(x * x, axis=-1, keepdims=True) * (1.0 / n) + EPS)

    cq = cq_ref[...].astype(f32)
    aq = (cq * rms(cq, MLA_Q_LORA) * glq_ref[...]).astype(bf16)
    qx = _dot(aq, wuq_ref[...])
    csq = tab(TB_CSQ)
    for h in range(MLA_HEADS):
        nope = qx[:, 256 * h:256 * h + 128]
        pair = qx[:, 256 * h + 128:256 * h + 256]
        ss = jnp.sum(nope * nope, axis=-1, keepdims=True) + 0.5 * jnp.sum(pair * pair, axis=-1, keepdims=True)
        r = lax.rsqrt(ss * (1.0 / MLA_QK) + EPS)
        qm_ref[h, :, 0:128] = (nope * r * gq_ref[:, 256 * h:256 * h + 128]).astype(bf16)
        qm_ref[h, :, 128:256] = (pair * r * gq_ref[:, 256 * h + 128:256 * h + 256] * csq).astype(bf16)

    t46 = t46_ref[...].astype(f32)
    ss_pe = jnp.sum(jnp.where(low, t46 * t46, 0.0), axis=-1, keepdims=True)
    y46 = t46 * g46_ref[...]
    r46 = (y46 * tab(TB_C46)
           + _roll(y46, 96) * tab(TB_S1A) + _roll(y46, 32) * tab(TB_S2A)
           + _roll(y46, 120) * tab(TB_S1B) + _roll(y46, 8) * tab(TB_S2B))
    sw = _roll(r46, 64)
    kpe2 = jnp.where(low, r46, sw)
    ikd_ref[...] = jnp.where(low, sw, r46).astype(bf16)

    ckv = ckv_ref[...].astype(f32)
    akv = (ckv * rms(ckv, MLA_KV_LORA) * glkv_ref[...]).astype(bf16)
    kvx = _dot(akv, wukv_ref[...])
    for h in range(MLA_HEADS):
        kn = kvx[:, 256 * h:256 * h + 128]
        ss = jnp.sum(kn * kn, axis=-1, keepdims=True) + ss_pe
        r = lax.rsqrt(ss * (1.0 / MLA_QK) + EPS)
        km_ref[h, :, 0:128] = (kn * r * gkn_ref[...]).astype(bf16)
        km_ref[h, :, 128:256] = (kpe2 * r).astype(bf16)
        vm_ref[h] = kvx[:, 256 * h + 128:256 * h + 256].astype(bf16)

    cd, s1d, s2d = tab(TB_CD), tab(TB_S1D), tab(TB_S2D)

    def dsa_head(x, g):
        y = x * rms(x, DSA_HEAD_DIM) * g
        return y * cd + _roll(y, 112) * s1d + _roll(y, 16) * s2d

    for h in range(DSA_HEADS):
        x = dq_ref[:, 128 * h:128 * h + 128].astype(f32)
        qd_ref[h] = dsa_head(x, gdq_ref[...]).astype(bf16)
    for g in range(DSA_KV_HEADS):
        x = dk_ref[:, 128 * g:128 * g + 128].astype(f32)
        kd_ref[g] = dsa_head(x, gdk_ref[...]).astype(bf16)

    ci, s1i, s2i = tab(TB_CI), tab(TB_S1I), tab(TB_S2I)
    for j in range(IDX_HEADS // 2):
        x = iq_ref[:, 128 * j:128 * j + 128].astype(f32)
        y = x * ci + _roll(x, 120) * s1i + _roll(x, 8) * s2i
        iqp_ref[:, 256 * j:256 * j + 128] = jnp.where(low, y, 0.0).astype(bf16)
        iqp_ref[:, 256 * j + 128:256 * j + 256] = jnp.where(low, 0.0, y).astype(bf16)


def _prep(z, basis, sel, glq, glkv, wuq, wukv, gq, gkn, g46, gdq, gdk, tm):
    t = z.shape[0]
    bf16 = jnp.bfloat16
    row = lambda w, c: pl.BlockSpec((tm, w), lambda i, c=c: (i, c))
    full = lambda a: pl.BlockSpec(a.shape, lambda i: (0,) * a.ndim)
    heads = lambda n, w: pl.BlockSpec((n, tm, w), lambda i: (0, i, 0))
    return pl.pallas_call(
        _prep_kernel,
        out_shape=(
            jax.ShapeDtypeStruct((MLA_HEADS, t, 256), bf16),
            jax.ShapeDtypeStruct((MLA_HEADS, t, 256), bf16),
            jax.ShapeDtypeStruct((MLA_HEADS, t, 128), bf16),
            jax.ShapeDtypeStruct((DSA_HEADS, t, 128), bf16),
            jax.ShapeDtypeStruct((DSA_KV_HEADS, t, 128), bf16),
            jax.ShapeDtypeStruct((t, IDX_HEADS * 128), bf16),
            jax.ShapeDtypeStruct((t, 128), bf16),
        ),
        grid=(t // tm,),
        in_specs=[
            row(512, Z_CQ // 512), row(512, Z_CKV // 512), row(128, Z_KPE_IK // 128),
            row(768, Z_DQ // 768), row(256, Z_DK // 256), row(1024, Z_IQ // 1024),
            pl.BlockSpec((tm, LANES), lambda i: (i, 0)), full(sel),
            full(glq), full(glkv), full(wuq), full(wukv), full(gq), full(gkn), full(g46), full(gdq), full(gdk),
        ],
        out_specs=(
            heads(MLA_HEADS, 256), heads(MLA_HEADS, 256), heads(MLA_HEADS, 128),
            heads(DSA_HEADS, 128), heads(DSA_KV_HEADS, 128),
            pl.BlockSpec((tm, IDX_HEADS * 128), lambda i: (i, 0)),
            pl.BlockSpec((tm, 128), lambda i: (i, 0)),
        ),
        compiler_params=_cparams(("parallel",)),
        name="prep",
    )(z, z, z, z, z, z, basis, sel, glq, glkv, wuq, wukv, gq, gkn, g46, gdq, gdk)


def _silu(x):
    return x / (1.0 + jnp.exp(-x))


def _gated_conv(cx_ref, cb_ref, cc_ref, cz_ref, hx_ref, hc_ref, w_ref, tiles_per_seq):
    f32 = jnp.float32
    tm, ch = cx_ref.shape
    u = cc_ref[...].astype(f32) * cx_ref[...].astype(f32)
    keep = (pl.program_id(0) % tiles_per_seq != 0).astype(f32)
    hu = hc_ref[...].astype(f32) * hx_ref[...].astype(f32) * keep
    row = lax.broadcasted_iota(jnp.int32, (tm, ch), 0)
    u1 = jnp.where(row == 0, hu[7:8, :], pltpu.roll(u, 1, 0))
    u2 = jnp.where(row == 0, hu[6:7, :], jnp.where(row == 1, hu[7:8, :], pltpu.roll(u, 2, 0)))
    conv = w_ref[0:1, :] * u2 + w_ref[1:2, :] * u1 + w_ref[2:3, :] * u
    return cb_ref[...].astype(f32) * conv * _silu(cz_ref[...].astype(f32))


def _tile_lanes(x, n):
    return x if n == 1 else jnp.concatenate([x] * n, axis=1)


def _softmax_step(s, v2, m_ref, acc_ref):
    m_prev = m_ref[...]
    m_new = jnp.maximum(m_prev, jnp.max(s, axis=-1, keepdims=True))
    p = jnp.exp2(s - _tile_lanes(m_new, s.shape[1] // LANES))
    alpha = jnp.exp2(m_prev - m_new)
    acc_ref[...] = _tile_lanes(alpha, 2) * acc_ref[...] + _dot(p.astype(jnp.bfloat16), v2)
    m_ref[...] = m_new


def _with_ones(v):
    return jnp.concatenate([v, jnp.ones_like(v)], axis=1)


MLA_HEADS_PER_STEP = 6


def _mla_kernel(q_ref, k_ref, v_ref, zg_ref, o_ref, m_ref, acc_ref, *, tk):
    f32 = jnp.float32
    qi = pl.program_id(2)
    m_ref[...] = jnp.full(m_ref.shape, NEG_BIG, f32)
    acc_ref[...] = jnp.zeros(acc_ref.shape, f32)

    def step(ki, row0, masked):
        keys = pl.ds(pl.multiple_of(ki * tk, tk), tk)
        rows = pl.ds(row0, q_ref.shape[1] - row0)
        for h in range(MLA_HEADS_PER_STEP):
            s = _dot_nt(q_ref[h, rows, :], k_ref[h, keys, :])
            if masked:
                r = lax.broadcasted_iota(jnp.int32, s.shape, 0)
                c = lax.broadcasted_iota(jnp.int32, s.shape, 1)
                s = jnp.where(c <= r, s, 2.0 * NEG_BIG)
            _softmax_step(s, _with_ones(v_ref[h, keys, :]), m_ref.at[h, rows, :], acc_ref.at[h, rows, :])

    def body(ki, c):
        step(ki, 0, False)
        return c

    lax.fori_loop(0, 2 * qi, body, 0)
    step(2 * qi, 0, True)
    step(2 * qi + 1, tk, True)
    for h in range(MLA_HEADS_PER_STEP):
        o = acc_ref[h, :, 0:128] / acc_ref[h, :, 128:256]
        zg = zg_ref[:, 128 * h:128 * h + 128].astype(f32)
        o_ref[:, 128 * h:128 * h + 128] = (o * _silu(zg)).astype(o_ref.dtype)


def _mla(qm, km, vm, z, batch, seq, tk):
    t = z.shape[0]
    tq = 2 * tk
    nq = seq // tq
    hs = MLA_HEADS_PER_STEP
    resident = dict(pipeline_mode=pl.Buffered(1))
    return pl.pallas_call(
        functools.partial(_mla_kernel, tk=tk),
        out_shape=jax.ShapeDtypeStruct((t, MLA_WIDTH), jnp.bfloat16),
        grid=(batch, MLA_HEADS // hs, nq),
        in_specs=[
            pl.BlockSpec((hs, tq, 256), lambda b, h, i: (h, b * nq + i, 0)),
            pl.BlockSpec((hs, seq, 256), lambda b, h, i: (h, b, 0), **resident),
            pl.BlockSpec((hs, seq, 128), lambda b, h, i: (h, b, 0), **resident),
            pl.BlockSpec((tq, 128 * hs), lambda b, h, i: (b * nq + i, Z_MLAZ // (128 * hs) + h)),
        ],
        out_specs=pl.BlockSpec((tq, 128 * hs), lambda b, h, i: (b * nq + i, h)),
        scratch_shapes=[pltpu.VMEM((hs, tq, LANES), jnp.float32), pltpu.VMEM((hs, tq, 256), jnp.float32)],
        compiler_params=_cparams(("parallel", "parallel", "arbitrary")),
        name="mla",
    )(qm, km, vm, z)


KC = 256
ATT_KC = 512
COUNT_FANIN = 8
SCORE_ROWS = 256


def _dsa_kernel(iq_ref, iw_ref, ik_ref, q_ref, k_ref, v0_ref, v1_ref, zg_ref, expand_ref, o_ref,
                key_ref, hi_ref, lo_ref, wb_ref, stage_ref, m_ref, acc_ref, *, k_top):
    f32, i32, bf16 = jnp.float32, jnp.int32, jnp.bfloat16
    tq = iq_ref.shape[0]
    i = pl.program_id(1)
    n_chunks = ((i + 1) * tq + KC - 1) // KC
    q_pos = i * tq + lax.broadcasted_iota(i32, (tq, KC), 0)
    lane_kc = lax.broadcasted_iota(i32, (tq, KC), 1)

    w = (iw_ref[...].astype(f32) * (IDX_HEADS ** -0.5)).astype(bf16)
    wb_ref[...] = _dot(w, expand_ref[...])

    def emit_keys(c):
        start = pl.multiple_of(c * KC, KC)
        bits = lax.bitcast_convert_type(stage_ref[...], i32)
        key = jnp.where(bits < 0, bits ^ jnp.int32(0x7FFFFFFF), bits)
        key = jnp.where(start + lane_kc <= q_pos, key, jnp.int32(INT_MIN))
        key_ref[:, pl.ds(start, KC)] = key
        key_t = key.T
        hi_ref[pl.ds(start, KC), :] = (key_t >> 16).astype(jnp.int16)
        lo_ref[pl.ds(start, KC), :] = ((key_t & 0xFFFF) - 32768).astype(jnp.int16)

    def score_chunk(c, first_row=0):
        emit_keys(jnp.maximum(c - 1, 0))
        ikc = ik_ref[pl.ds(pl.multiple_of(c * KC, KC), KC), :]
        for r0 in range(first_row, tq, SCORE_ROWS):
            rows = pl.ds(r0, SCORE_ROWS)
            acc = jnp.zeros((SCORE_ROWS, KC), f32)
            for h in range(IDX_HEADS):
                lg = _dot_nt(iq_ref[rows, 128 * h:128 * h + 128], ikc)
                acc = acc + _tile_lanes(wb_ref[rows, 128 * h:128 * h + 128], KC // LANES) * jnp.maximum(lg, 0.0)
            stage_ref[rows, :] = acc

    stage_ref[...] = jnp.zeros((tq, KC), f32)
    n_full = (i * tq) // KC + 1
    lax.fori_loop(0, n_full, lambda c, carry: (score_chunk(c), carry)[1], 0)
    for extra in range(1, tq // KC):
        score_chunk(n_full + extra - 1, first_row=(extra * KC // SCORE_ROWS) * SCORE_ROWS)
    emit_keys(n_chunks - 1)

    n_att = (n_chunks * KC + ATT_KC - 1) // ATT_KC

    def fill(c, carry):
        start = pl.multiple_of(c * KC, KC)
        key_ref[:, pl.ds(start, KC)] = jnp.full((tq, KC), INT_MIN, i32)
        hi_ref[pl.ds(start, KC), :] = jnp.full((KC, tq), -32768, jnp.int16)
        lo_ref[pl.ds(start, KC), :] = jnp.full((KC, tq), -32768, jnp.int16)
        return carry

    lax.fori_loop(n_chunks, n_att * (ATT_KC // KC), fill, 0)

    i16 = jnp.int16
    low16 = jnp.full((), -32768, i16)
    one16, zero16 = jnp.ones((), i16), jnp.zeros((), i16)

    def count16(ref, cand):
        c16 = cand.astype(i16)
        def body(c, cnt):
            start = pl.multiple_of(c * ATT_KC, ATT_KC)
            for g in range(0, ATT_KC, 16 * COUNT_FANIN):
                hit = jnp.where(ref[pl.ds(start + g, 16 * COUNT_FANIN), :] >= c16, one16, zero16)
                part = [hit[16 * j:16 * (j + 1)] for j in range(COUNT_FANIN)]
                while len(part) > 1:
                    part = [a + b for a, b in zip(part[0::2], part[1::2])]
                cnt = cnt + part[0]
            return cnt
        cnt = lax.fori_loop(0, n_att, body, jnp.zeros((16, tq), i16))
        return jnp.sum(cnt.astype(i32), axis=0, keepdims=True)

    def search16(ref, want):
        def step(it, carry):
            r, n_r = carry
            cand = r + lax.shift_left(jnp.int32(1), 15 - it)
            n = count16(ref, cand)
            ok = n >= want
            return jnp.where(ok, cand, r), jnp.where(ok, n, n_r)
        return lax.fori_loop(0, 16, step, (jnp.full((1, tq), -32768, i32), jnp.zeros((1, tq), i32)))

    t_hi, n_hi = search16(hi_ref, k_top)
    n_above = count16(hi_ref, t_hi + 1)
    t_hi16 = t_hi.astype(i16)

    def restrict(c, carry):
        rows = pl.ds(pl.multiple_of(c * ATT_KC, ATT_KC), ATT_KC)
        lo_ref[rows, :] = jnp.where(hi_ref[rows, :] == t_hi16, lo_ref[rows, :], low16)
        return carry

    lax.fori_loop(0, n_att, restrict, 0)
    t_lo, n_lo = search16(lo_ref, k_top - n_above)
    thr = t_hi * 65536 + (t_lo + 32768)

    n_ge = jnp.where(t_lo > -32768, n_above + n_lo, jnp.where(t_hi > -32768, n_hi, n_above))
    has_excess = jnp.max(n_ge) > k_top

    @pl.when(has_excess)
    def _():
        def count_where(hits):
            def body(c, cnt):
                start = pl.multiple_of(c * KC, KC)
                hit = hits(key_ref[:, pl.ds(start, KC)].T, start)
                return cnt + jnp.sum(hit.reshape(KC // 8, 8, tq), axis=0)
            cnt = lax.fori_loop(0, n_chunks, body, jnp.zeros((8, tq), i32))
            return jnp.sum(cnt, axis=0, keepdims=True)

        need = k_top - count_where(lambda kt, _: jnp.where(kt > thr, 1, 0))
        sub = lax.broadcasted_iota(i32, (KC, tq), 0)

        def idx_step(it, bound):
            cand = bound + lax.shift_left(jnp.int32(1), 30 - it)
            n_below = count_where(lambda kt, start: jnp.where(kt == thr, jnp.where(start + sub < cand, 1, 0), 0))
            return jnp.where(n_below <= need, cand, bound)

        bound = lax.fori_loop(0, 31, idx_step, jnp.zeros((1, tq), i32))

        def demote(c, carry):
            start = pl.multiple_of(c * KC, KC)
            kt = key_ref[:, pl.ds(start, KC)].T
            drop = jnp.where(kt == thr, jnp.where(start + sub >= bound, 1, 0), 0)
            key_ref[:, pl.ds(start, KC)] = jnp.where(drop == 1, jnp.int32(INT_MIN), kt).T
            return carry

        lax.fori_loop(0, n_chunks, demote, 0)

    thr_sel = jnp.maximum(thr, jnp.int32(INT_MIN + 1))
    thr_kc = _tile_lanes(jnp.broadcast_to(thr_sel, (LANES, tq)).T, KC // LANES)

    m_ref[...] = jnp.full(m_ref.shape, NEG_BIG, f32)
    acc_ref[...] = jnp.zeros(acc_ref.shape, f32)
    v_refs = (v0_ref, v1_ref)

    def attend(start, n_keys, row0=0):
        rows = pl.ds(row0, tq - row0)
        keys = pl.ds(start, n_keys)
        bias = jnp.where(key_ref[rows, keys] >= _tile_lanes(thr_kc[row0:], n_keys // KC), 0.0, 2.0 * NEG_BIG)
        for hd in range(DSA_HEADS):
            g = hd // DSA_GROUP
            s = _dot_nt(q_ref[hd, rows, :], k_ref[g, keys, :]) + bias
            _softmax_step(s, _with_ones(v_refs[g][keys, :]), m_ref.at[hd, rows, :], acc_ref.at[hd, rows, :])

    def attn_chunk(c, carry):
        attend(pl.multiple_of(c * ATT_KC, ATT_KC), ATT_KC)
        return carry

    lax.fori_loop(0, i, attn_chunk, 0)
    diag = pl.multiple_of(i * ATT_KC, ATT_KC)
    attend(diag, KC)
    attend(diag + KC, KC, row0=KC)

    for hd in range(DSA_HEADS):
        o = acc_ref[hd, :, 0:128] / acc_ref[hd, :, 128:256]
        zg = zg_ref[:, 128 * hd:128 * hd + 128].astype(f32)
        o_ref[:, 128 * hd:128 * hd + 128] = (o * _silu(zg)).astype(o_ref.dtype)


def _dsa(iqp, ikd, qd, kd, z, batch, seq, tq, k_top):
    t = z.shape[0]
    nq = seq // tq
    f32 = jnp.float32
    assert tq == ATT_KC == 2 * KC and seq % tq == 0, "the attention phase's diagonal handling assumes these tile ratios"
    expand = np.zeros((LANES, IDX_HEADS * LANES), np.float32)
    for h in range(IDX_HEADS):
        expand[h, LANES * h:LANES * (h + 1)] = 1.0
    expand = jnp.asarray(expand, jnp.bfloat16)
    return pl.pallas_call(
        functools.partial(_dsa_kernel, k_top=k_top),
        out_shape=jax.ShapeDtypeStruct((t, DSA_WIDTH), jnp.bfloat16),
        grid=(batch, nq),
        in_specs=[
            pl.BlockSpec((tq, IDX_HEADS * 128), lambda b, i: (b * nq + i, 0)),
            pl.BlockSpec((tq, 128), lambda b, i: (b * nq + i, Z_IW // 128)),
            pl.BlockSpec((seq, 128), lambda b, i: (b, 0)),
            pl.BlockSpec((DSA_HEADS, tq, 128), lambda b, i: (0, b * nq + i, 0)),
            pl.BlockSpec((DSA_KV_HEADS, seq, 128), lambda b, i: (0, b, 0)),
            pl.BlockSpec((seq, 128), lambda b, i: (b, Z_DV // 128)),
            pl.BlockSpec((seq, 128), lambda b, i: (b, Z_DV // 128 + 1)),
            pl.BlockSpec((tq, DSA_WIDTH), lambda b, i: (b * nq + i, Z_DSAZ // DSA_WIDTH)),
            pl.BlockSpec(expand.shape, lambda b, i: (0, 0)),
        ],
        out_specs=pl.BlockSpec((tq, DSA_WIDTH), lambda b, i: (b * nq + i, 0)),
        scratch_shapes=[
            pltpu.VMEM((tq, seq), jnp.int32),
            pltpu.VMEM((seq, tq), jnp.int16),
            pltpu.VMEM((seq, tq), jnp.int16),
            pltpu.VMEM((tq, IDX_HEADS * LANES), f32),
            pltpu.VMEM((tq, KC), f32),
            pltpu.VMEM((DSA_HEADS, tq, LANES), f32),
            pltpu.VMEM((DSA_HEADS, tq, 2 * DSA_HEAD_DIM), f32),
        ],
        compiler_params=_cparams(("parallel", "arbitrary")),
        name="dsa",
    )(iqp, z, ikd, qd, kd, z, z, z, expand)


def _post_kernel(h_ref, ya_ref, yc_ref, cx_ref, cb_ref, cc_ref, cz_ref, hx_ref, hc_ref, cw_ref, p_ref, g_ref,
                 wa_ref, wb_ref, wc_ref, wg_ref, wp_ref, o_ref, *, tiles_per_seq):
    yb = _gated_conv(cx_ref, cb_ref, cc_ref, cz_ref, hx_ref, hc_ref, cw_ref, tiles_per_seq).astype(jnp.bfloat16)
    h1 = h_ref[...] + _dot(ya_ref[...], wa_ref[...])
    h1 = h1 + _dot(yb, wb_ref[...])
    h1 = h1 + _dot(yc_ref[...], wc_ref[...])
    r = lax.rsqrt(jnp.mean(h1 * h1, axis=-1, keepdims=True) + EPS)
    a = (h1 * r * g_ref[...]).astype(jnp.bfloat16)
    gate = _dot(a, wg_ref[...])
    gate = 1.0 / (1.0 + jnp.exp(-gate))
    o_ref[...] = h1 + gate * _dot(p_ref[0].astype(jnp.bfloat16), wp_ref[...])


def _post(h, ya, yc, z, conv_w, seq, p, layer, g, wa, wb, wc, wg, wp, tm):
    t, d = h.shape
    rows = lambda a: pl.BlockSpec((tm, a.shape[1]), lambda i: (i, 0))
    zcol = lambda c: pl.BlockSpec((tm, CONV_CH), lambda i, c=c: (i, c // CONV_CH))
    halo = lambda c: pl.BlockSpec((8, CONV_CH), lambda i, c=c: (jnp.maximum(i * (tm // 8) - 1, 0), c // CONV_CH))
    whole = lambda a: pl.BlockSpec(a.shape, lambda i: (0, 0), pipeline_mode=pl.Buffered(1))
    return pl.pallas_call(
        functools.partial(_post_kernel, tiles_per_seq=seq // tm),
        out_shape=jax.ShapeDtypeStruct((t, d), jnp.float32),
        grid=(t // tm,),
        in_specs=[rows(h), rows(ya), rows(yc),
                  zcol(Z_CX), zcol(Z_CB), zcol(Z_CC), zcol(Z_CONVZ), halo(Z_CX), halo(Z_CC), whole(conv_w),
                  pl.BlockSpec((1, tm, p.shape[2]), lambda i: (layer, i, 0)), whole(g),
                  whole(wa), whole(wb), whole(wc), whole(wg), whole(wp)],
        out_specs=rows(h),
        compiler_params=_cparams(("parallel",)),
        name="post",
    )(h, ya, yc, z, z, z, z, z, z, conv_w, p, g, wa, wb, wc, wg, wp)


_REGROUP = (
    (Z_CQ, _O_CQ, 512), (Z_CKV, _O_CKV, 512),
    (Z_CX, _O_CX, 512), (Z_CB, _O_CB, 512), (Z_CC, _O_CC, 512), (Z_CONVZ, _O_CONVZ, 512),
    (Z_MLAZ, _O_MLAZ, 768), (Z_DQ, _O_DQ, 768), (Z_DSAZ, _O_DSAZ, 768),
    (Z_DK, _O_DK, 256), (Z_DV, _O_DV, 256), (Z_IQ, _O_IQ, 1024),
)


def _regroup_kernel(w_ref, o_ref):
    bf16 = jnp.bfloat16
    cols = w_ref.shape[2]
    for dst, src, n in _REGROUP + ((Z_KPE_IK, _O_KPE, MLA_ROPE), (Z_KPE_IK + MLA_ROPE, _O_IK, IDX_DIM),
                                   (Z_IW, _O_IW, IDX_HEADS)):
        o_ref[0, dst:dst + n, :] = w_ref[0, src:src + n, :].astype(bf16)
    o_ref[0, Z_IW + IDX_HEADS:Z_IQ, :] = jnp.zeros((Z_IQ - Z_IW - IDX_HEADS, cols), bf16)


def _regroup_w_in(w, tc=TC_REGROUP):
    depth, d, n = w.shape
    return pl.pallas_call(
        _regroup_kernel,
        out_shape=jax.ShapeDtypeStruct((depth, Z_WIDTH, d), jnp.bfloat16),
        grid=(depth, d // tc),
        in_specs=[pl.BlockSpec((1, n, tc), lambda l, i: (l, 0, i))],
        out_specs=pl.BlockSpec((1, Z_WIDTH, tc), lambda l, i: (l, 0, i)),
        compiler_params=_cparams(("parallel", "parallel")),
        name="regroup",
    )(jnp.swapaxes(w, 1, 2))


def _pair_layout(a):
    lead = a.shape[:-1]
    a = a.reshape(lead + (MLA_HEADS, MLA_QK))
    nope, x1, x2 = a[..., :MLA_NOPE], a[..., MLA_NOPE:MLA_NOPE + 32], a[..., MLA_NOPE + 32:]
    return jnp.concatenate([nope, x1, x2, x2, x1], axis=-1).reshape(lead + (MLA_HEADS * 256,))


_B_C32, _B_S32, _B_C16, _B_S16, _B_C8, _B_S8, _B_ONE, _B_ROWS = 0, 32, 64, 80, 96, 104, 112, 128


def _table_selector():
    sel = np.zeros((_B_ROWS, N_TABLES * LANES), np.float32)

    def put(table, lane, base, n, coef=1.0):
        for j in range(n):
            sel[base + (j if base != _B_ONE else 0), table * LANES + lane + j] = coef

    c32, s32, c16, s16, c8, s8, one = _B_C32, _B_S32, _B_C16, _B_S16, _B_C8, _B_S8, _B_ONE
    put(TB_CSQ, 0, c32, 32); put(TB_CSQ, 32, c32, 32); put(TB_CSQ, 64, s32, 32, -1.0); put(TB_CSQ, 96, s32, 32)
    put(TB_C46, 0, c32, 32); put(TB_C46, 32, c32, 32); put(TB_C46, 64, c8, 8); put(TB_C46, 72, c8, 8)
    put(TB_C46, 80, one, 48)
    put(TB_S1A, 0, s32, 32, -1.0)
    put(TB_S2A, 32, s32, 32)
    put(TB_S1B, 64, s8, 8, -1.0)
    put(TB_S2B, 72, s8, 8)
    put(TB_CD, 0, c16, 16); put(TB_CD, 16, c16, 16); put(TB_CD, 32, one, 96)
    put(TB_S1D, 0, s16, 16, -1.0)
    put(TB_S2D, 16, s16, 16)
    qs = IDX_DIM ** -0.5
    for off in (0, 64):
        put(TB_CI, off, c8, 8, qs); put(TB_CI, off + 8, c8, 8, qs); put(TB_CI, off + 16, one, 48, qs)
        put(TB_S1I, off, s8, 8, -qs)
        put(TB_S2I, off + 8, s8, 8, qs)
    return sel


def _rope_basis(positions):
    f32 = jnp.float32
    pos = positions.reshape(-1, 1).astype(f32)
    freqs = [ROPE_THETA ** (-jnp.arange(half, dtype=f32) / half) for half in (MLA_ROPE // 2, DSA_ROT // 2, IDX_ROT // 2)]
    inv = jnp.concatenate([freqs[0], freqs[0], freqs[1], freqs[1], freqs[2], freqs[2], jnp.zeros((_B_ROWS - _B_ONE,), f32)])
    is_cos = np.zeros((_B_ROWS,), bool)
    for start, stop in ((_B_C32, _B_S32), (_B_C16, _B_S16), (_B_C8, _B_S8), (_B_ONE, _B_ONE + 1)):
        is_cos[start:stop] = True
    ang = pos * inv[None, :]
    basis = jnp.where(jnp.asarray(is_cos)[None, :], jnp.cos(ang), jnp.sin(ang))
    return basis, jnp.asarray(_table_selector(), jnp.bfloat16)


def _layer(h, layer, p_all, w_in_all, tabs, batch, seq, k_top, norm_in, mla_gq, mla_w_uq, mla_gkv, mla_w_ukv, mla_qn,
           mla_kn, conv_w, dsa_qn, dsa_kn, w_out, ple_norm, ple_w_gate, ple_w_proj):
    f32, bf16 = jnp.float32, jnp.bfloat16
    t = h.shape[0]

    z = _inproj(h, norm_in[None, :], w_in_all, layer, min(TM_INPROJ, t), TN_INPROJ)

    gq = _pair_layout(jnp.tile(mla_qn, MLA_HEADS))[None, :] * (MLA_QK ** -0.5 * LOG2E)
    g46 = jnp.concatenate([mla_kn[MLA_NOPE:], jnp.ones((64,), f32)])[None, :]
    qm, km, vm, qd, kd, iqp, ikd = _prep(
        z, *tabs, mla_gq[None, :], mla_gkv[None, :], _pair_layout(mla_w_uq).astype(bf16), mla_w_ukv.astype(bf16),
        gq, mla_kn[None, :MLA_NOPE], g46, dsa_qn[None, :] * (DSA_HEAD_DIM ** -0.5 * LOG2E), dsa_kn[None, :],
        min(TM_PREP, t))

    y_a = _mla(qm, km, vm, z, batch, seq, TK_MLA)
    y_c = _dsa(iqp, ikd, qd, kd, z, batch, seq, TQ_DSA, k_top)

    wo = w_out.astype(bf16)
    return _post(h, y_a, y_c, z, jnp.pad(conv_w, ((0, 8 - CONV_K), (0, 0))), seq, p_all, layer, ple_norm[None, :],
                 wo[:MLA_WIDTH], wo[MLA_WIDTH:MLA_WIDTH + CONV_CH], wo[MLA_WIDTH + CONV_CH:],
                 ple_w_gate.astype(bf16), ple_w_proj.astype(bf16), min(TM_POST, t))


def kernel(x, p, positions, norm_in, w_in, mla_gq, mla_w_uq, mla_gkv, mla_w_ukv, mla_qn, mla_kn, conv_w, dsa_qn,
           dsa_kn, w_out, ple_norm, ple_w_gate, ple_w_proj):
    batch, seq, d = x.shape
    depth = p.shape[0]
    t = batch * seq
    k_top = min(TOPK_MAX, seq // 4)
    tabs = _rope_basis(positions)
    w_in_z = _regroup_w_in(w_in)
    h = x.reshape(t, d)
    p_all = p.reshape(depth, t, PLE_DIM)
    for i in range(depth):
        h = _layer(h, i, p_all, w_in_z, tabs, batch, seq, k_top, norm_in[i], mla_gq[i],
                   mla_w_uq[i], mla_gkv[i], mla_w_ukv[i], mla_qn[i], mla_kn[i], conv_w[i], dsa_qn[i], dsa_kn[i],
                   w_out[i], ple_norm[i], ple_w_gate[i], ple_w_proj[i])
    return h.reshape(batch, seq, d)
```

```python
import functools

import numpy as np
import jax
import jax.numpy as jnp
from jax import lax
from jax.experimental import pallas as pl
from jax.experimental.pallas import tpu as pltpu

D_MODEL = 2048
PLE_DIM = 256
ROPE_THETA = 500000.0
EPS = 1e-6

MLA_HEADS = 6
MLA_Q_LORA = 512
MLA_KV_LORA = 512
MLA_NOPE = 128
MLA_ROPE = 64
MLA_V = 128
MLA_QK = MLA_NOPE + MLA_ROPE
MLA_WIDTH = MLA_HEADS * MLA_V

CONV_CH = 512
CONV_K = 3

DSA_HEADS = 6
DSA_KV_HEADS = 2
DSA_GROUP = DSA_HEADS // DSA_KV_HEADS
DSA_HEAD_DIM = 128
DSA_WIDTH = DSA_HEADS * DSA_HEAD_DIM
DSA_ROT = DSA_HEAD_DIM // 4
IDX_HEADS = 16
IDX_DIM = 64
IDX_ROT = IDX_DIM // 4
TOPK_MAX = 256

LANES = 128

_O_CQ, _O_CKV, _O_KPE, _O_MLAZ = 0, 512, 1024, 1088
_O_CX, _O_CB, _O_CC, _O_CONVZ = 1856, 2368, 2880, 3392
_O_DQ, _O_DK, _O_DV, _O_DSAZ = 3904, 4672, 4928, 5184
_O_IQ, _O_IW, _O_IK, _N_IN = 5952, 6976, 6992, 7056

Z_CQ, Z_CKV = 0, 512
Z_CX, Z_CB, Z_CC, Z_CONVZ = 1024, 1536, 2048, 2560
Z_MLAZ, Z_DQ, Z_DSAZ = 3072, 3840, 4608
Z_DK, Z_DV = 5376, 5632
Z_KPE_IK, Z_IW, Z_IQ = 5888, 6016, 6144
Z_WIDTH = 7168

LOG2E = 1.4426950408889634
INT_MIN = -(2 ** 31)
NEG_BIG = -1e30

VMEM_LIMIT = 56 * 1024 * 1024

TM_INPROJ, TN_INPROJ = 1024, Z_WIDTH // 4
TC_REGROUP = 256
TM_PREP = 512
TK_MLA = 512
TQ_DSA = 512
TM_POST = 512


def _cparams(sem):
    return pltpu.CompilerParams(dimension_semantics=sem, vmem_limit_bytes=VMEM_LIMIT)


def _dot(a, b):
    return jnp.dot(a, b, preferred_element_type=jnp.float32)


def _dot_nt(a, b):
    return lax.dot_general(a, b, (((1,), (1,)), ((), ())), preferred_element_type=jnp.float32)


def _roll(x, shift):
    return pltpu.roll(x, shift, 1)


def _inproj_kernel(x_ref, g_ref, w_ref, o_ref, a_ref):
    @pl.when(pl.program_id(1) == 0)
    def _():
        x = x_ref[...]
        r = lax.rsqrt(jnp.mean(x * x, axis=-1, keepdims=True) + EPS)
        a_ref[...] = (x * r * g_ref[...]).astype(jnp.bfloat16)

    o_ref[...] = _dot_nt(a_ref[...], w_ref[0]).astype(o_ref.dtype)


def _inproj(h, g, w, layer, tm, tn):
    t, d = h.shape
    n = w.shape[1]
    return pl.pallas_call(
        _inproj_kernel,
        out_shape=jax.ShapeDtypeStruct((t, n), jnp.bfloat16),
        grid=(t // tm, n // tn),
        in_specs=[
            pl.BlockSpec((tm, d), lambda i, j: (i, 0)),
            pl.BlockSpec((1, d), lambda i, j: (0, 0)),
            pl.BlockSpec((1, tn, d), lambda i, j: (layer, j, 0)),
        ],
        out_specs=pl.BlockSpec((tm, tn), lambda i, j: (i, j)),
        scratch_shapes=[pltpu.VMEM((tm, d), jnp.bfloat16)],
        compiler_params=_cparams(("parallel", "arbitrary")),
        name="inproj",
    )(h, g, w)


TB_CSQ = 0
TB_C46, TB_S1A, TB_S2A, TB_S1B, TB_S2B = 1, 2, 3, 4, 5
TB_CD, TB_S1D, TB_S2D = 6, 7, 8
TB_CI, TB_S1I, TB_S2I = 9, 10, 11
N_TABLES = 12


def _prep_kernel(cq_ref, ckv_ref, t46_ref, dq_ref, dk_ref, iq_ref, basis_ref, sel_ref,
                 glq_ref, glkv_ref, wuq_ref, wukv_ref, gq_ref, gkn_ref, g46_ref, gdq_ref, gdk_ref,
                 qm_ref, km_ref, vm_ref, qd_ref, kd_ref, iqp_ref, ikd_ref):
    f32, bf16 = jnp.float32, jnp.bfloat16
    tm = cq_ref.shape[0]
    lane = lax.broadcasted_iota(jnp.int32, (tm, LANES), 1)
    low = lane < 64

    b0 = basis_ref[...]
    b_hi = b0.astype(bf16)
    b1 = b0 - b_hi.astype(f32)
    b_mid = b1.astype(bf16)
    b_lo = (b1 - b_mid.astype(f32)).astype(bf16)
    tabs = _dot(b_hi, sel_ref[...]) + _dot(b_mid, sel_ref[...]) + _dot(b_lo, sel_ref[...])
    tab = lambda k: tabs[:, LANES * k:LANES * (k + 1)]

    def rms(x, n):
        return lax.rsqrt(jnp.sum(x * x, axis=-1, keepdims=True) * (1.0 / n) + EPS)

    cq = cq_ref[...].astype(f32)
    aq = (cq * rms(cq, MLA_Q_LORA) * glq_ref[...]).astype(bf16)
    qx = _dot(aq, wuq_ref[...])
    csq = tab(TB_CSQ)
    for h in range(MLA_HEADS):
        nope = qx[:, 256 * h:256 * h + 128]
        pair = qx[:, 256 * h + 128:256 * h + 256]
        ss = jnp.sum(nope * nope, axis=-1, keepdims=True) + 0.5 * jnp.sum(pair * pair, axis=-1, keepdims=True)
        r = lax.rsqrt(ss * (1.0 / MLA_QK) + EPS)
        qm_ref[h, :, 0:128] = (nope * r * gq_ref[:, 256 * h:256 * h + 128]).astype(bf16)
        qm_ref[h, :, 128:256] = (pair * r * gq_ref[:, 256 * h + 128:256 * h + 256] * csq).astype(bf16)

    t46 = t46_ref[...].astype(f32)
    ss_pe = jnp.sum(jnp.where(low, t46 * t46, 0.0), axis=-1, keepdims=True)
    y46 = t46 * g46_ref[...]
    r46 = (y46 * tab(TB_C46)
           + _roll(y46, 96) * tab(TB_S1A) + _roll(y46, 32) * tab(TB_S2A)
           + _roll(y46, 120) * tab(TB_S1B) + _roll(y46, 8) * tab(TB_S2B))
    sw = _roll(r46, 64)
    kpe2 = jnp.where(low, r46, sw)
    ikd_ref[...] = jnp.where(low, sw, r46).astype(bf16)

    ckv = ckv_ref[...].astype(f32)
    akv = (ckv * rms(ckv, MLA_KV_LORA) * glkv_ref[...]).astype(bf16)
    kvx = _dot(akv, wukv_ref[...])
    for h in range(MLA_HEADS):
        kn = kvx[:, 256 * h:256 * h + 128]
        ss = jnp.sum(kn * kn, axis=-1, keepdims=True) + ss_pe
        r = lax.rsqrt(ss * (1.0 / MLA_QK) + EPS)
        km_ref[h, :, 0:128] = (kn * r * gkn_ref[...]).astype(bf16)
        km_ref[h, :, 128:256] = (kpe2 * r).astype(bf16)
        vm_ref[h] = kvx[:, 256 * h + 128:256 * h + 256].astype(bf16)

    cd, s1d, s2d = tab(TB_CD), tab(TB_S1D), tab(TB_S2D)

    def dsa_head(x, g):
        y = x * rms(x, DSA_HEAD_DIM) * g
        return y * cd + _roll(y, 112) * s1d + _roll(y, 16) * s2d

    for h in range(DSA_HEADS):
        x = dq_ref[:, 128 * h:128 * h + 128].astype(f32)
        qd_ref[h] = dsa_head(x, gdq_ref[...]).astype(bf16)
    for g in range(DSA_KV_HEADS):
        x = dk_ref[:, 128 * g:128 * g + 128].astype(f32)
        kd_ref[g] = dsa_head(x, gdk_ref[...]).astype(bf16)

    ci, s1i, s2i = tab(TB_CI), tab(TB_S1I), tab(TB_S2I)
    for j in range(IDX_HEADS // 2):
        x = iq_ref[:, 128 * j:128 * j + 128].astype(f32)
        y = x * ci + _roll(x, 120) * s1i + _roll(x, 8) * s2i
        iqp_ref[:, 256 * j:256 * j + 128] = jnp.where(low, y, 0.0).astype(bf16)
        iqp_ref[:, 256 * j + 128:256 * j + 256] = jnp.where(low, 0.0, y).astype(bf16)


def _prep(z, basis, sel, glq, glkv, wuq, wukv, gq, gkn, g46, gdq, gdk, tm):
    t = z.shape[0]
    bf16 = jnp.bfloat16
    row = lambda w, c: pl.BlockSpec((tm, w), lambda i, c=c: (i, c))
    full = lambda a: pl.BlockSpec(a.shape, lambda i: (0,) * a.ndim)
    heads = lambda n, w: pl.BlockSpec((n, tm, w), lambda i: (0, i, 0))
    return pl.pallas_call(
        _prep_kernel,
        out_shape=(
            jax.ShapeDtypeStruct((MLA_HEADS, t, 256), bf16),
            jax.ShapeDtypeStruct((MLA_HEADS, t, 256), bf16),
            jax.ShapeDtypeStruct((MLA_HEADS, t, 128), bf16),
            jax.ShapeDtypeStruct((DSA_HEADS, t, 128), bf16),
            jax.ShapeDtypeStruct((DSA_KV_HEADS, t, 128), bf16),
            jax.ShapeDtypeStruct((t, IDX_HEADS * 128), bf16),
            jax.ShapeDtypeStruct((t, 128), bf16),
        ),
        grid=(t // tm,),
        in_specs=[
            row(512, Z_CQ // 512), row(512, Z_CKV // 512), row(128, Z_KPE_IK // 128),
            row(768, Z_DQ // 768), row(256, Z_DK // 256), row(1024, Z_IQ // 1024),
            pl.BlockSpec((tm, LANES), lambda i: (i, 0)), full(sel),
            full(glq), full(glkv), full(wuq), full(wukv), full(gq), full(gkn), full(g46), full(gdq), full(gdk),
        ],
        out_specs=(
            heads(MLA_HEADS, 256), heads(MLA_HEADS, 256), heads(MLA_HEADS, 128),
            heads(DSA_HEADS, 128), heads(DSA_KV_HEADS, 128),
            pl.BlockSpec((tm, IDX_HEADS * 128), lambda i: (i, 0)),
            pl.BlockSpec((tm, 128), lambda i: (i, 0)),
        ),
        compiler_params=_cparams(("parallel",)),
        name="prep",
    )(z, z, z, z, z, z, basis, sel, glq, glkv, wuq, wukv, gq, gkn, g46, gdq, gdk)


def _silu(x):
    return x / (1.0 + jnp.exp(-x))


def _gated_conv(cx_ref, cb_ref, cc_ref, cz_ref, hx_ref, hc_ref, w_ref, tiles_per_seq):
    f32 = jnp.float32
    tm, ch = cx_ref.shape
    u = cc_ref[...].astype(f32) * cx_ref[...].astype(f32)
    keep = (pl.program_id(0) % tiles_per_seq != 0).astype(f32)
    hu = hc_ref[...].astype(f32) * hx_ref[...].astype(f32) * keep
    row = lax.broadcasted_iota(jnp.int32, (tm, ch), 0)
    u1 = jnp.where(row == 0, hu[7:8, :], pltpu.roll(u, 1, 0))
    u2 = jnp.where(row == 0, hu[6:7, :], jnp.where(row == 1, hu[7:8, :], pltpu.roll(u, 2, 0)))
    conv = w_ref[0:1, :] * u2 + w_ref[1:2, :] * u1 + w_ref[2:3, :] * u
    return cb_ref[...].astype(f32) * conv * _silu(cz_ref[...].astype(f32))


def _tile_lanes(x, n):
    return x if n == 1 else jnp.concatenate([x] * n, axis=1)


def _softmax_step(s, v2, m_ref, acc_ref):
    m_prev = m_ref[...]
    m_new = jnp.maximum(m_prev, jnp.max(s, axis=-1, keepdims=True))
    p = jnp.exp2(s - _tile_lanes(m_new, s.shape[1] // LANES))
    alpha = jnp.exp2(m_prev - m_new)
    acc_ref[...] = _tile_lanes(alpha, 2) * acc_ref[...] + _dot(p.astype(jnp.bfloat16), v2)
    m_ref[...] = m_new


def _with_ones(v):
    return jnp.concatenate([v, jnp.ones_like(v)], axis=1)


MLA_HEADS_PER_STEP = 6


def _mla_kernel(q_ref, k_ref, v_ref, zg_ref, o_ref, m_ref, acc_ref, *, tk):
    f32 = jnp.float32
    qi = pl.program_id(2)
    m_ref[...] = jnp.full(m_ref.shape, NEG_BIG, f32)
    acc_ref[...] = jnp.zeros(acc_ref.shape, f32)

    def step(ki, row0, masked):
        keys = pl.ds(pl.multiple_of(ki * tk, tk), tk)
        rows = pl.ds(row0, q_ref.shape[1] - row0)
        for h in range(MLA_HEADS_PER_STEP):
            s = _dot_nt(q_ref[h, rows, :], k_ref[h, keys, :])
            if masked:
                r = lax.broadcasted_iota(jnp.int32, s.shape, 0)
                c = lax.broadcasted_iota(jnp.int32, s.shape, 1)
                s = jnp.where(c <= r, s, 2.0 * NEG_BIG)
            _softmax_step(s, _with_ones(v_ref[h, keys, :]), m_ref.at[h, rows, :], acc_ref.at[h, rows, :])

    def body(ki, c):
        step(ki, 0, False)
        return c

    lax.fori_loop(0, 2 * qi, body, 0)
    step(2 * qi, 0, True)
    step(2 * qi + 1, tk, True)
    for h in range(MLA_HEADS_PER_STEP):
        o = acc_ref[h, :, 0:128] / acc_ref[h, :, 128:256]
        zg = zg_ref[:, 128 * h:128 * h + 128].astype(f32)
        o_ref[:, 128 * h:128 * h + 128] = (o * _silu(zg)).astype(o_ref.dtype)


def _mla(qm, km, vm, z, batch, seq, tk):
    t = z.shape[0]
    tq = 2 * tk
    nq = seq // tq
    hs = MLA_HEADS_PER_STEP
    resident = dict(pipeline_mode=pl.Buffered(1))
    return pl.pallas_call(
        functools.partial(_mla_kernel, tk=tk),
        out_shape=jax.ShapeDtypeStruct((t, MLA_WIDTH), jnp.bfloat16),
        grid=(batch, MLA_HEADS // hs, nq),
        in_specs=[
            pl.BlockSpec((hs, tq, 256), lambda b, h, i: (h, b * nq + i, 0)),
            pl.BlockSpec((hs, seq, 256), lambda b, h, i: (h, b, 0), **resident),
            pl.BlockSpec((hs, seq, 128), lambda b, h, i: (h, b, 0)),
            pl.BlockSpec((tq, 128 * hs), lambda b, h, i: (b * nq + i, Z_MLAZ // (128 * hs) + h)),
        ],
        out_specs=pl.BlockSpec((tq, 128 * hs), lambda b, h, i: (b * nq + i, h)),
        scratch_shapes=[pltpu.VMEM((hs, tq, LANES), jnp.float32), pltpu.VMEM((hs, tq, 256), jnp.float32)],
        compiler_params=_cparams(("parallel", "parallel", "arbitrary")),
        name="mla",
    )(qm, km, vm, z)


KC = 256
ATT_KC = 512
COUNT_FANIN = 8
SCORE_ROWS = 256


def _dsa_kernel(iq_ref, iw_ref, ik_ref, q_ref, k_ref, v0_ref, v1_ref, zg_ref, expand_ref, o_ref,
                key_ref, hi_ref, lo_ref, wb_ref, stage_ref, m_ref, acc_ref, *, k_top):
    f32, i32, bf16 = jnp.float32, jnp.int32, jnp.bfloat16
    tq = iq_ref.shape[0]
    i = pl.program_id(1)
    n_chunks = ((i + 1) * tq + KC - 1) // KC
    q_pos = i * tq + lax.broadcasted_iota(i32, (tq, KC), 0)
    lane_kc = lax.broadcasted_iota(i32, (tq, KC), 1)

    w = (iw_ref[...].astype(f32) * (IDX_HEADS ** -0.5)).astype(bf16)
    wb_ref[...] = _dot(w, expand_ref[...])

    def emit_keys(c):
        start = pl.multiple_of(c * KC, KC)
        bits = lax.bitcast_convert_type(stage_ref[...], i32)
        key = jnp.where(bits < 0, bits ^ jnp.int32(0x7FFFFFFF), bits)
        key = jnp.where(start + lane_kc <= q_pos, key, jnp.int32(INT_MIN))
        key_ref[:, pl.ds(start, KC)] = key
        key_t = key.T
        hi_ref[pl.ds(start, KC), :] = (key_t >> 16).astype(jnp.int16)
        lo_ref[pl.ds(start, KC), :] = ((key_t & 0xFFFF) - 32768).astype(jnp.int16)

    def score_chunk(c, first_row=0):
        emit_keys(jnp.maximum(c - 1, 0))
        ikc = ik_ref[pl.ds(pl.multiple_of(c * KC, KC), KC), :]
        for r0 in range(first_row, tq, SCORE_ROWS):
            rows = pl.ds(r0, SCORE_ROWS)
            acc = jnp.zeros((SCORE_ROWS, KC), f32)
            for h in range(IDX_HEADS):
                lg = _dot_nt(iq_ref[rows, 128 * h:128 * h + 128], ikc)
                acc = acc + _tile_lanes(wb_ref[rows, 128 * h:128 * h + 128], KC // LANES) * jnp.maximum(lg, 0.0)
            stage_ref[rows, :] = acc

    stage_ref[...] = jnp.zeros((tq, KC), f32)
    n_full = (i * tq) // KC + 1
    lax.fori_loop(0, n_full, lambda c, carry: (score_chunk(c), carry)[1], 0)
    for extra in range(1, tq // KC):
        score_chunk(n_full + extra - 1, first_row=(extra * KC // SCORE_ROWS) * SCORE_ROWS)
    emit_keys(n_chunks - 1)

    n_att = (n_chunks * KC + ATT_KC - 1) // ATT_KC

    def fill(c, carry):
        start = pl.multiple_of(c * KC, KC)
        key_ref[:, pl.ds(start, KC)] = jnp.full((tq, KC), INT_MIN, i32)
        hi_ref[pl.ds(start, KC), :] = jnp.full((KC, tq), -32768, jnp.int16)
        lo_ref[pl.ds(start, KC), :] = jnp.full((KC, tq), -32768, jnp.int16)
        return carry

    lax.fori_loop(n_chunks, n_att * (ATT_KC // KC), fill, 0)

    i16 = jnp.int16
    low16 = jnp.full((), -32768, i16)
    one16, zero16 = jnp.ones((), i16), jnp.zeros((), i16)

    def count16(ref, cand):
        c16 = cand.astype(i16)
        def body(c, cnt):
            start = pl.multiple_of(c * ATT_KC, ATT_KC)
            for g in range(0, ATT_KC, 16 * COUNT_FANIN):
                hit = jnp.where(ref[pl.ds(start + g, 16 * COUNT_FANIN), :] >= c16, one16, zero16)
                part = [hit[16 * j:16 * (j + 1)] for j in range(COUNT_FANIN)]
                while len(part) > 1:
                    part = [a + b for a, b in zip(part[0::2], part[1::2])]
                cnt = cnt + part[0]
            return cnt
        cnt = lax.fori_loop(0, n_att, body, jnp.zeros((16, tq), i16))
        return jnp.sum(cnt.astype(i32), axis=0, keepdims=True)

    def search16(ref, want):
        def step(it, carry):
            r, n_r = carry
            cand = r + lax.shift_left(jnp.int32(1), 15 - it)
            n = count16(ref, cand)
            ok = n >= want
            return jnp.where(ok, cand, r), jnp.where(ok, n, n_r)
        return lax.fori_loop(0, 16, step, (jnp.full((1, tq), -32768, i32), jnp.zeros((1, tq), i32)))

    t_hi, n_hi = search16(hi_ref, k_top)
    n_above = count16(hi_ref, t_hi + 1)
    t_hi16 = t_hi.astype(i16)

    def restrict(c, carry):
        rows = pl.ds(pl.multiple_of(c * ATT_KC, ATT_KC), ATT_KC)
        lo_ref[rows, :] = jnp.where(hi_ref[rows, :] == t_hi16, lo_ref[rows, :], low16)
        return carry

    lax.fori_loop(0, n_att, restrict, 0)
    t_lo, n_lo = search16(lo_ref, k_top - n_above)
    thr = t_hi * 65536 + (t_lo + 32768)

    n_ge = jnp.where(t_lo > -32768, n_above + n_lo, jnp.where(t_hi > -32768, n_hi, n_above))
    has_excess = jnp.max(n_ge) > k_top

    @pl.when(has_excess)
    def _():
        def count_where(hits):
            def body(c, cnt):
                start = pl.multiple_of(c * KC, KC)
                hit = hits(key_ref[:, pl.ds(start, KC)].T, start)
                return cnt + jnp.sum(hit.reshape(KC // 8, 8, tq), axis=0)
            cnt = lax.fori_loop(0, n_chunks, body, jnp.zeros((8, tq), i32))
            return jnp.sum(cnt, axis=0, keepdims=True)

        need = k_top - count_where(lambda kt, _: jnp.where(kt > thr, 1, 0))
        sub = lax.broadcasted_iota(i32, (KC, tq), 0)

        def idx_step(it, bound):
            cand = bound + lax.shift_left(jnp.int32(1), 30 - it)
            n_below = count_where(lambda kt, start: jnp.where(kt == thr, jnp.where(start + sub < cand, 1, 0), 0))
            return jnp.where(n_below <= need, cand, bound)

        bound = lax.fori_loop(0, 31, idx_step, jnp.zeros((1, tq), i32))

        def demote(c, carry):
            start = pl.multiple_of(c * KC, KC)
            kt = key_ref[:, pl.ds(start, KC)].T
            drop = jnp.where(kt == thr, jnp.where(start + sub >= bound, 1, 0), 0)
            key_ref[:, pl.ds(start, KC)] = jnp.where(drop == 1, jnp.int32(INT_MIN), kt).T
            return carry

        lax.fori_loop(0, n_chunks, demote, 0)

    thr_sel = jnp.maximum(thr, jnp.int32(INT_MIN + 1))
    thr_kc = _tile_lanes(jnp.broadcast_to(thr_sel, (LANES, tq)).T, KC // LANES)

    m_ref[...] = jnp.full(m_ref.shape, NEG_BIG, f32)
    acc_ref[...] = jnp.zeros(acc_ref.shape, f32)
    v_refs = (v0_ref, v1_ref)

    def attend(start, n_keys, row0=0):
        rows = pl.ds(row0, tq - row0)
        keys = pl.ds(start, n_keys)
        bias = jnp.where(key_ref[rows, keys] >= _tile_lanes(thr_kc[row0:], n_keys // KC), 0.0, 2.0 * NEG_BIG)
        for hd in range(DSA_HEADS):
            g = hd // DSA_GROUP
            s = _dot_nt(q_ref[hd, rows, :], k_ref[g, keys, :]) + bias
            _softmax_step(s, _with_ones(v_refs[g][keys, :]), m_ref.at[hd, rows, :], acc_ref.at[hd, rows, :])

    def attn_chunk(c, carry):
        attend(pl.multiple_of(c * ATT_KC, ATT_KC), ATT_KC)
        return carry

    lax.fori_loop(0, i, attn_chunk, 0)
    diag = pl.multiple_of(i * ATT_KC, ATT_KC)
    attend(diag, KC)
    attend(diag + KC, KC, row0=KC)

    for hd in range(DSA_HEADS):
        o = acc_ref[hd, :, 0:128] / acc_ref[hd, :, 128:256]
        zg = zg_ref[:, 128 * hd:128 * hd + 128].astype(f32)
        o_ref[:, 128 * hd:128 * hd + 128] = (o * _silu(zg)).astype(o_ref.dtype)


def _dsa(iqp, ikd, qd, kd, z, batch, seq, tq, k_top):
    t = z.shape[0]
    nq = seq // tq
    f32 = jnp.float32
    assert tq == ATT_KC == 2 * KC and seq % tq == 0, "the attention phase's diagonal handling assumes these tile ratios"
    expand = np.zeros((LANES, IDX_HEADS * LANES), np.float32)
    for h in range(IDX_HEADS):
        expand[h, LANES * h:LANES * (h + 1)] = 1.0
    expand = jnp.asarray(expand, jnp.bfloat16)
    return pl.pallas_call(
        functools.partial(_dsa_kernel, k_top=k_top),
        out_shape=jax.ShapeDtypeStruct((t, DSA_WIDTH), jnp.bfloat16),
        grid=(batch, nq),
        in_specs=[
            pl.BlockSpec((tq, IDX_HEADS * 128), lambda b, i: (b * nq + i, 0)),
            pl.BlockSpec((tq, 128), lambda b, i: (b * nq + i, Z_IW // 128)),
            pl.BlockSpec((seq, 128), lambda b, i: (b, 0)),
            pl.BlockSpec((DSA_HEADS, tq, 128), lambda b, i: (0, b * nq + i, 0)),
            pl.BlockSpec((DSA_KV_HEADS, seq, 128), lambda b, i: (0, b, 0)),
            pl.BlockSpec((seq, 128), lambda b, i: (b, Z_DV // 128)),
            pl.BlockSpec((seq, 128), lambda b, i: (b, Z_DV // 128 + 1)),
            pl.BlockSpec((tq, DSA_WIDTH), lambda b, i: (b * nq + i, Z_DSAZ // DSA_WIDTH)),
            pl.BlockSpec(expand.shape, lambda b, i: (0, 0)),
        ],
        out_specs=pl.BlockSpec((tq, DSA_WIDTH), lambda b, i: (b * nq + i, 0)),
        scratch_shapes=[
            pltpu.VMEM((tq, seq), jnp.int32),
            pltpu.VMEM((seq, tq), jnp.int16),
            pltpu.VMEM((seq, tq), jnp.int16),
            pltpu.VMEM((tq, IDX_HEADS * LANES), f32),
            pltpu.VMEM((tq, KC), f32),
            pltpu.VMEM((DSA_HEADS, tq, LANES), f32),
            pltpu.VMEM((DSA_HEADS, tq, 2 * DSA_HEAD_DIM), f32),
        ],
        compiler_params=_cparams(("parallel", "arbitrary")),
        name="dsa",
    )(iqp, z, ikd, qd, kd, z, z, z, expand)


def _post_kernel(h_ref, ya_ref, yc_ref, cx_ref, cb_ref, cc_ref, cz_ref, hx_ref, hc_ref, cw_ref, p_ref, g_ref,
                 wa_ref, wb_ref, wc_ref, wg_ref, wp_ref, o_ref, *, tiles_per_seq):
    yb = _gated_conv(cx_ref, cb_ref, cc_ref, cz_ref, hx_ref, hc_ref, cw_ref, tiles_per_seq).astype(jnp.bfloat16)
    h1 = h_ref[...] + _dot(ya_ref[...], wa_ref[...])
    h1 = h1 + _dot(yb, wb_ref[...])
    h1 = h1 + _dot(yc_ref[...], wc_ref[...])
    r = lax.rsqrt(jnp.mean(h1 * h1, axis=-1, keepdims=True) + EPS)
    a = (h1 * r * g_ref[...]).astype(jnp.bfloat16)
    gate = _dot(a, wg_ref[...])
    gate = 1.0 / (1.0 + jnp.exp(-gate))
    o_ref[...] = h1 + gate * _dot(p_ref[0].astype(jnp.bfloat16), wp_ref[...])


def _post(h, ya, yc, z, conv_w, seq, p, layer, g, wa, wb, wc, wg, wp, tm):
    t, d = h.shape
    rows = lambda a: pl.BlockSpec((tm, a.shape[1]), lambda i: (i, 0))
    zcol = lambda c: pl.BlockSpec((tm, CONV_CH), lambda i, c=c: (i, c // CONV_CH))
    halo = lambda c: pl.BlockSpec((8, CONV_CH), lambda i, c=c: (jnp.maximum(i * (tm // 8) - 1, 0), c // CONV_CH))
    whole = lambda a: pl.BlockSpec(a.shape, lambda i: (0, 0), pipeline_mode=pl.Buffered(1))
    return pl.pallas_call(
        functools.partial(_post_kernel, tiles_per_seq=seq // tm),
        out_shape=jax.ShapeDtypeStruct((t, d), jnp.float32),
        grid=(t // tm,),
        in_specs=[rows(h), rows(ya), rows(yc),
                  zcol(Z_CX), zcol(Z_CB), zcol(Z_CC), zcol(Z_CONVZ), halo(Z_CX), halo(Z_CC), whole(conv_w),
                  pl.BlockSpec((1, tm, p.shape[2]), lambda i: (layer, i, 0)), whole(g),
                  whole(wa), whole(wb), whole(wc), whole(wg), whole(wp)],
        out_specs=rows(h),
        compiler_params=_cparams(("parallel",)),
        name="post",
    )(h, ya, yc, z, z, z, z, z, z, conv_w, p, g, wa, wb, wc, wg, wp)


_REGROUP = (
    (Z_CQ, _O_CQ, 512), (Z_CKV, _O_CKV, 512),
    (Z_CX, _O_CX, 512), (Z_CB, _O_CB, 512), (Z_CC, _O_CC, 512), (Z_CONVZ, _O_CONVZ, 512),
    (Z_MLAZ, _O_MLAZ, 768), (Z_DQ, _O_DQ, 768), (Z_DSAZ, _O_DSAZ, 768),
    (Z_DK, _O_DK, 256), (Z_DV, _O_DV, 256), (Z_IQ, _O_IQ, 1024),
)


def _regroup_kernel(w_ref, o_ref):
    bf16 = jnp.bfloat16
    cols = w_ref.shape[2]
    for dst, src, n in _REGROUP + ((Z_KPE_IK, _O_KPE, MLA_ROPE), (Z_KPE_IK + MLA_ROPE, _O_IK, IDX_DIM),
                                   (Z_IW, _O_IW, IDX_HEADS)):
        o_ref[0, dst:dst + n, :] = w_ref[0, src:src + n, :].astype(bf16)
    o_ref[0, Z_IW + IDX_HEADS:Z_IQ, :] = jnp.zeros((Z_IQ - Z_IW - IDX_HEADS, cols), bf16)


def _regroup_w_in(w, tc=TC_REGROUP):
    depth, d, n = w.shape
    assert (d, n) == (D_MODEL, _N_IN), "the regroup table is written for the (D_MODEL, 7056) combined projection"
    return pl.pallas_call(
        _regroup_kernel,
        out_shape=jax.ShapeDtypeStruct((depth, Z_WIDTH, d), jnp.bfloat16),
        grid=(depth, d // tc),
        in_specs=[pl.BlockSpec((1, n, tc), lambda l, i: (l, 0, i))],
        out_specs=pl.BlockSpec((1, Z_WIDTH, tc), lambda l, i: (l, 0, i)),
        compiler_params=_cparams(("parallel", "parallel")),
        name="regroup",
    )(jnp.swapaxes(w, 1, 2))


def _pair_layout(a):
    lead = a.shape[:-1]
    a = a.reshape(lead + (MLA_HEADS, MLA_QK))
    nope, x1, x2 = a[..., :MLA_NOPE], a[..., MLA_NOPE:MLA_NOPE + 32], a[..., MLA_NOPE + 32:]
    return jnp.concatenate([nope, x1, x2, x2, x1], axis=-1).reshape(lead + (MLA_HEADS * 256,))


_B_C32, _B_S32, _B_C16, _B_S16, _B_C8, _B_S8, _B_ONE, _B_ROWS = 0, 32, 64, 80, 96, 104, 112, 128


def _table_selector():
    sel = np.zeros((_B_ROWS, N_TABLES * LANES), np.float32)

    def put(table, lane, base, n, coef=1.0):
        for j in range(n):
            sel[base + (j if base != _B_ONE else 0), table * LANES + lane + j] = coef

    c32, s32, c16, s16, c8, s8, one = _B_C32, _B_S32, _B_C16, _B_S16, _B_C8, _B_S8, _B_ONE
    put(TB_CSQ, 0, c32, 32); put(TB_CSQ, 32, c32, 32); put(TB_CSQ, 64, s32, 32, -1.0); put(TB_CSQ, 96, s32, 32)
    put(TB_C46, 0, c32, 32); put(TB_C46, 32, c32, 32); put(TB_C46, 64, c8, 8); put(TB_C46, 72, c8, 8)
    put(TB_C46, 80, one, 48)
    put(TB_S1A, 0, s32, 32, -1.0)
    put(TB_S2A, 32, s32, 32)
    put(TB_S1B, 64, s8, 8, -1.0)
    put(TB_S2B, 72, s8, 8)
    put(TB_CD, 0, c16, 16); put(TB_CD, 16, c16, 16); put(TB_CD, 32, one, 96)
    put(TB_S1D, 0, s16, 16, -1.0)
    put(TB_S2D, 16, s16, 16)
    qs = IDX_DIM ** -0.5
    for off in (0, 64):
        put(TB_CI, off, c8, 8, qs); put(TB_CI, off + 8, c8, 8, qs); put(TB_CI, off + 16, one, 48, qs)
        put(TB_S1I, off, s8, 8, -qs)
        put(TB_S2I, off + 8, s8, 8, qs)
    return sel


def _rope_basis(positions):
    f32 = jnp.float32
    pos = positions.reshape(-1, 1).astype(f32)
    freqs = [ROPE_THETA ** (-jnp.arange(half, dtype=f32) / half) for half in (MLA_ROPE // 2, DSA_ROT // 2, IDX_ROT // 2)]
    inv = jnp.concatenate([freqs[0], freqs[0], freqs[1], freqs[1], freqs[2], freqs[2], jnp.zeros((_B_ROWS - _B_ONE,), f32)])
    is_cos = np.zeros((_B_ROWS,), bool)
    for start, stop in ((_B_C32, _B_S32), (_B_C16, _B_S16), (_B_C8, _B_S8), (_B_ONE, _B_ONE + 1)):
        is_cos[start:stop] = True
    ang = pos * inv[None, :]
    basis = jnp.where(jnp.asarray(is_cos)[None, :], jnp.cos(ang), jnp.sin(ang))
    return basis, jnp.asarray(_table_selector(), jnp.bfloat16)


def _layer(h, layer, p_all, w_in_all, tabs, batch, seq, k_top, norm_in, mla_gq, mla_w_uq, mla_gkv, mla_w_ukv, mla_qn,
           mla_kn, conv_w, dsa_qn, dsa_kn, w_out, ple_norm, ple_w_gate, ple_w_proj):
    f32, bf16 = jnp.float32, jnp.bfloat16
    t = h.shape[0]

    z = _inproj(h, norm_in[None, :], w_in_all, layer, min(TM_INPROJ, t), TN_INPROJ)

    gq = _pair_layout(jnp.tile(mla_qn, MLA_HEADS))[None, :] * (MLA_QK ** -0.5 * LOG2E)
    g46 = jnp.concatenate([mla_kn[MLA_NOPE:], jnp.ones((64,), f32)])[None, :]
    qm, km, vm, qd, kd, iqp, ikd = _prep(
        z, *tabs, mla_gq[None, :], mla_gkv[None, :], _pair_layout(mla_w_uq).astype(bf16), mla_w_ukv.astype(bf16),
        gq, mla_kn[None, :MLA_NOPE], g46, dsa_qn[None, :] * (DSA_HEAD_DIM ** -0.5 * LOG2E), dsa_kn[None, :],
        min(TM_PREP, t))

    y_a = _mla(qm, km, vm, z, batch, seq, TK_MLA)
    y_c = _dsa(iqp, ikd, qd, kd, z, batch, seq, TQ_DSA, k_top)

    wo = w_out.astype(bf16)
    return _post(h, y_a, y_c, z, jnp.pad(conv_w, ((0, 8 - CONV_K), (0, 0))), seq, p_all, layer, ple_norm[None, :],
                 wo[:MLA_WIDTH], wo[MLA_WIDTH:MLA_WIDTH + CONV_CH], wo[MLA_WIDTH + CONV_CH:],
                 ple_w_gate.astype(bf16), ple_w_proj.astype(bf16), min(TM_POST, t))


def kernel(x, p, positions, norm_in, w_in, mla_gq, mla_w_uq, mla_gkv, mla_w_ukv, mla_qn, mla_kn, conv_w, dsa_qn,
           dsa_kn, w_out, ple_norm, ple_w_gate, ple_w_proj):
    batch, seq, d = x.shape
    depth = p.shape[0]
    t = batch * seq
    k_top = min(TOPK_MAX, seq // 4)
    tabs = _rope_basis(positions)
    w_in_z = _regroup_w_in(w_in)
    h = x.reshape(t, d)
    p_all = p.reshape(depth, t, PLE_DIM)
    for i in range(depth):
        h = _layer(h, i, p_all, w_in_z, tabs, batch, seq, k_top, norm_in[i], mla_gq[i],
                   mla_w_uq[i], mla_gkv[i], mla_w_ukv[i], mla_qn[i], mla_kn[i], conv_w[i], dsa_qn[i], dsa_kn[i],
                   w_out[i], ple_norm[i], ple_w_gate[i], ple_w_proj[i])
    return h.reshape(batch, seq, d)
```

```python
import functools

import numpy as np
import jax
import jax.numpy as jnp
from jax import lax
from jax.experimental import pallas as pl
from jax.experimental.pallas import tpu as pltpu

D_MODEL = 2048
PLE_DIM = 256
ROPE_THETA = 500000.0
EPS = 1e-6

MLA_HEADS = 6
MLA_Q_LORA = 512
MLA_KV_LORA = 512
MLA_NOPE = 128
MLA_ROPE = 64
MLA_V = 128
MLA_QK = MLA_NOPE + MLA_ROPE
MLA_WIDTH = MLA_HEADS * MLA_V

CONV_CH = 512
CONV_K = 3

DSA_HEADS = 6
DSA_KV_HEADS = 2
DSA_GROUP = DSA_HEADS // DSA_KV_HEADS
DSA_HEAD_DIM = 128
DSA_WIDTH = DSA_HEADS * DSA_HEAD_DIM
DSA_ROT = DSA_HEAD_DIM // 4
IDX_HEADS = 16
IDX_DIM = 64
IDX_ROT = IDX_DIM // 4
TOPK_MAX = 256

LANES = 128

_O_CQ, _O_CKV, _O_KPE, _O_MLAZ = 0, 512, 1024, 1088
_O_CX, _O_CB, _O_CC, _O_CONVZ = 1856, 2368, 2880, 3392
_O_DQ, _O_DK, _O_DV, _O_DSAZ = 3904, 4672, 4928, 5184
_O_IQ, _O_IW, _O_IK, _N_IN = 5952, 6976, 6992, 7056

Z_CQ, Z_CKV = 0, 512
Z_CX, Z_CB, Z_CC, Z_CONVZ = 1024, 1536, 2048, 2560
Z_MLAZ, Z_DQ, Z_DSAZ = 3072, 3840, 4608
Z_DK, Z_DV = 5376, 5632
Z_KPE_IK, Z_IW, Z_IQ = 5888, 6016, 6144
Z_WIDTH = 7168

LOG2E = 1.4426950408889634
INT_MIN = -(2 ** 31)
NEG_BIG = -1e30

VMEM_LIMIT = 56 * 1024 * 1024

TM_INPROJ, TN_INPROJ = 1024, Z_WIDTH // 4
TC_REGROUP = 256
TM_PREP = 512
TK_MLA = 512
TQ_DSA = 512
TM_POST = 512


def _cparams(sem):
    return pltpu.CompilerParams(dimension_semantics=sem, vmem_limit_bytes=VMEM_LIMIT)


def _dot(a, b):
    return jnp.dot(a, b, preferred_element_type=jnp.float32)


def _dot_nt(a, b):
    return lax.dot_general(a, b, (((1,), (1,)), ((), ())), preferred_element_type=jnp.float32)


def _roll(x, shift):
    return pltpu.roll(x, shift, 1)


def _inproj_kernel(x_ref, g_ref, w_ref, o_ref, a_ref):
    @pl.when(pl.program_id(1) == 0)
    def _():
        x = x_ref[...]
        r = lax.rsqrt(jnp.mean(x * x, axis=-1, keepdims=True) + EPS)
        a_ref[...] = (x * r * g_ref[...]).astype(jnp.bfloat16)

    o_ref[...] = _dot_nt(a_ref[...], w_ref[0]).astype(o_ref.dtype)


def _inproj(h, g, w, layer, tm, tn):
    t, d = h.shape
    n = w.shape[1]
    return pl.pallas_call(
        _inproj_kernel,
        out_shape=jax.ShapeDtypeStruct((t, n), jnp.bfloat16),
        grid=(t // tm, n // tn),
        in_specs=[
            pl.BlockSpec((tm, d), lambda i, j: (i, 0)),
            pl.BlockSpec((1, d), lambda i, j: (0, 0)),
            pl.BlockSpec((1, tn, d), lambda i, j: (layer, j, 0)),
        ],
        out_specs=pl.BlockSpec((tm, tn), lambda i, j: (i, j)),
        scratch_shapes=[pltpu.VMEM((tm, d), jnp.bfloat16)],
        compiler_params=_cparams(("parallel", "arbitrary")),
        name="inproj",
    )(h, g, w)


TB_CSQ = 0
TB_C46, TB_S1A, TB_S2A, TB_S1B, TB_S2B = 1, 2, 3, 4, 5
TB_CD, TB_S1D, TB_S2D = 6, 7, 8
TB_CI, TB_S1I, TB_S2I = 9, 10, 11
N_TABLES = 12


def _prep_kernel(cq_ref, ckv_ref, t46_ref, dq_ref, dk_ref, iq_ref, basis_ref, sel_ref,
                 glq_ref, glkv_ref, wuq_ref, wukv_ref, gq_ref, gkn_ref, g46_ref, gdq_ref, gdk_ref,
                 qm_ref, km_ref, vm_ref, qd_ref, kd_ref, iqp_ref, ikd_ref):
    f32, bf16 = jnp.float32, jnp.bfloat16
    tm = cq_ref.shape[0]
    lane = lax.broadcasted_iota(jnp.int32, (tm, LANES), 1)
    low = lane < 64

    b0 = basis_ref[...]
    b_hi = b0.astype(bf16)
    b1 = b0 - b_hi.astype(f32)
    b_mid = b1.astype(bf16)
    b_lo = (b1 - b_mid.astype(f32)).astype(bf16)
    tabs = _dot(b_hi, sel_ref[...]) + _dot(b_mid, sel_ref[...]) + _dot(b_lo, sel_ref[...])
    tab = lambda k: tabs[:, LANES * k:LANES * (k + 1)]

    def rms(x, n):
        return lax.rsqrt(jnp.sum(x * x, axis=-1, keepdims=True) * (1.0 / n) + EPS)

    cq = cq_ref[...].astype(f32)
    aq = (cq * rms(cq, MLA_Q_LORA) * glq_ref[...]).astype(bf16)
    qx = _dot(aq, wuq_ref[...])
    csq = tab(TB_CSQ)
    for h in range(MLA_HEADS):
        nope = qx[:, 256 * h:256 * h + 128]
        pair = qx[:, 256 * h + 128:256 * h + 256]
        ss = jnp.sum(nope * nope, axis=-1, keepdims=True) + 0.5 * jnp.sum(pair * pair, axis=-1, keepdims=True)
        r = lax.rsqrt(ss * (1.0 / MLA_QK) + EPS)
        qm_ref[h, :, 0:128] = (nope * r * gq_ref[:, 256 * h:256 * h + 128]).astype(bf16)
        qm_ref[h, :, 128:256] = (pair * r * gq_ref[:, 256 * h + 128:256 * h + 256] * csq).astype(bf16)

    t46 = t46_ref[...].astype(f32)
    ss_pe = jnp.sum(jnp.where(low, t46 * t46, 0.0), axis=-1, keepdims=True)
    y46 = t46 * g46_ref[...]
    r46 = (y46 * tab(TB_C46)
           + _roll(y46, 96) * tab(TB_S1A) + _roll(y46, 32) * tab(TB_S2A)
           + _roll(y46, 120) * tab(TB_S1B) + _roll(y46, 8) * tab(TB_S2B))
    sw = _roll(r46, 64)
    kpe2 = jnp.where(low, r46, sw)
    ikd_ref[...] = jnp.where(low, sw, r46).astype(bf16)

    ckv = ckv_ref[...].astype(f32)
    akv = (ckv * rms(ckv, MLA_KV_LORA) * glkv_ref[...]).astype(bf16)
    kvx = _dot(akv, wukv_ref[...])
    for h in range(MLA_HEADS):
        kn = kvx[:, 256 * h:256 * h + 128]
        ss = jnp.sum(kn * kn, axis=-1, keepdims=True) + ss_pe
        r = lax.rsqrt(ss * (1.0 / MLA_QK) + EPS)
        km_ref[h, :, 0:128] = (kn * r * gkn_ref[...]).astype(bf16)
        km_ref[h, :, 128:256] = (kpe2 * r).astype(bf16)
        vm_ref[h] = kvx[:, 256 * h + 128:256 * h + 256].astype(bf16)

    cd, s1d, s2d = tab(TB_CD), tab(TB_S1D), tab(TB_S2D)

    def dsa_head(x, g):
        y = x * rms(x, DSA_HEAD_DIM) * g
        return y * cd + _roll(y, 112) * s1d + _roll(y, 16) * s2d

    for h in range(DSA_HEADS):
        x = dq_ref[:, 128 * h:128 * h + 128].astype(f32)
        qd_ref[h] = dsa_head(x, gdq_ref[...]).astype(bf16)
    for g in range(DSA_KV_HEADS):
        x = dk_ref[:, 128 * g:128 * g + 128].astype(f32)
        kd_ref[g] = dsa_head(x, gdk_ref[...]).astype(bf16)

    ci, s1i, s2i = tab(TB_CI), tab(TB_S1I), tab(TB_S2I)
    for j in range(IDX_HEADS // 2):
        x = iq_ref[:, 128 * j:128 * j + 128].astype(f32)
        y = x * ci + _roll(x, 120) * s1i + _roll(x, 8) * s2i
        iqp_ref[:, 256 * j:256 * j + 128] = jnp.where(low, y, 0.0).astype(bf16)
        iqp_ref[:, 256 * j + 128:256 * j + 256] = jnp.where(low, 0.0, y).astype(bf16)


def _prep(z, basis, sel, glq, glkv, wuq, wukv, gq, gkn, g46, gdq, gdk, tm):
    t = z.shape[0]
    bf16 = jnp.bfloat16
    row = lambda w, c: pl.BlockSpec((tm, w), lambda i, c=c: (i, c))
    full = lambda a: pl.BlockSpec(a.shape, lambda i: (0,) * a.ndim)
    heads = lambda n, w: pl.BlockSpec((n, tm, w), lambda i: (0, i, 0))
    return pl.pallas_call(
        _prep_kernel,
        out_shape=(
            jax.ShapeDtypeStruct((MLA_HEADS, t, 256), bf16),
            jax.ShapeDtypeStruct((MLA_HEADS, t, 256), bf16),
            jax.ShapeDtypeStruct((MLA_HEADS, t, 128), bf16),
            jax.ShapeDtypeStruct((DSA_HEADS, t, 128), bf16),
            jax.ShapeDtypeStruct((DSA_KV_HEADS, t, 128), bf16),
            jax.ShapeDtypeStruct((t, IDX_HEADS * 128), bf16),
            jax.ShapeDtypeStruct((t, 128), bf16),
        ),
        grid=(t // tm,),
        in_specs=[
            row(512, Z_CQ // 512), row(512, Z_CKV // 512), row(128, Z_KPE_IK // 128),
            row(768, Z_DQ // 768), row(256, Z_DK // 256), row(1024, Z_IQ // 1024),
            pl.BlockSpec((tm, LANES), lambda i: (i, 0)), full(sel),
            full(glq), full(glkv), full(wuq), full(wukv), full(gq), full(gkn), full(g46), full(gdq), full(gdk),
        ],
        out_specs=(
            heads(MLA_HEADS, 256), heads(MLA_HEADS, 256), heads(MLA_HEADS, 128),
            heads(DSA_HEADS, 128), heads(DSA_KV_HEADS, 128),
            pl.BlockSpec((tm, IDX_HEADS * 128), lambda i: (i, 0)),
            pl.BlockSpec((tm, 128), lambda i: (i, 0)),
        ),
        compiler_params=_cparams(("parallel",)),
        name="prep",
    )(z, z, z, z, z, z, basis, sel, glq, glkv, wuq, wukv, gq, gkn, g46, gdq, gdk)


def _silu(x):
    return x / (1.0 + jnp.exp(-x))


def _gated_conv(cx_ref, cb_ref, cc_ref, cz_ref, hx_ref, hc_ref, w_ref, tiles_per_seq):
    f32 = jnp.float32
    tm, ch = cx_ref.shape
    u = cc_ref[...].astype(f32) * cx_ref[...].astype(f32)
    keep = (pl.program_id(0) % tiles_per_seq != 0).astype(f32)
    hu = hc_ref[...].astype(f32) * hx_ref[...].astype(f32) * keep
    row = lax.broadcasted_iota(jnp.int32, (tm, ch), 0)
    u1 = jnp.where(row == 0, hu[7:8, :], pltpu.roll(u, 1, 0))
    u2 = jnp.where(row == 0, hu[6:7, :], jnp.where(row == 1, hu[7:8, :], pltpu.roll(u, 2, 0)))
    conv = w_ref[0:1, :] * u2 + w_ref[1:2, :] * u1 + w_ref[2:3, :] * u
    return cb_ref[...].astype(f32) * conv * _silu(cz_ref[...].astype(f32))


def _tile_lanes(x, n):
    return x if n == 1 else jnp.concatenate([x] * n, axis=1)


def _softmax_step(s, v2, m_ref, acc_ref):
    m_prev = m_ref[...]
    m_new = jnp.maximum(m_prev, jnp.max(s, axis=-1, keepdims=True))
    p = jnp.exp2(s - _tile_lanes(m_new, s.shape[1] // LANES))
    alpha = jnp.exp2(m_prev - m_new)
    acc_ref[...] = _tile_lanes(alpha, 2) * acc_ref[...] + _dot(p.astype(jnp.bfloat16), v2)
    m_ref[...] = m_new


def _with_ones(v):
    return jnp.concatenate([v, jnp.ones_like(v)], axis=1)


MLA_HEADS_PER_STEP = 6


def _mla_kernel(q_ref, k_ref, v_ref, zg_ref, o_ref, m_ref, acc_ref, *, tk):
    f32 = jnp.float32
    qi = pl.program_id(2)
    m_ref[...] = jnp.full(m_ref.shape, NEG_BIG, f32)
    acc_ref[...] = jnp.zeros(acc_ref.shape, f32)

    def step(ki, row0, masked):
        keys = pl.ds(pl.multiple_of(ki * tk, tk), tk)
        rows = pl.ds(row0, q_ref.shape[1] - row0)
        for h in range(MLA_HEADS_PER_STEP):
            s = _dot_nt(q_ref[h, rows, :], k_ref[h, keys, :])
            if masked:
                r = lax.broadcasted_iota(jnp.int32, s.shape, 0)
                c = lax.broadcasted_iota(jnp.int32, s.shape, 1)
                s = jnp.where(c <= r, s, 2.0 * NEG_BIG)
            _softmax_step(s, _with_ones(v_ref[h, keys, :]), m_ref.at[h, rows, :], acc_ref.at[h, rows, :])

    def body(ki, c):
        step(ki, 0, False)
        return c

    lax.fori_loop(0, 2 * qi, body, 0)
    step(2 * qi, 0, True)
    step(2 * qi + 1, tk, True)
    for h in range(MLA_HEADS_PER_STEP):
        o = acc_ref[h, :, 0:128] / acc_ref[h, :, 128:256]
        zg = zg_ref[:, 128 * h:128 * h + 128].astype(f32)
        o_ref[:, 128 * h:128 * h + 128] = (o * _silu(zg)).astype(o_ref.dtype)


def _mla(qm, km, vm, z, batch, seq, tk):
    t = z.shape[0]
    tq = 2 * tk
    nq = seq // tq
    hs = MLA_HEADS_PER_STEP
    resident = dict(pipeline_mode=pl.Buffered(1))
    return pl.pallas_call(
        functools.partial(_mla_kernel, tk=tk),
        out_shape=jax.ShapeDtypeStruct((t, MLA_WIDTH), jnp.bfloat16),
        grid=(batch, MLA_HEADS // hs, nq),
        in_specs=[
            pl.BlockSpec((hs, tq, 256), lambda b, h, i: (h, b * nq + i, 0)),
            pl.BlockSpec((hs, seq, 256), lambda b, h, i: (h, b, 0), **resident),
            pl.BlockSpec((hs, seq, 128), lambda b, h, i: (h, b, 0)),
            pl.BlockSpec((tq, 128 * hs), lambda b, h, i: (b * nq + i, Z_MLAZ // (128 * hs) + h)),
        ],
        out_specs=pl.BlockSpec((tq, 128 * hs), lambda b, h, i: (b * nq + i, h)),
        scratch_shapes=[pltpu.VMEM((hs, tq, LANES), jnp.float32), pltpu.VMEM((hs, tq, 256), jnp.float32)],
        compiler_params=_cparams(("parallel", "parallel", "arbitrary")),
        name="mla",
    )(qm, km, vm, z)


KC = 256
ATT_KC = 512
COUNT_FANIN = 8
SCORE_ROWS = 256


def _dsa_kernel(iq_ref, iw_ref, ik_ref, q_ref, k_ref, v0_ref, v1_ref, zg_ref, expand_ref, o_ref,
                key_ref, hi_ref, lo_ref, wb_ref, stage_ref, m_ref, acc_ref, *, k_top):
    f32, i32, bf16 = jnp.float32, jnp.int32, jnp.bfloat16
    tq = iq_ref.shape[0]
    i = pl.program_id(1)
    n_chunks = ((i + 1) * tq + KC - 1) // KC
    q_pos = i * tq + lax.broadcasted_iota(i32, (tq, KC), 0)
    lane_kc = lax.broadcasted_iota(i32, (tq, KC), 1)

    w = (iw_ref[...].astype(f32) * (IDX_HEADS ** -0.5)).astype(bf16)
    wb_ref[...] = _dot(w, expand_ref[...])

    def emit_keys(c):
        start = pl.multiple_of(c * KC, KC)
        bits = lax.bitcast_convert_type(stage_ref[...], i32)
        key = jnp.where(bits < 0, bits ^ jnp.int32(0x7FFFFFFF), bits)
        key = jnp.where(start + lane_kc <= q_pos, key, jnp.int32(INT_MIN))
        key_ref[:, pl.ds(start, KC)] = key
        key_t = key.T
        hi_ref[pl.ds(start, KC), :] = (key_t >> 16).astype(jnp.int16)
        lo_ref[pl.ds(start, KC), :] = ((key_t & 0xFFFF) - 32768).astype(jnp.int16)
        tiles = [key_t[8 * a:8 * a + 8, LANES * b:LANES * b + LANES] for a in range(KC // 8) for b in range(tq // LANES)]
        while len(tiles) > 1:
            tiles = [x | y for x, y in zip(tiles[0::2], tiles[1::2])]
        word = lax.bitcast_convert_type(tiles[0], jnp.uint32)
        zero = lax.shift_right_logical(lax.shift_right_logical(word, jnp.uint32(16)), jnp.uint32(16))
        return lax.bitcast_convert_type(zero, f32)

    def score_chunk(c, first_row=0):
        anchor = emit_keys(jnp.maximum(c - 1, 0))
        anchor = jnp.tile(anchor, (SCORE_ROWS // 8, KC // LANES))
        ikc = ik_ref[pl.ds(pl.multiple_of(c * KC, KC), KC), :]
        for r0 in range(first_row, tq, SCORE_ROWS):
            rows = pl.ds(r0, SCORE_ROWS)
            acc = anchor
            for h in range(IDX_HEADS):
                lg = _dot_nt(iq_ref[rows, 128 * h:128 * h + 128], ikc)
                acc = acc + _tile_lanes(wb_ref[rows, 128 * h:128 * h + 128], KC // LANES) * jnp.maximum(lg, 0.0)
            stage_ref[rows, :] = acc

    stage_ref[...] = jnp.zeros((tq, KC), f32)
    n_full = (i * tq) // KC + 1
    lax.fori_loop(0, n_full, lambda c, carry: (score_chunk(c), carry)[1], 0)
    for extra in range(1, tq // KC):
        score_chunk(n_full + extra - 1, first_row=(extra * KC // SCORE_ROWS) * SCORE_ROWS)
    emit_keys(n_chunks - 1)

    n_att = (n_chunks * KC + ATT_KC - 1) // ATT_KC

    def fill(c, carry):
        start = pl.multiple_of(c * KC, KC)
        key_ref[:, pl.ds(start, KC)] = jnp.full((tq, KC), INT_MIN, i32)
        hi_ref[pl.ds(start, KC), :] = jnp.full((KC, tq), -32768, jnp.int16)
        lo_ref[pl.ds(start, KC), :] = jnp.full((KC, tq), -32768, jnp.int16)
        return carry

    lax.fori_loop(n_chunks, n_att * (ATT_KC // KC), fill, 0)

    i16 = jnp.int16
    low16 = jnp.full((), -32768, i16)
    one16, zero16 = jnp.ones((), i16), jnp.zeros((), i16)

    def count16(ref, cand):
        c16 = cand.astype(i16)
        def body(c, cnt):
            start = pl.multiple_of(c * ATT_KC, ATT_KC)
            for g in range(0, ATT_KC, 16 * COUNT_FANIN):
                hit = jnp.where(ref[pl.ds(start + g, 16 * COUNT_FANIN), :] >= c16, one16, zero16)
                part = [hit[16 * j:16 * (j + 1)] for j in range(COUNT_FANIN)]
                while len(part) > 1:
                    part = [a + b for a, b in zip(part[0::2], part[1::2])]
                cnt = cnt + part[0]
            return cnt
        cnt = lax.fori_loop(0, n_att, body, jnp.zeros((16, tq), i16))
        return jnp.sum(cnt.astype(i32), axis=0, keepdims=True)

    def search16(ref, want):
        def step(it, carry):
            r, n_r = carry
            cand = r + lax.shift_left(jnp.int32(1), 15 - it)
            n = count16(ref, cand)
            ok = n >= want
            return jnp.where(ok, cand, r), jnp.where(ok, n, n_r)
        return lax.fori_loop(0, 16, step, (jnp.full((1, tq), -32768, i32), jnp.zeros((1, tq), i32)))

    t_hi, n_hi = search16(hi_ref, k_top)
    n_above = count16(hi_ref, t_hi + 1)
    t_hi16 = t_hi.astype(i16)

    def restrict(c, carry):
        rows = pl.ds(pl.multiple_of(c * ATT_KC, ATT_KC), ATT_KC)
        lo_ref[rows, :] = jnp.where(hi_ref[rows, :] == t_hi16, lo_ref[rows, :], low16)
        return carry

    lax.fori_loop(0, n_att, restrict, 0)
    t_lo, n_lo = search16(lo_ref, k_top - n_above)
    thr = t_hi * 65536 + (t_lo + 32768)

    n_ge = jnp.where(t_lo > -32768, n_above + n_lo, jnp.where(t_hi > -32768, n_hi, n_above))
    has_excess = jnp.max(n_ge) > k_top

    @pl.when(has_excess)
    def _():
        def count_where(hits):
            def body(c, cnt):
                start = pl.multiple_of(c * KC, KC)
                hit = hits(key_ref[:, pl.ds(start, KC)].T, start)
                return cnt + jnp.sum(hit.reshape(KC // 8, 8, tq), axis=0)
            cnt = lax.fori_loop(0, n_chunks, body, jnp.zeros((8, tq), i32))
            return jnp.sum(cnt, axis=0, keepdims=True)

        need = k_top - count_where(lambda kt, _: jnp.where(kt > thr, 1, 0))
        sub = lax.broadcasted_iota(i32, (KC, tq), 0)

        def idx_step(it, bound):
            cand = bound + lax.shift_left(jnp.int32(1), 30 - it)
            n_below = count_where(lambda kt, start: jnp.where(kt == thr, jnp.where(start + sub < cand, 1, 0), 0))
            return jnp.where(n_below <= need, cand, bound)

        bound = lax.fori_loop(0, 31, idx_step, jnp.zeros((1, tq), i32))

        def demote(c, carry):
            start = pl.multiple_of(c * KC, KC)
            kt = key_ref[:, pl.ds(start, KC)].T
            drop = jnp.where(kt == thr, jnp.where(start + sub >= bound, 1, 0), 0)
            key_ref[:, pl.ds(start, KC)] = jnp.where(drop == 1, jnp.int32(INT_MIN), kt).T
            return carry

        lax.fori_loop(0, n_chunks, demote, 0)

    thr_sel = jnp.maximum(thr, jnp.int32(INT_MIN + 1))
    thr_kc = _tile_lanes(jnp.broadcast_to(thr_sel, (LANES, tq)).T, KC // LANES)

    m_ref[...] = jnp.full(m_ref.shape, NEG_BIG, f32)
    acc_ref[...] = jnp.zeros(acc_ref.shape, f32)
    v_refs = (v0_ref, v1_ref)

    def attend(start, n_keys, row0=0):
        rows = pl.ds(row0, tq - row0)
        keys = pl.ds(start, n_keys)
        bias = jnp.where(key_ref[rows, keys] >= _tile_lanes(thr_kc[row0:], n_keys // KC), 0.0, 2.0 * NEG_BIG)
        for hd in range(DSA_HEADS):
            g = hd // DSA_GROUP
            s = _dot_nt(q_ref[hd, rows, :], k_ref[g, keys, :]) + bias
            _softmax_step(s, _with_ones(v_refs[g][keys, :]), m_ref.at[hd, rows, :], acc_ref.at[hd, rows, :])

    def attn_chunk(c, carry):
        attend(pl.multiple_of(c * ATT_KC, ATT_KC), ATT_KC)
        return carry

    lax.fori_loop(0, i, attn_chunk, 0)
    diag = pl.multiple_of(i * ATT_KC, ATT_KC)
    attend(diag, KC)
    attend(diag + KC, KC, row0=KC)

    for hd in range(DSA_HEADS):
        o = acc_ref[hd, :, 0:128] / acc_ref[hd, :, 128:256]
        zg = zg_ref[:, 128 * hd:128 * hd + 128].astype(f32)
        o_ref[:, 128 * hd:128 * hd + 128] = (o * _silu(zg)).astype(o_ref.dtype)


def _dsa(iqp, ikd, qd, kd, z, batch, seq, tq, k_top):
    t = z.shape[0]
    nq = seq // tq
    f32 = jnp.float32
    assert tq == ATT_KC == 2 * KC and seq % tq == 0, "the attention phase's diagonal handling assumes these tile ratios"
    expand = np.zeros((LANES, IDX_HEADS * LANES), np.float32)
    for h in range(IDX_HEADS):
        expand[h, LANES * h:LANES * (h + 1)] = 1.0
    expand = jnp.asarray(expand, jnp.bfloat16)
    return pl.pallas_call(
        functools.partial(_dsa_kernel, k_top=k_top),
        out_shape=jax.ShapeDtypeStruct((t, DSA_WIDTH), jnp.bfloat16),
        grid=(batch, nq),
        in_specs=[
            pl.BlockSpec((tq, IDX_HEADS * 128), lambda b, i: (b * nq + i, 0)),
            pl.BlockSpec((tq, 128), lambda b, i: (b * nq + i, Z_IW // 128)),
            pl.BlockSpec((seq, 128), lambda b, i: (b, 0)),
            pl.BlockSpec((DSA_HEADS, tq, 128), lambda b, i: (0, b * nq + i, 0)),
            pl.BlockSpec((DSA_KV_HEADS, seq, 128), lambda b, i: (0, b, 0)),
            pl.BlockSpec((seq, 128), lambda b, i: (b, Z_DV // 128)),
            pl.BlockSpec((seq, 128), lambda b, i: (b, Z_DV // 128 + 1)),
            pl.BlockSpec((tq, DSA_WIDTH), lambda b, i: (b * nq + i, Z_DSAZ // DSA_WIDTH)),
            pl.BlockSpec(expand.shape, lambda b, i: (0, 0)),
        ],
        out_specs=pl.BlockSpec((tq, DSA_WIDTH), lambda b, i: (b * nq + i, 0)),
        scratch_shapes=[
            pltpu.VMEM((tq, seq), jnp.int32),
            pltpu.VMEM((seq, tq), jnp.int16),
            pltpu.VMEM((seq, tq), jnp.int16),
            pltpu.VMEM((tq, IDX_HEADS * LANES), f32),
            pltpu.VMEM((tq, KC), f32),
            pltpu.VMEM((DSA_HEADS, tq, LANES), f32),
            pltpu.VMEM((DSA_HEADS, tq, 2 * DSA_HEAD_DIM), f32),
        ],
        compiler_params=_cparams(("parallel", "arbitrary")),
        name="dsa",
    )(iqp, z, ikd, qd, kd, z, z, z, expand)


def _post_kernel(h_ref, ya_ref, yc_ref, cx_ref, cb_ref, cc_ref, cz_ref, hx_ref, hc_ref, cw_ref, p_ref, g_ref,
                 wa_ref, wb_ref, wc_ref, wg_ref, wp_ref, o_ref, *, tiles_per_seq):
    yb = _gated_conv(cx_ref, cb_ref, cc_ref, cz_ref, hx_ref, hc_ref, cw_ref, tiles_per_seq).astype(jnp.bfloat16)
    h1 = h_ref[...] + _dot(ya_ref[...], wa_ref[...])
    h1 = h1 + _dot(yb, wb_ref[...])
    h1 = h1 + _dot(yc_ref[...], wc_ref[...])
    r = lax.rsqrt(jnp.mean(h1 * h1, axis=-1, keepdims=True) + EPS)
    a = (h1 * r * g_ref[...]).astype(jnp.bfloat16)
    gate = _dot(a, wg_ref[...])
    gate = 1.0 / (1.0 + jnp.exp(-gate))
    o_ref[...] = h1 + gate * _dot(p_ref[0].astype(jnp.bfloat16), wp_ref[...])


def _post(h, ya, yc, z, conv_w, seq, p, layer, g, wa, wb, wc, wg, wp, tm):
    t, d = h.shape
    rows = lambda a: pl.BlockSpec((tm, a.shape[1]), lambda i: (i, 0))
    zcol = lambda c: pl.BlockSpec((tm, CONV_CH), lambda i, c=c: (i, c // CONV_CH))
    halo = lambda c: pl.BlockSpec((8, CONV_CH), lambda i, c=c: (jnp.maximum(i * (tm // 8) - 1, 0), c // CONV_CH))
    whole = lambda a: pl.BlockSpec(a.shape, lambda i: (0, 0), pipeline_mode=pl.Buffered(1))
    return pl.pallas_call(
        functools.partial(_post_kernel, tiles_per_seq=seq // tm),
        out_shape=jax.ShapeDtypeStruct((t, d), jnp.float32),
        grid=(t // tm,),
        in_specs=[rows(h), rows(ya), rows(yc),
                  zcol(Z_CX), zcol(Z_CB), zcol(Z_CC), zcol(Z_CONVZ), halo(Z_CX), halo(Z_CC), whole(conv_w),
                  pl.BlockSpec((1, tm, p.shape[2]), lambda i: (layer, i, 0)), whole(g),
                  whole(wa), whole(wb), whole(wc), whole(wg), whole(wp)],
        out_specs=rows(h),
        compiler_params=_cparams(("parallel",)),
        name="post",
    )(h, ya, yc, z, z, z, z, z, z, conv_w, p, g, wa, wb, wc, wg, wp)


_REGROUP = (
    (Z_CQ, _O_CQ, 512), (Z_CKV, _O_CKV, 512),
    (Z_CX, _O_CX, 512), (Z_CB, _O_CB, 512), (Z_CC, _O_CC, 512), (Z_CONVZ, _O_CONVZ, 512),
    (Z_MLAZ, _O_MLAZ, 768), (Z_DQ, _O_DQ, 768), (Z_DSAZ, _O_DSAZ, 768),
    (Z_DK, _O_DK, 256), (Z_DV, _O_DV, 256), (Z_IQ, _O_IQ, 1024),
)


def _regroup_kernel(w_ref, o_ref):
    bf16 = jnp.bfloat16
    cols = w_ref.shape[2]
    for dst, src, n in _REGROUP + ((Z_KPE_IK, _O_KPE, MLA_ROPE), (Z_KPE_IK + MLA_ROPE, _O_IK, IDX_DIM),
                                   (Z_IW, _O_IW, IDX_HEADS)):
        o_ref[0, dst:dst + n, :] = w_ref[0, src:src + n, :].astype(bf16)
    o_ref[0, Z_IW + IDX_HEADS:Z_IQ, :] = jnp.zeros((Z_IQ - Z_IW - IDX_HEADS, cols), bf16)


def _regroup_w_in(w, tc=TC_REGROUP):
    depth, d, n = w.shape
    assert (d, n) == (D_MODEL, _N_IN), "the regroup table is written for the (D_MODEL, 7056) combined projection"
    return pl.pallas_call(
        _regroup_kernel,
        out_shape=jax.ShapeDtypeStruct((depth, Z_WIDTH, d), jnp.bfloat16),
        grid=(depth, d // tc),
        in_specs=[pl.BlockSpec((1, n, tc), lambda l, i: (l, 0, i))],
        out_specs=pl.BlockSpec((1, Z_WIDTH, tc), lambda l, i: (l, 0, i)),
        compiler_params=_cparams(("parallel", "parallel")),
        name="regroup",
    )(jnp.swapaxes(w, 1, 2))


def _pair_layout(a):
    lead = a.shape[:-1]
    a = a.reshape(lead + (MLA_HEADS, MLA_QK))
    nope, x1, x2 = a[..., :MLA_NOPE], a[..., MLA_NOPE:MLA_NOPE + 32], a[..., MLA_NOPE + 32:]
    return jnp.concatenate([nope, x1, x2, x2, x1], axis=-1).reshape(lead + (MLA_HEADS * 256,))


_B_C32, _B_S32, _B_C16, _B_S16, _B_C8, _B_S8, _B_ONE, _B_ROWS = 0, 32, 64, 80, 96, 104, 112, 128


def _table_selector():
    sel = np.zeros((_B_ROWS, N_TABLES * LANES), np.float32)

    def put(table, lane, base, n, coef=1.0):
        for j in range(n):
            sel[base + (j if base != _B_ONE else 0), table * LANES + lane + j] = coef

    c32, s32, c16, s16, c8, s8, one = _B_C32, _B_S32, _B_C16, _B_S16, _B_C8, _B_S8, _B_ONE
    put(TB_CSQ, 0, c32, 32); put(TB_CSQ, 32, c32, 32); put(TB_CSQ, 64, s32, 32, -1.0); put(TB_CSQ, 96, s32, 32)
    put(TB_C46, 0, c32, 32); put(TB_C46, 32, c32, 32); put(TB_C46, 64, c8, 8); put(TB_C46, 72, c8, 8)
    put(TB_C46, 80, one, 48)
    put(TB_S1A, 0, s32, 32, -1.0)
    put(TB_S2A, 32, s32, 32)
    put(TB_S1B, 64, s8, 8, -1.0)
    put(TB_S2B, 72, s8, 8)
    put(TB_CD, 0, c16, 16); put(TB_CD, 16, c16, 16); put(TB_CD, 32, one, 96)
    put(TB_S1D, 0, s16, 16, -1.0)
    put(TB_S2D, 16, s16, 16)
    qs = IDX_DIM ** -0.5
    for off in (0, 64):
        put(TB_CI, off, c8, 8, qs); put(TB_CI, off + 8, c8, 8, qs); put(TB_CI, off + 16, one, 48, qs)
        put(TB_S1I, off, s8, 8, -qs)
        put(TB_S2I, off + 8, s8, 8, qs)
    return sel


def _rope_basis(positions):
    f32 = jnp.float32
    pos = positions.reshape(-1, 1).astype(f32)
    freqs = [ROPE_THETA ** (-jnp.arange(half, dtype=f32) / half) for half in (MLA_ROPE // 2, DSA_ROT // 2, IDX_ROT // 2)]
    inv = jnp.concatenate([freqs[0], freqs[0], freqs[1], freqs[1], freqs[2], freqs[2], jnp.zeros((_B_ROWS - _B_ONE,), f32)])
    is_cos = np.zeros((_B_ROWS,), bool)
    for start, stop in ((_B_C32, _B_S32), (_B_C16, _B_S16), (_B_C8, _B_S8), (_B_ONE, _B_ONE + 1)):
        is_cos[start:stop] = True
    ang = pos * inv[None, :]
    basis = jnp.where(jnp.asarray(is_cos)[None, :], jnp.cos(ang), jnp.sin(ang))
    return basis, jnp.asarray(_table_selector(), jnp.bfloat16)


def _layer(h, layer, p_all, w_in_all, tabs, batch, seq, k_top, norm_in, mla_gq, mla_w_uq, mla_gkv, mla_w_ukv, mla_qn,
           mla_kn, conv_w, dsa_qn, dsa_kn, w_out, ple_norm, ple_w_gate, ple_w_proj):
    f32, bf16 = jnp.float32, jnp.bfloat16
    t = h.shape[0]

    z = _inproj(h, norm_in[None, :], w_in_all, layer, min(TM_INPROJ, t), TN_INPROJ)

    gq = _pair_layout(jnp.tile(mla_qn, MLA_HEADS))[None, :] * (MLA_QK ** -0.5 * LOG2E)
    g46 = jnp.concatenate([mla_kn[MLA_NOPE:], jnp.ones((64,), f32)])[None, :]
    qm, km, vm, qd, kd, iqp, ikd = _prep(
        z, *tabs, mla_gq[None, :], mla_gkv[None, :], _pair_layout(mla_w_uq).astype(bf16), mla_w_ukv.astype(bf16),
        gq, mla_kn[None, :MLA_NOPE], g46, dsa_qn[None, :] * (DSA_HEAD_DIM ** -0.5 * LOG2E), dsa_kn[None, :],
        min(TM_PREP, t))

    y_a = _mla(qm, km, vm, z, batch, seq, TK_MLA)
    y_c = _dsa(iqp, ikd, qd, kd, z, batch, seq, TQ_DSA, k_top)

    wo = w_out.astype(bf16)
    return _post(h, y_a, y_c, z, jnp.pad(conv_w, ((0, 8 - CONV_K), (0, 0))), seq, p_all, layer, ple_norm[None, :],
                 wo[:MLA_WIDTH], wo[MLA_WIDTH:MLA_WIDTH + CONV_CH], wo[MLA_WIDTH + CONV_CH:],
                 ple_w_gate.astype(bf16), ple_w_proj.astype(bf16), min(TM_POST, t))


def kernel(x, p, positions, norm_in, w_in, mla_gq, mla_w_uq, mla_gkv, mla_w_ukv, mla_qn, mla_kn, conv_w, dsa_qn,
           dsa_kn, w_out, ple_norm, ple_w_gate, ple_w_proj):
    batch, seq, d = x.shape
    depth = p.shape[0]
    t = batch * seq
    k_top = min(TOPK_MAX, seq // 4)
    tabs = _rope_basis(positions)
    w_in_z = _regroup_w_in(w_in)
    h = x.reshape(t, d)
    p_all = p.reshape(depth, t, PLE_DIM)
    for i in range(depth):
        h = _layer(h, i, p_all, w_in_z, tabs, batch, seq, k_top, norm_in[i], mla_gq[i],
                   mla_w_uq[i], mla_gkv[i], mla_w_ukv[i], mla_qn[i], mla_kn[i], conv_w[i], dsa_qn[i], dsa_kn[i],
                   w_out[i], ple_norm[i], ple_w_gate[i], ple_w_proj[i])
    return h.reshape(batch, seq, d)
```
